```python
import jax, jax.numpy as jnp
from jax import lax
import numpy as np

D_MODEL = 4096
BATCH = 2
SEQ = 4096
DEPTH = 2

CTX_LEN = 256
GRID_W = 64
MIX_WIDTH = D_MODEL
N_MOD = 6
EPS = 1e-6
N_DIRS = 2
MLSTM_HEADS = 4
MLSTM_WIDTH = D_MODEL // 2
MLSTM_DV = MLSTM_WIDTH // MLSTM_HEADS
MLSTM_DK = MLSTM_DV // 2
MLSTM_CHUNK = 128
MLSTM_QK_WIDTH = MLSTM_HEADS * MLSTM_DK
MLSTM_GATE_COLS = N_DIRS * 2 * MLSTM_HEADS
GMLP_HEADS = 4
GMLP_WIDTH = D_MODEL // 4
GMLP_HEAD_DIM = GMLP_WIDTH // GMLP_HEADS
GMLP_CHUNK = 128
CONV_WIDTH = D_MODEL // 4
CONV_K = 3
N_EXPERTS = 16
N_EXPERT_GROUPS = 4
EXPERTS_PER_GROUP = N_EXPERTS // N_EXPERT_GROUPS
TOP_K = 2
D_FF_EXPERT = D_MODEL // 4
IN_SIZES = (MLSTM_QK_WIDTH, MLSTM_QK_WIDTH, MLSTM_WIDTH, MLSTM_WIDTH, MLSTM_GATE_COLS, GMLP_WIDTH, GMLP_WIDTH, CONV_WIDTH, CONV_WIDTH, CONV_WIDTH)
IN_COLS = sum(IN_SIZES)

kernel_name = 'hybrid_mlstm_gmlp_conv_moe_dit'


def rms_norm(x, g):
    xf = x.astype(jnp.float32)
    y = xf * lax.rsqrt(jnp.mean(xf * xf, axis=-1, keepdims=True) + EPS)
    return (y * g.astype(jnp.float32)).astype(x.dtype)


def modulate(h, shift, scale):
    return h * (1 + scale) + shift


def split_cols(z):
    outs, start = [], 0
    for n in IN_SIZES:
        outs.append(z[..., start:start + n])
        start += n
    return outs


def zero_mlstm_state(bsz):
    f32 = jnp.float32
    return (jnp.zeros((bsz, MLSTM_HEADS, MLSTM_DK, MLSTM_DV), f32),
            jnp.zeros((bsz, MLSTM_HEADS, MLSTM_DK), f32),
            jnp.zeros((bsz, MLSTM_HEADS), f32))


def mlstm_chunkwise(q, k, v, i_pre, f_pre, state):
    bsz, nh, t, _ = q.shape
    n_chunks = t // MLSTM_CHUNK

    def to_chunks(a):
        a = a.reshape((bsz, nh, n_chunks, MLSTM_CHUNK) + a.shape[3:])
        return jnp.moveaxis(a, 2, 0)

    log_f = jax.nn.log_sigmoid(f_pre)
    mask = jnp.tril(jnp.ones((MLSTM_CHUNK, MLSTM_CHUNK), dtype=bool))

    def step(carry, inp):
        c_prev, n_prev, m_prev = carry
        qc, kc, vc, ic, lfc = inp
        b = jnp.cumsum(lfc, axis=-1)
        d = jnp.where(mask, b[..., :, None] - b[..., None, :] + ic[..., None, :], -jnp.inf)
        m_inter = b + m_prev[..., None]
        m_t = jnp.maximum(jnp.max(d, axis=-1), m_inter)
        s = jnp.einsum('bhtd,bhsd->bhts', qc, kc) * jnp.exp(d - m_t[..., None])
        inter = jnp.exp(m_inter - m_t)
        num = jnp.einsum('bhts,bhsv->bhtv', s, vc) + inter[..., None] * jnp.einsum('bhtd,bhdv->bhtv', qc, c_prev)
        den = jnp.sum(s, axis=-1) + inter * jnp.einsum('bhtd,bhd->bht', qc, n_prev)
        h = num / jnp.maximum(jnp.abs(den), jnp.exp(-m_t))[..., None]
        b_last = b[..., -1]
        g = b_last[..., None] - b + ic
        m_new = jnp.maximum(b_last + m_prev, jnp.max(g, axis=-1))
        decay = jnp.exp(b_last + m_prev - m_new)
        wk = jnp.exp(g - m_new[..., None])
        c_new = decay[..., None, None] * c_prev + jnp.einsum('bhs,bhsd,bhsv->bhdv', wk, kc, vc)
        n_new = decay[..., None] * n_prev + jnp.einsum('bhs,bhsd->bhd', wk, kc)
        return (c_new, n_new, m_new), h

    final, h = lax.scan(step, state, (to_chunks(q), to_chunks(k), to_chunks(v), to_chunks(i_pre), to_chunks(log_f)))
    h = jnp.moveaxis(h, 0, 2).reshape(bsz, nh, t, -1)
    return h, final


def mlstm_mixer(q, k, v, o, gate_pre, b_gate, norm_g, init_states):
    bsz, t, _ = q.shape
    f32 = jnp.float32

    def heads(a, dh):
        return a.reshape(bsz, t, MLSTM_HEADS, dh).transpose(0, 2, 1, 3).astype(f32)

    qh = heads(q, MLSTM_DK) * (MLSTM_DK ** -0.5)
    kh = heads(k, MLSTM_DK)
    vh = heads(v, MLSTM_DV)
    g = gate_pre.astype(f32).reshape(bsz, t, N_DIRS, 2, MLSTM_HEADS) + b_gate.astype(f32)
    g = g.transpose(2, 3, 0, 4, 1)
    if init_states is None:
        init_states = (zero_mlstm_state(bsz), zero_mlstm_state(bsz))

    def flip(a):
        return jnp.flip(a, axis=2)

    h_f, s_f = mlstm_chunkwise(qh, kh, vh, g[0, 0], g[0, 1], init_states[0])
    h_b, s_b = mlstm_chunkwise(flip(qh), flip(kh), flip(vh), flip(g[1, 0]), flip(g[1, 1]), init_states[1])
    h = h_f + flip(h_b)
    h = h * lax.rsqrt(jnp.mean(h * h, axis=-1, keepdims=True) + EPS)
    h = h.transpose(0, 2, 1, 3).reshape(bsz, t, MLSTM_WIDTH) * norm_g.astype(f32)
    out = h * jax.nn.sigmoid(o.astype(f32))
    return out.astype(q.dtype), (s_f, s_b)


def gmlp_mixer(u, v, w_s, b_s, norm_g):
    bsz, t, _ = u.shape
    u = jax.nn.gelu(u)
    v = rms_norm(jax.nn.gelu(v), norm_g)
    vc = v.reshape(bsz, t // GMLP_CHUNK, GMLP_CHUNK, GMLP_HEADS, GMLP_HEAD_DIM)
    s = jnp.einsum('hts,bnshd->bnthd', w_s, vc) + b_s.T[None, None, :, :, None]
    return u * s.reshape(bsz, t, GMLP_WIDTH)


def centred_conv3(y, w):
    pad = [(0, 0)] * (y.ndim - 2) + [(1, 1), (0, 0)]
    yp = jnp.pad(y, pad)
    return w[0] * yp[..., :-2, :] + w[1] * yp[..., 1:-1, :] + w[2] * yp[..., 2:, :]


def conv_mixer(bg, cg, xin, w, rows):
    y = cg * xin
    if rows is None:
        y = centred_conv3(y, w)
    else:
        bsz, t, ch = y.shape
        y = centred_conv3(y.reshape(bsz, rows, GRID_W, ch), w).reshape(bsz, t, ch)
    return bg * y


def moe_ffn(h, w_router, b_router, w_gate, w_up, w_down):
    shp = h.shape
    hf = h.reshape(-1, shp[-1])
    n = hf.shape[0]
    f32 = jnp.float32
    scores = jax.nn.sigmoid(jnp.dot(hf.astype(f32), w_router.astype(f32)))
    sel = (scores + b_router.astype(f32)).reshape(n, N_EXPERT_GROUPS, EXPERTS_PER_GROUP)
    group_score = jnp.sum(lax.top_k(sel, TOP_K)[0], axis=-1)
    top_group = lax.top_k(group_score, 1)[1][:, 0]
    in_group = sel[jnp.arange(n), top_group]
    local = lax.top_k(in_group, TOP_K)[1]
    expert_idx = top_group[:, None] * EXPERTS_PER_GROUP + local
    w_sel = jnp.take_along_axis(scores, expert_idx, axis=1)
    w_sel = w_sel / jnp.sum(w_sel, axis=-1, keepdims=True)
    combine = jnp.sum(jax.nn.one_hot(expert_idx, N_EXPERTS, dtype=f32) * w_sel[..., None], axis=1)
    combine = combine.astype(h.dtype)
    out = jnp.zeros_like(hf)
    for e in range(N_EXPERTS):
        a = jax.nn.silu(hf @ w_gate[e]) * (hf @ w_up[e])
        out = out + combine[:, e:e + 1] * (a @ w_down[e])
    return out.reshape(shp)


def setup_inputs(seed: int = 0) -> dict:
    key = jax.random.key(seed)
    ks = jax.random.split(key, 24)
    f32 = jnp.float32

    def nrm(k, shape, scale):
        return jax.random.normal(k, shape, f32) * scale

    b_i = nrm(ks[10], (DEPTH, N_DIRS, 1, MLSTM_HEADS), 0.1)
    b_f = jnp.linspace(3.0, 6.0, MLSTM_HEADS, dtype=f32) + nrm(ks[11], (DEPTH, N_DIRS, 1, MLSTM_HEADS), 0.1)
    return {
        'x': nrm(ks[0], (BATCH, SEQ, D_MODEL), 1.0),
        'c': nrm(ks[1], (BATCH, D_MODEL), 1.0),
        'ctx': nrm(ks[2], (BATCH, CTX_LEN, D_MODEL), 1.0),
        'c_ctx': nrm(ks[3], (D_MODEL,), 1.0),
        'w_mod': nrm(ks[4], (DEPTH, D_MODEL, N_MOD * D_MODEL), 0.5 * D_MODEL ** -0.5),
        'b_mod': nrm(ks[5], (DEPTH, N_MOD * D_MODEL), 0.02),
        'norm1_g': 1.0 + nrm(ks[6], (DEPTH, D_MODEL), 0.02),
        'norm2_g': 1.0 + nrm(ks[7], (DEPTH, D_MODEL), 0.02),
        'w_in': nrm(ks[8], (DEPTH, D_MODEL, IN_COLS), D_MODEL ** -0.5),
        'b_gates': jnp.concatenate([b_i, b_f], axis=2),
        'mlstm_norm_g': 1.0 + nrm(ks[12], (DEPTH, MLSTM_WIDTH), 0.02),
        'gmlp_ws': nrm(ks[13], (DEPTH, GMLP_HEADS, GMLP_CHUNK, GMLP_CHUNK), GMLP_CHUNK ** -0.5),
        'gmlp_bs': 1.0 + nrm(ks[14], (DEPTH, GMLP_HEADS, GMLP_CHUNK), 0.02),
        'gmlp_norm_g': 1.0 + nrm(ks[15], (DEPTH, GMLP_WIDTH), 0.02),
        'conv_w': nrm(ks[16], (DEPTH, CONV_K, CONV_WIDTH), CONV_K ** -0.5),
        'w_out': nrm(ks[17], (DEPTH, MIX_WIDTH, D_MODEL), MIX_WIDTH ** -0.5),
        'w_router': nrm(ks[18], (D_MODEL, N_EXPERTS), D_MODEL ** -0.5),
        'b_router': nrm(ks[19], (N_EXPERTS,), 0.01),
        'w_gate_e': nrm(ks[20], (DEPTH, N_EXPERTS, D_MODEL, D_FF_EXPERT), D_MODEL ** -0.5),
        'w_up_e': nrm(ks[21], (DEPTH, N_EXPERTS, D_MODEL, D_FF_EXPERT), D_MODEL ** -0.5),
        'w_down_e': nrm(ks[22], (DEPTH, N_EXPERTS, D_FF_EXPERT, D_MODEL), D_FF_EXPERT ** -0.5),
        'final_g': 1.0 + nrm(ks[23], (D_MODEL,), 0.02),
    }


def reference(x, c, ctx, c_ctx, w_mod, b_mod, norm1_g, norm2_g, w_in, b_gates, mlstm_norm_g, gmlp_ws, gmlp_bs, gmlp_norm_g, conv_w, w_out, w_router, b_router, w_gate_e, w_up_e, w_down_e, final_g):
    rows = x.shape[1] // GRID_W
    for layer in range(DEPTH):
        mod_x = jnp.split((jax.nn.silu(c) @ w_mod[layer] + b_mod[layer])[:, None, :], N_MOD, axis=-1)
        mod_c = jnp.split(jax.nn.silu(c_ctx) @ w_mod[layer] + b_mod[layer], N_MOD, axis=-1)
        hx = modulate(rms_norm(x, norm1_g[layer]), mod_x[0], mod_x[1])
        hc = modulate(rms_norm(ctx, norm1_g[layer]), mod_c[0], mod_c[1])
        px = split_cols(hx @ w_in[layer])
        pc = split_cols(hc @ w_in[layer])
        mc, ctx_states = mlstm_mixer(pc[0], pc[1], pc[2], pc[3], pc[4], b_gates[layer], mlstm_norm_g[layer], None)
        mx, _ = mlstm_mixer(px[0], px[1], px[2], px[3], px[4], b_gates[layer], mlstm_norm_g[layer], ctx_states)
        yx = jnp.concatenate([
            mx,
            gmlp_mixer(px[5], px[6], gmlp_ws[layer], gmlp_bs[layer], gmlp_norm_g[layer]),
            conv_mixer(px[7], px[8], px[9], conv_w[layer], rows),
        ], axis=-1) @ w_out[layer]
        x_mix = x + mod_x[2] * yx
        x_new = x_mix + mod_x[5] * moe_ffn(modulate(rms_norm(x_mix, norm2_g[layer]), mod_x[3], mod_x[4]),
                                           w_router, b_router, w_gate_e[layer], w_up_e[layer], w_down_e[layer])
        if layer + 1 < DEPTH:
            yc = jnp.concatenate([
                mc,
                gmlp_mixer(pc[5], pc[6], gmlp_ws[layer], gmlp_bs[layer], gmlp_norm_g[layer]),
                conv_mixer(pc[7], pc[8], pc[9], conv_w[layer], None),
            ], axis=-1) @ w_out[layer]
            ctx_mix = ctx + mod_c[2] * yc
            ctx = ctx_mix + mod_c[5] * moe_ffn(modulate(rms_norm(ctx_mix, norm2_g[layer]), mod_c[3], mod_c[4]),
                                               w_router, b_router, w_gate_e[layer], w_up_e[layer], w_down_e[layer])
        x = x_new
    return rms_norm(x, final_g)
```

```python
import functools

import jax
import jax.numpy as jnp
from jax import lax
from jax.experimental import pallas as pl
from jax.experimental.pallas import tpu as pltpu

F32 = jnp.float32
BF16 = jnp.bfloat16

N_MOD = 6
EPS = 1e-6
N_DIRS = 2
MLSTM_HEADS = 4
MLSTM_CHUNK = 128
GMLP_HEADS = 4
GMLP_CHUNK = 128
GRID_W = 64
N_EXPERT_GROUPS = 4
TOP_K = 2

V7X_LANES = 128
V7X_SUBLANES = 8
V7X_VMEM_BYTES = 64 * 1024 * 1024
VMEM_LIMIT = V7X_VMEM_BYTES - 8 * 1024 * 1024

ROW_TILE = 256
MM_ROW_TILE = 512
MOE_ROW_TILE = 256


def _params(n_axes):
    return pltpu.CompilerParams(
        dimension_semantics=("arbitrary",) * n_axes, vmem_limit_bytes=VMEM_LIMIT)


def _largest_divisor(n, candidates):
    for c in candidates:
        if n % c == 0:
            return c
    raise ValueError(f"no tile in {candidates} divides {n}")


def _mod_row(i, tile, n_lat_rows, seq, n_batch):
    return jnp.where(i * tile < n_lat_rows, (i * tile) // seq, n_batch)


def _mod_kernel(a_ref, w_ref, b_ref, o_ref):
    a = a_ref[...]
    a = a * jax.nn.sigmoid(a)
    acc = jnp.dot(a.astype(BF16), w_ref[...].astype(BF16), preferred_element_type=F32)
    o_ref[...] = acc + b_ref[...]


def _mod_table(a, w_mod, b_mod):
    depth, d, n = w_mod.shape
    tn = _largest_divisor(n, (1024, 512, 256, 128))
    rows = a.shape[0]
    return pl.pallas_call(
        _mod_kernel,
        grid=(depth, n // tn),
        in_specs=[
            pl.BlockSpec((rows, d), lambda l, j: (0, 0)),
            pl.BlockSpec((None, d, tn), lambda l, j: (l, 0, j)),
            pl.BlockSpec((None, 1, tn), lambda l, j: (l, 0, j)),
        ],
        out_specs=pl.BlockSpec((None, rows, tn), lambda l, j: (l, 0, j)),
        out_shape=jax.ShapeDtypeStruct((depth, rows, n), F32),
        compiler_params=_params(2),
        name="mod_table",
    )(a, w_mod, b_mod.reshape(depth, 1, n))


def _norm_modulate(x, g, shift, scale):
    y = x * lax.rsqrt(jnp.mean(x * x, axis=-1, keepdims=True) + EPS)
    return (y * g) * (1.0 + scale) + shift


def _normmod_kernel(x_ref, g_ref, sh_ref, sc_ref, o_ref):
    o_ref[...] = _norm_modulate(x_ref[...], g_ref[...], sh_ref[...], sc_ref[...]).astype(o_ref.dtype)


def _normmod(rows, g, mod3, shift_col, n_lat_rows, seq, n_batch):
    r, d = rows.shape
    mrow = functools.partial(_mod_row, tile=ROW_TILE, n_lat_rows=n_lat_rows, seq=seq, n_batch=n_batch)
    return pl.pallas_call(
        _normmod_kernel,
        grid=(r // ROW_TILE,),
        in_specs=[
            pl.BlockSpec((ROW_TILE, d), lambda i: (i, 0)),
            pl.BlockSpec((1, d), lambda i: (0, 0)),
            pl.BlockSpec((None, 1, d), lambda i: (mrow(i), 0, shift_col)),
            pl.BlockSpec((None, 1, d), lambda i: (mrow(i), 0, shift_col + 1)),
        ],
        out_specs=pl.BlockSpec((ROW_TILE, d), lambda i: (i, 0)),
        out_shape=jax.ShapeDtypeStruct((r, d), BF16),
        compiler_params=_params(1),
        name="norm_modulate",
    )(rows, g.reshape(1, d), mod3, mod3)


def _mm_kernel(a_ref, b_ref, o_ref):
    o_ref[...] = jnp.dot(a_ref[...], b_ref[...], preferred_element_type=F32).astype(o_ref.dtype)


def _matmul(a, b, out_dtype):
    m, k = a.shape
    n = b.shape[1]
    tm = _largest_divisor(m, (MM_ROW_TILE, 256, 128))
    tn = _largest_divisor(n, (1024, 512, 256, 128))
    return pl.pallas_call(
        _mm_kernel,
        grid=(n // tn, m // tm),
        in_specs=[
            pl.BlockSpec((tm, k), lambda j, i: (i, 0)),
            pl.BlockSpec((k, tn), lambda j, i: (0, j)),
        ],
        out_specs=pl.BlockSpec((tm, tn), lambda j, i: (i, j)),
        out_shape=jax.ShapeDtypeStruct((m, n), out_dtype),
        compiler_params=_params(2),
        name="projection",
    )(a, b)


def _log_sigmoid(x):
    return jnp.minimum(x, 0.0) - jnp.log1p(jnp.exp(-jnp.abs(x)))


def _mlstm_kernel(q_ref, k_ref, v_ref, g_ref, gt_ref, b_ref, bt_ref, o_ref, c_scr, m_scr, *, n_heads):
    d = pl.program_id(0)
    h = pl.program_id(2)
    step = pl.program_id(3)
    chunk, dk = q_ref.shape
    dv = v_ref.shape[1]

    @pl.when(step == 0)
    def _():
        c_scr[...] = jnp.zeros_like(c_scr)
        m_scr[...] = jnp.zeros_like(m_scr)

    col_i = d * (2 * n_heads) + h
    col_f = col_i + n_heads
    g = g_ref[...] + b_ref[...]
    lane = lax.broadcasted_iota(jnp.int32, g.shape, 1)
    i_col = jnp.sum(jnp.where(lane == col_i, g, 0.0), axis=1, keepdims=True)
    f_col = jnp.sum(jnp.where(lane == col_f, g, 0.0), axis=1, keepdims=True)
    gt = gt_ref[...] + bt_ref[...]
    sub = lax.broadcasted_iota(jnp.int32, gt.shape, 0)
    i_row = jnp.sum(jnp.where(sub == col_i, gt, 0.0), axis=0, keepdims=True)
    f_row = jnp.sum(jnp.where(sub == col_f, gt, 0.0), axis=0, keepdims=True)
    lf_col = _log_sigmoid(f_col)
    lf_row = _log_sigmoid(f_row)

    t_idx = lax.broadcasted_iota(jnp.int32, (chunk, chunk), 0)
    s_idx = lax.broadcasted_iota(jnp.int32, (chunk, chunk), 1)
    sign = 1 - 2 * d
    before = (s_idx - t_idx) * sign <= 0
    after = (t_idx - s_idx) * sign <= 0
    b_col = jnp.sum(jnp.where(before, lf_row, 0.0), axis=1, keepdims=True)
    b_row = jnp.sum(jnp.where(after, lf_col, 0.0), axis=0, keepdims=True)
    b_last = jnp.sum(lf_row, axis=1, keepdims=True)

    m_prev = m_scr[0:1, 0:1]
    dmat = jnp.where(before, b_col - b_row + i_row, -jnp.inf)
    m_inter = b_col + m_prev
    m_t = jnp.maximum(jnp.max(dmat, axis=1, keepdims=True), m_inter)
    q = (q_ref[...].astype(F32) * (dk ** -0.5)).astype(BF16)
    k = k_ref[...]
    qk = lax.dot_general(q, k, (((1,), (1,)), ((), ())), preferred_element_type=F32)
    s = (qk * jnp.exp(dmat - m_t)).astype(BF16)
    v_aug = jnp.concatenate([v_ref[...], jnp.ones((chunk, V7X_LANES), BF16)], axis=1)
    c_prev = c_scr[...]
    inter = jnp.exp(m_inter - m_t)
    num = (jnp.dot(s, v_aug, preferred_element_type=F32)
           + inter * jnp.dot(q, c_prev.astype(BF16), preferred_element_type=F32))
    den = num[:, dv:dv + 1]
    o_ref[...] = num[:, :dv] / jnp.maximum(jnp.abs(den), jnp.exp(-m_t))

    g_end = b_last - b_col + i_col
    m_new = jnp.maximum(b_last + m_prev, jnp.max(g_end, axis=0, keepdims=True))
    decay = jnp.exp(b_last + m_prev - m_new)
    wv = (jnp.exp(g_end - m_new) * v_aug.astype(F32)).astype(BF16)
    c_scr[...] = decay * c_prev + lax.dot_general(k, wv, (((0,), (0,)), ((), ())), preferred_element_type=F32)
    m_scr[...] = jnp.broadcast_to(m_new, m_scr.shape)


def _mlstm(z, gates, gates_t, bias, bias_t, n_batch, seq, ctx_len, width):
    r = z.shape[0]
    nh = MLSTM_HEADS
    dv = width // nh
    dk = dv // 2
    chunk = MLSTM_CHUNK
    n_ctx = ctx_len // chunk
    n_lat = seq // chunk
    k_block0 = (nh * dk) // dk
    v_block0 = (2 * nh * dk) // dv

    def tile(d, b, step):
        lat = step - n_ctx
        ctx_tile = n_batch * n_lat + b * n_ctx + jnp.where(d == 0, step, n_ctx - 1 - step)
        lat_tile = b * n_lat + jnp.where(d == 0, lat, n_lat - 1 - lat)
        return jnp.where(step < n_ctx, ctx_tile, lat_tile)

    return pl.pallas_call(
        functools.partial(_mlstm_kernel, n_heads=nh),
        grid=(N_DIRS, n_batch, nh, n_ctx + n_lat),
        in_specs=[
            pl.BlockSpec((chunk, dk), lambda d, b, h, s: (tile(d, b, s), h)),
            pl.BlockSpec((chunk, dk), lambda d, b, h, s: (tile(d, b, s), k_block0 + h)),
            pl.BlockSpec((chunk, dv), lambda d, b, h, s: (tile(d, b, s), v_block0 + h)),
            pl.BlockSpec((chunk, gates.shape[1]), lambda d, b, h, s: (tile(d, b, s), 0)),
            pl.BlockSpec((gates_t.shape[0], chunk), lambda d, b, h, s: (0, tile(d, b, s))),
            pl.BlockSpec(bias.shape, lambda d, b, h, s: (0, 0)),
            pl.BlockSpec(bias_t.shape, lambda d, b, h, s: (0, 0)),
        ],
        out_specs=pl.BlockSpec((None, chunk, dv), lambda d, b, h, s: (d, tile(d, b, s), h)),
        out_shape=jax.ShapeDtypeStruct((N_DIRS, r, width), F32),
        scratch_shapes=[
            pltpu.VMEM((dk, dv + V7X_LANES), F32),
            pltpu.VMEM((V7X_SUBLANES, V7X_LANES), F32),
        ],
        compiler_params=_params(4),
        name="mlstm_scan",
    )(z, z, z, gates, gates_t, bias, bias_t)


def _mix_kernel(hd_ref, o_ref, u_ref, v_ref, bg_ref, cg_ref, xin_ref, mg_ref, ws_ref, bst_ref, gg_ref,
                cw_ref, y_ref, *, n_lat_rows, ctx_len):
    i = pl.program_id(0)
    tm = y_ref.shape[0]
    wm = o_ref.shape[1]
    gw = u_ref.shape[1]
    dv = wm // MLSTM_HEADS
    hdim = gw // GMLP_HEADS

    for hd in range(MLSTM_HEADS):
        cols = slice(hd * dv, (hd + 1) * dv)
        hh = hd_ref[0, :, cols] + hd_ref[1, :, cols]
        hh = hh * lax.rsqrt(jnp.mean(hh * hh, axis=-1, keepdims=True) + EPS)
        out = (hh * mg_ref[:, cols]) * jax.nn.sigmoid(o_ref[:, cols].astype(F32))
        y_ref[:, cols] = out.astype(y_ref.dtype)

    u = jax.nn.gelu(u_ref[...].astype(F32))
    v = jax.nn.gelu(v_ref[...].astype(F32))
    v = (v * lax.rsqrt(jnp.mean(v * v, axis=-1, keepdims=True) + EPS)) * gg_ref[...]
    vb = v.astype(BF16)
    for c in range(tm // GMLP_CHUNK):
        rows = slice(c * GMLP_CHUNK, (c + 1) * GMLP_CHUNK)
        for hd in range(GMLP_HEADS):
            cols = slice(hd * hdim, (hd + 1) * hdim)
            sp = jnp.dot(ws_ref[hd], vb[rows, cols], preferred_element_type=F32) + bst_ref[:, hd:hd + 1]
            y_ref[rows, wm + hd * hdim:wm + (hd + 1) * hdim] = (u[rows, cols] * sp).astype(y_ref.dtype)

    yv = cg_ref[...].astype(F32) * xin_ref[...].astype(F32)
    t = lax.broadcasted_iota(jnp.int32, (tm, 1), 0)
    is_lat = i * tm < n_lat_rows
    pos = jnp.where(is_lat, t % GRID_W, t % ctx_len)
    last = jnp.where(is_lat, GRID_W - 1, ctx_len - 1)
    prev = jnp.where(pos != 0, pltpu.roll(yv, 1, axis=0), 0.0)
    nxt = jnp.where(pos != last, pltpu.roll(yv, tm - 1, axis=0), 0.0)
    conv = cw_ref[0:1, :] * prev + cw_ref[1:2, :] * yv + cw_ref[2:3, :] * nxt
    y_ref[:, wm + gw:] = (bg_ref[...].astype(F32) * conv).astype(y_ref.dtype)


def _mix(hdirs, z, mg, ws, bst, gg, cw, n_rows, n_lat_rows, ctx_len):
    wm = mg.shape[1]
    gw = gg.shape[1]
    cwid = cw.shape[1]
    tm = ROW_TILE
    assert tm % GRID_W == 0 and tm % ctx_len == 0 and tm % GMLP_CHUNK == 0
    assert gw == cwid and wm % gw == 0
    o_blk = (2 * wm // 2) // wm
    o_blk = (wm // 2 + wm // 2 + wm) // wm
    g0 = (3 * wm) // gw
    full = lambda i: (0, 0)
    return pl.pallas_call(
        functools.partial(_mix_kernel, n_lat_rows=n_lat_rows, ctx_len=ctx_len),
        grid=(n_rows // tm,),
        in_specs=[
            pl.BlockSpec((N_DIRS, tm, wm), lambda i: (0, i, 0)),
            pl.BlockSpec((tm, wm), lambda i: (i, o_blk)),
            pl.BlockSpec((tm, gw), lambda i: (i, g0)),
            pl.BlockSpec((tm, gw), lambda i: (i, g0 + 1)),
            pl.BlockSpec((tm, gw), lambda i: (i, g0 + 2)),
            pl.BlockSpec((tm, gw), lambda i: (i, g0 + 3)),
            pl.BlockSpec((tm, gw), lambda i: (i, g0 + 4)),
            pl.BlockSpec(mg.shape, full),
            pl.BlockSpec(ws.shape, lambda i: (0, 0, 0)),
            pl.BlockSpec(bst.shape, full),
            pl.BlockSpec(gg.shape, full),
            pl.BlockSpec(cw.shape, full),
        ],
        out_specs=pl.BlockSpec((tm, wm + gw + cwid), lambda i: (i, 0)),
        out_shape=jax.ShapeDtypeStruct((n_rows, wm + gw + cwid), BF16),
        compiler_params=_params(1),
        name="mixer_epilogue",
    )(hdirs, z, z, z, z, z, z, mg, ws, bst, gg, cw)


def _wout_kernel(y_ref, w_ref, x_ref, g_ref, o_ref):
    acc = jnp.dot(y_ref[...], w_ref[...], preferred_element_type=F32)
    o_ref[...] = x_ref[...] + g_ref[...] * acc


def _wout(y, w, rows, mod3, n_lat_rows, seq, n_batch):
    m, k = y.shape
    d = w.shape[1]
    tm = _largest_divisor(m, (MM_ROW_TILE, 256))
    tn = _largest_divisor(d, (1024, 512, 256, 128))
    mrow = functools.partial(_mod_row, tile=tm, n_lat_rows=n_lat_rows, seq=seq, n_batch=n_batch)
    gate_blk0 = (2 * d) // tn
    return pl.pallas_call(
        _wout_kernel,
        grid=(d // tn, m // tm),
        in_specs=[
            pl.BlockSpec((tm, k), lambda j, i: (i, 0)),
            pl.BlockSpec((k, tn), lambda j, i: (0, j)),
            pl.BlockSpec((tm, tn), lambda j, i: (i, j)),
            pl.BlockSpec((None, 1, tn), lambda j, i: (mrow(i), 0, gate_blk0 + j)),
        ],
        out_specs=pl.BlockSpec((tm, tn), lambda j, i: (i, j)),
        out_shape=jax.ShapeDtypeStruct((m, d), F32),
        compiler_params=_params(2),
        name="out_projection",
    )(y, w, rows, mod3)


def _router_kernel(x_ref, g_ref, sh_ref, sc_ref, wr_ref, br_ref, h_ref, w_ref, e_ref, *, n_experts):
    h = _norm_modulate(x_ref[...], g_ref[...], sh_ref[...], sc_ref[...])
    h_ref[...] = h
    logits = jnp.dot(h, wr_ref[...], precision=lax.Precision.HIGHEST, preferred_element_type=F32)
    scores = jax.nn.sigmoid(logits)
    sel = scores + br_ref[...]
    epg = n_experts // N_EXPERT_GROUPS
    sel_c = [sel[:, e:e + 1] for e in range(n_experts)]
    sc_c = [scores[:, e:e + 1] for e in range(n_experts)]

    best_g = None
    for gi in range(N_EXPERT_GROUPS):
        mem = sel_c[gi * epg:(gi + 1) * epg]
        gs = None
        for a in range(epg):
            for b in range(a + 1, epg):
                pair = mem[a] + mem[b]
                gs = pair if gs is None else jnp.maximum(gs, pair)
        if best_g is None:
            best_g, gidx = gs, jnp.zeros_like(gs, dtype=jnp.int32)
        else:
            upd = gs > best_g
            gidx = jnp.where(upd, gi, gidx)
            best_g = jnp.where(upd, gs, best_g)

    in_sel, in_sc = [], []
    for a in range(epg):
        vs, vr = sel_c[a], sc_c[a]
        for gi in range(1, N_EXPERT_GROUPS):
            vs = jnp.where(gidx == gi, sel_c[gi * epg + a], vs)
            vr = jnp.where(gidx == gi, sc_c[gi * epg + a], vr)
        in_sel.append(vs)
        in_sc.append(vr)

    v1, i1, s1 = in_sel[0], jnp.zeros_like(gidx), in_sc[0]
    for a in range(1, epg):
        upd = in_sel[a] > v1
        i1 = jnp.where(upd, a, i1)
        s1 = jnp.where(upd, in_sc[a], s1)
        v1 = jnp.where(upd, in_sel[a], v1)
    v2 = jnp.full_like(v1, -jnp.inf)
    i2 = jnp.zeros_like(gidx)
    s2 = jnp.zeros_like(s1)
    for a in range(epg):
        upd = jnp.logical_and(i1 != a, in_sel[a] > v2)
        i2 = jnp.where(upd, a, i2)
        s2 = jnp.where(upd, in_sc[a], s2)
        v2 = jnp.where(upd, in_sel[a], v2)
    denom = s1 + s2
    lane = lax.broadcasted_iota(jnp.int32, w_ref.shape, 1)
    w_ref[...] = jnp.where(lane == 0, s1 / denom, jnp.where(lane == 1, s2 / denom, 0.0))
    e_ref[...] = jnp.where(lane == 0, gidx * epg + i1, jnp.where(lane == 1, gidx * epg + i2, 0))


def _router(x_mix, g, mod3, wr, br, n_experts, n_lat_rows, seq, n_batch):
    r, d = x_mix.shape
    tm = ROW_TILE
    mrow = functools.partial(_mod_row, tile=tm, n_lat_rows=n_lat_rows, seq=seq, n_batch=n_batch)
    lanes = wr.shape[1]
    return pl.pallas_call(
        functools.partial(_router_kernel, n_experts=n_experts),
        grid=(r // tm,),
        in_specs=[
            pl.BlockSpec((tm, d), lambda i: (i, 0)),
            pl.BlockSpec((1, d), lambda i: (0, 0)),
            pl.BlockSpec((None, 1, d), lambda i: (mrow(i), 0, 3)),
            pl.BlockSpec((None, 1, d), lambda i: (mrow(i), 0, 4)),
            pl.BlockSpec(wr.shape, lambda i: (0, 0)),
            pl.BlockSpec(br.shape, lambda i: (0, 0)),
        ],
        out_specs=[
            pl.BlockSpec((tm, d), lambda i: (i, 0)),
            pl.BlockSpec((tm, lanes), lambda i: (i, 0)),
            pl.BlockSpec((tm, lanes), lambda i: (i, 0)),
        ],
        out_shape=[
            jax.ShapeDtypeStruct((r, d), F32),
            jax.ShapeDtypeStruct((r, lanes), F32),
            jax.ShapeDtypeStruct((r, lanes), jnp.int32),
        ],
        compiler_params=_params(1),
        name="moe_router",
    )(x_mix, g.reshape(1, d), mod3, mod3, wr, br)


def _dispatch_tables(e_pair, n_experts, tm):
    r = e_pair.shape[0]
    flat_e = e_pair.reshape(-1)
    n_pairs = flat_e.shape[0]
    n_rows = n_pairs + n_experts * tm
    n_tiles = n_rows // tm
    onehot = (flat_e[:, None] == jnp.arange(n_experts, dtype=jnp.int32)[None, :]).astype(jnp.int32)
    rank = jnp.take_along_axis(jnp.cumsum(onehot, axis=0) - onehot, flat_e[:, None], axis=1)[:, 0]
    counts = jnp.sum(onehot, axis=0)
    padded = ((counts + tm - 1) // tm) * tm
    ends = jnp.cumsum(padded)
    dest = (ends - padded)[flat_e] + rank
    tok_of_row = jnp.zeros((n_rows,), jnp.int32).at[dest].set(jnp.arange(n_pairs, dtype=jnp.int32) // TOP_K)
    n_used = (ends[-1] // tm).astype(jnp.int32)
    tile_start = jnp.minimum(jnp.arange(n_tiles, dtype=jnp.int32), n_used - 1) * tm
    tile_expert = jnp.sum((tile_start[:, None] >= ends[None, :]).astype(jnp.int32), axis=1)
    return tok_of_row, tile_expert.astype(jnp.int32), n_used.reshape(1), dest.astype(jnp.int32), n_rows


def _moe_up_kernel(te_ref, nu_ref, tok_ref, h_hbm, wg_ref, wu_ref, a_ref, xbuf, sem):
    i = pl.program_id(0)
    n_used = nu_ref[0]
    tm = a_ref.shape[0]

    def row_copy(tile, r, slot):
        tok = tok_ref[tile * tm + r]
        return pltpu.make_async_copy(h_hbm.at[pl.ds(tok, 1)], xbuf.at[slot, pl.ds(r, 1)], sem.at[slot])

    def gather(tile, slot):
        def body(r, carry):
            row_copy(tile, r, slot).start()
            return carry
        lax.fori_loop(0, tm, body, 0)

    @pl.when(i == 0)
    def _():
        gather(0, 0)

    @pl.when(i + 1 < n_used)
    def _():
        gather(i + 1, (i + 1) % 2)

    @pl.when(i < n_used)
    def _():
        slot = i % 2
        pltpu.make_async_copy(h_hbm.at[pl.ds(0, tm)], xbuf.at[slot], sem.at[slot]).wait()
        x = xbuf[slot].astype(BF16)
        gate = jnp.dot(x, wg_ref[...], preferred_element_type=F32)
        up = jnp.dot(x, wu_ref[...], preferred_element_type=F32)
        a_ref[...] = ((gate * jax.nn.sigmoid(gate)) * up).astype(a_ref.dtype)

    @pl.when(i >= n_used)
    def _():
        a_ref[...] = jnp.zeros_like(a_ref)


def _moe_up(h2, tok_of_row, tile_expert, n_used, wg, wu, n_rows, tm):
    d = h2.shape[1]
    f = wg.shape[2]
    return pl.pallas_call(
        _moe_up_kernel,
        grid_spec=pltpu.PrefetchScalarGridSpec(
            num_scalar_prefetch=3,
            grid=(n_rows // tm,),
            in_specs=[
                pl.BlockSpec(memory_space=pl.ANY),
                pl.BlockSpec((None, d, f), lambda i, te, nu, tok: (te[i], 0, 0)),
                pl.BlockSpec((None, d, f), lambda i, te, nu, tok: (te[i], 0, 0)),
            ],
            out_specs=pl.BlockSpec((tm, f), lambda i, te, nu, tok: (i, 0)),
            scratch_shapes=[pltpu.VMEM((2, tm, d), F32), pltpu.SemaphoreType.DMA((2,))],
        ),
        out_shape=jax.ShapeDtypeStruct((n_rows, f), BF16),
        compiler_params=_params(1),
        name="moe_up",
    )(tile_expert, n_used, tok_of_row, h2, wg, wu)


def _moe_down_kernel(te_ref, nu_ref, a_ref, wd_ref, y_ref):
    i = pl.program_id(0)

    @pl.when(i < nu_ref[0])
    def _():
        y_ref[...] = jnp.dot(a_ref[...], wd_ref[...], preferred_element_type=F32)

    @pl.when(i >= nu_ref[0])
    def _():
        y_ref[...] = jnp.zeros_like(y_ref)


def _moe_down(a, tile_expert, n_used, wd, tm):
    n_rows, f = a.shape
    d = wd.shape[2]
    return pl.pallas_call(
        _moe_down_kernel,
        grid_spec=pltpu.PrefetchScalarGridSpec(
            num_scalar_prefetch=2,
            grid=(n_rows // tm,),
            in_specs=[
                pl.BlockSpec((tm, f), lambda i, te, nu: (i, 0)),
                pl.BlockSpec((None, f, d), lambda i, te, nu: (te[i], 0, 0)),
            ],
            out_specs=pl.BlockSpec((tm, d), lambda i, te, nu: (i, 0)),
        ),
        out_shape=jax.ShapeDtypeStruct((n_rows, d), F32),
        compiler_params=_params(1),
        name="moe_down",
    )(tile_expert, n_used, a, wd)


def _combine_kernel(pos_ref, x_ref, w_ref, g_ref, fg_ref, y_hbm, o_ref, ybuf, sem, *, final_norm):
    i = pl.program_id(0)
    n = pl.num_programs(0)
    tm = x_ref.shape[0]

    def row_copy(tile, r, k, slot):
        row = pos_ref[(tile * tm + r) * TOP_K + k]
        return pltpu.make_async_copy(y_hbm.at[pl.ds(row, 1)], ybuf.at[slot, k, pl.ds(r, 1)], sem.at[slot])

    def gather(tile, slot):
        def body(r, carry):
            for k in range(TOP_K):
                row_copy(tile, r, k, slot).start()
            return carry
        lax.fori_loop(0, tm, body, 0)

    @pl.when(i == 0)
    def _():
        gather(0, 0)

    @pl.when(i + 1 < n)
    def _():
        gather(i + 1, (i + 1) % 2)

    slot = i % 2
    for k in range(TOP_K):
        pltpu.make_async_copy(y_hbm.at[pl.ds(0, tm)], ybuf.at[slot, k], sem.at[slot]).wait()
    w = w_ref[...]
    moe = w[:, 0:1] * ybuf[slot, 0] + w[:, 1:2] * ybuf[slot, 1]
    out = x_ref[...] + g_ref[...] * moe
    if final_norm:
        out = (out * lax.rsqrt(jnp.mean(out * out, axis=-1, keepdims=True) + EPS)) * fg_ref[...]
    o_ref[...] = out


def _combine(x_mix, route_w, pos, y, mod3, final_g, n_rows, n_lat_rows, seq, n_batch, final_norm):
    d = x_mix.shape[1]
    tm = ROW_TILE
    lanes = route_w.shape[1]
    mrow = functools.partial(_mod_row, tile=tm, n_lat_rows=n_lat_rows, seq=seq, n_batch=n_batch)
    return pl.pallas_call(
        functools.partial(_combine_kernel, final_norm=final_norm),
        grid_spec=pltpu.PrefetchScalarGridSpec(
            num_scalar_prefetch=1,
            grid=(n_rows // tm,),
            in_specs=[
                pl.BlockSpec((tm, d), lambda i, p: (i, 0)),
                pl.BlockSpec((tm, lanes), lambda i, p: (i, 0)),
                pl.BlockSpec((None, 1, d), lambda i, p: (mrow(i), 0, 5)),
                pl.BlockSpec((1, d), lambda i, p: (0, 0)),
                pl.BlockSpec(memory_space=pl.ANY),
            ],
            out_specs=pl.BlockSpec((tm, d), lambda i, p: (i, 0)),
            scratch_shapes=[pltpu.VMEM((2, TOP_K, tm, d), F32), pltpu.SemaphoreType.DMA((2,))],
        ),
        out_shape=jax.ShapeDtypeStruct((n_rows, d), F32),
        compiler_params=_params(1),
        name="moe_combine",
    )(pos, x_mix, route_w, mod3, final_g.reshape(1, d), y)


def kernel(x, c, ctx, c_ctx, w_mod, b_mod, norm1_g, norm2_g, w_in, b_gates, mlstm_norm_g, gmlp_ws, gmlp_bs,
           gmlp_norm_g, conv_w, w_out, w_router, b_router, w_gate_e, w_up_e, w_down_e, final_g):
    n_batch, seq, d = x.shape
    ctx_len = ctx.shape[1]
    depth = w_mod.shape[0]
    wm = mlstm_norm_g.shape[1]
    n_experts = w_router.shape[1]
    n_lat_rows = n_batch * seq
    n_all_rows = n_lat_rows + n_batch * ctx_len
    qk_w = wm // 2
    n_gate_cols = N_DIRS * 2 * MLSTM_HEADS
    gate_col0 = 2 * qk_w + 2 * wm
    assert seq % ROW_TILE == 0 and seq % MM_ROW_TILE == 0 and (n_batch * ctx_len) % ROW_TILE == 0
    assert n_batch + 1 <= V7X_SUBLANES and n_experts <= V7X_LANES and n_gate_cols <= V7X_LANES

    rows = jnp.concatenate([x.reshape(n_lat_rows, d), ctx.reshape(n_batch * ctx_len, d)], axis=0)

    cond = jnp.zeros((V7X_SUBLANES, d), F32).at[:n_batch].set(c).at[n_batch].set(c_ctx)
    mods = _mod_table(cond, w_mod, b_mod)

    wr = jnp.zeros((d, V7X_LANES), F32).at[:, :n_experts].set(w_router)
    br = jnp.zeros((1, V7X_LANES), F32).at[0, :n_experts].set(b_router)

    for layer in range(depth):
        last = layer + 1 == depth
        mod3 = mods[layer].reshape(V7X_SUBLANES, 1, N_MOD * d)
        w_in_l = w_in[layer]
        w_main = jnp.concatenate([w_in_l[:, :gate_col0], w_in_l[:, gate_col0 + n_gate_cols:]], axis=1).astype(BF16)
        w_gate = jnp.zeros((d, V7X_LANES), BF16).at[:, :n_gate_cols].set(
            w_in_l[:, gate_col0:gate_col0 + n_gate_cols].astype(BF16))
        bias = jnp.zeros((1, V7X_LANES), F32).at[0, :n_gate_cols].set(b_gates[layer].reshape(-1))
        bias_t = jnp.broadcast_to(b_gates[layer].reshape(-1, 1), (n_gate_cols, MLSTM_CHUNK))

        h1 = _normmod(rows, norm1_g[layer], mod3, 0, n_lat_rows, seq, n_batch)
        z = _matmul(h1, w_main, BF16)
        gates = _matmul(h1, w_gate, F32)
        gates_t = gates[:, :n_gate_cols].T
        hdirs = _mlstm(z, gates, gates_t, bias, bias_t, n_batch, seq, ctx_len, wm)

        n_rows = n_lat_rows if last else n_all_rows
        y = _mix(hdirs, z, mlstm_norm_g[layer].reshape(1, wm), gmlp_ws[layer].astype(BF16), gmlp_bs[layer].T,
                 gmlp_norm_g[layer].reshape(1, -1), conv_w[layer], n_rows, n_lat_rows, ctx_len)
        x_mix = _wout(y, w_out[layer].astype(BF16), rows, mod3, n_lat_rows, seq, n_batch)

        h2, route_w, route_e = _router(x_mix, norm2_g[layer], mod3, wr, br, n_experts, n_lat_rows, seq, n_batch)
        tok_of_row, tile_expert, n_used, pos, n_sorted = _dispatch_tables(
            route_e[:, :TOP_K], n_experts, MOE_ROW_TILE)
        act = _moe_up(h2, tok_of_row, tile_expert, n_used, w_gate_e[layer].astype(BF16),
                      w_up_e[layer].astype(BF16), n_sorted, MOE_ROW_TILE)
        y_moe = _moe_down(act, tile_expert, n_used, w_down_e[layer].astype(BF16), MOE_ROW_TILE)
        rows = _combine(x_mix, route_w, pos, y_moe, mod3, final_g, n_rows, n_lat_rows, seq, n_batch, last)

    return rows.reshape(n_batch, seq, d)
```

```python
import functools

import jax
import jax.numpy as jnp
from jax import lax
from jax.experimental import pallas as pl
from jax.experimental.pallas import tpu as pltpu

F32 = jnp.float32
BF16 = jnp.bfloat16

N_MOD = 6
EPS = 1e-6
N_DIRS = 2
MLSTM_HEADS = 4
MLSTM_CHUNK = 128
GMLP_HEADS = 4
GMLP_CHUNK = 128
GRID_W = 64
N_EXPERT_GROUPS = 4
TOP_K = 2

V7X_LANES = 128
V7X_SUBLANES = 8
V7X_VMEM_BYTES = 64 * 1024 * 1024
VMEM_LIMIT = V7X_VMEM_BYTES - 8 * 1024 * 1024

ROW_TILE = 256
MM_ROW_TILE = 512
MOE_ROW_TILE = 256
MOE_UP_SPLITS = 2


def _params(n_axes):
    return pltpu.CompilerParams(
        dimension_semantics=("arbitrary",) * n_axes, vmem_limit_bytes=VMEM_LIMIT)


def _largest_divisor(n, candidates):
    for c in candidates:
        if n % c == 0:
            return c
    raise ValueError(f"no tile in {candidates} divides {n}")


def _mod_row(i, tile, n_lat_rows, seq, n_batch):
    return jnp.where(i * tile < n_lat_rows, (i * tile) // seq, n_batch)


def _mod_kernel(a_ref, w_ref, b_ref, o_ref):
    a = a_ref[...]
    a = a * jax.nn.sigmoid(a)
    acc = jnp.dot(a.astype(BF16), w_ref[...].astype(BF16), preferred_element_type=F32)
    o_ref[...] = acc + b_ref[...]


def _mod_table(a, w_mod, b_mod):
    depth, d, n = w_mod.shape
    tn = _largest_divisor(n, (1024, 512, 256, 128))
    rows = a.shape[0]
    return pl.pallas_call(
        _mod_kernel,
        grid=(depth, n // tn),
        in_specs=[
            pl.BlockSpec((rows, d), lambda l, j: (0, 0)),
            pl.BlockSpec((None, d, tn), lambda l, j: (l, 0, j)),
            pl.BlockSpec((None, 1, tn), lambda l, j: (l, 0, j)),
        ],
        out_specs=pl.BlockSpec((None, rows, tn), lambda l, j: (l, 0, j)),
        out_shape=jax.ShapeDtypeStruct((depth, rows, n), F32),
        compiler_params=_params(2),
        name="mod_table",
    )(a, w_mod, b_mod.reshape(depth, 1, n))


def _norm_modulate(x, g, shift, scale):
    y = x * lax.rsqrt(jnp.mean(x * x, axis=-1, keepdims=True) + EPS)
    return (y * g) * (1.0 + scale) + shift


def _normmod_kernel(x_ref, g_ref, sh_ref, sc_ref, o_ref):
    o_ref[...] = _norm_modulate(x_ref[...], g_ref[...], sh_ref[...], sc_ref[...]).astype(o_ref.dtype)


def _normmod(rows, g, mod3, shift_col, n_lat_rows, seq, n_batch):
    r, d = rows.shape
    mrow = functools.partial(_mod_row, tile=ROW_TILE, n_lat_rows=n_lat_rows, seq=seq, n_batch=n_batch)
    return pl.pallas_call(
        _normmod_kernel,
        grid=(r // ROW_TILE,),
        in_specs=[
            pl.BlockSpec((ROW_TILE, d), lambda i: (i, 0)),
            pl.BlockSpec((1, d), lambda i: (0, 0)),
            pl.BlockSpec((None, 1, d), lambda i: (mrow(i), 0, shift_col)),
            pl.BlockSpec((None, 1, d), lambda i: (mrow(i), 0, shift_col + 1)),
        ],
        out_specs=pl.BlockSpec((ROW_TILE, d), lambda i: (i, 0)),
        out_shape=jax.ShapeDtypeStruct((r, d), BF16),
        compiler_params=_params(1),
        name="norm_modulate",
    )(rows, g.reshape(1, d), mod3, mod3)


def _mm_kernel(a_ref, b_ref, o_ref):
    o_ref[...] = jnp.dot(a_ref[...], b_ref[...], preferred_element_type=F32).astype(o_ref.dtype)


def _matmul(a, b, out_dtype):
    m, k = a.shape
    n = b.shape[1]
    tm = _largest_divisor(m, (MM_ROW_TILE, 256, 128))
    tn = _largest_divisor(n, (1024, 512, 256, 128))
    return pl.pallas_call(
        _mm_kernel,
        grid=(n // tn, m // tm),
        in_specs=[
            pl.BlockSpec((tm, k), lambda j, i: (i, 0)),
            pl.BlockSpec((k, tn), lambda j, i: (0, j)),
        ],
        out_specs=pl.BlockSpec((tm, tn), lambda j, i: (i, j)),
        out_shape=jax.ShapeDtypeStruct((m, n), out_dtype),
        compiler_params=_params(2),
        name="projection",
    )(a, b)


def _mm_f32w_kernel(a_ref, w_ref, o_ref, wb_scr):
    @pl.when(pl.program_id(1) == 0)
    def _():
        wb_scr[...] = w_ref[...].astype(BF16)

    o_ref[...] = jnp.dot(a_ref[...], wb_scr[...], preferred_element_type=F32).astype(o_ref.dtype)


def _matmul_f32w(a, w3, layer, n_cols, out_dtype):
    m, k = a.shape
    tm = _largest_divisor(m, (MM_ROW_TILE, 256, 128))
    tn = _largest_divisor(n_cols, (1024, 512, 256, 128))
    return pl.pallas_call(
        _mm_f32w_kernel,
        grid=(n_cols // tn, m // tm),
        in_specs=[
            pl.BlockSpec((tm, k), lambda j, i: (i, 0)),
            pl.BlockSpec((None, k, tn), lambda j, i: (layer, 0, j)),
        ],
        out_specs=pl.BlockSpec((tm, tn), lambda j, i: (i, j)),
        out_shape=jax.ShapeDtypeStruct((m, n_cols), out_dtype),
        scratch_shapes=[pltpu.VMEM((k, tn), BF16)],
        compiler_params=_params(2),
        name="projection_f32w",
    )(a, w3)


def _gates_kernel(a_ref, w_ref, wt_ref, g_ref, gt_ref):
    a = a_ref[...]
    g_ref[...] = jnp.dot(a, w_ref[...], preferred_element_type=F32)
    gt_ref[...] = lax.dot_general(wt_ref[...], a, (((1,), (1,)), ((), ())), preferred_element_type=F32)


def _gates(a, w_gate, w_gate_t):
    m, k = a.shape
    lanes = w_gate.shape[1]
    tm = _largest_divisor(m, (MM_ROW_TILE, 256, 128))
    return pl.pallas_call(
        _gates_kernel,
        grid=(m // tm,),
        in_specs=[
            pl.BlockSpec((tm, k), lambda i: (i, 0)),
            pl.BlockSpec((k, lanes), lambda i: (0, 0)),
            pl.BlockSpec((lanes, k), lambda i: (0, 0)),
        ],
        out_specs=[
            pl.BlockSpec((tm, lanes), lambda i: (i, 0)),
            pl.BlockSpec((lanes, tm), lambda i: (0, i)),
        ],
        out_shape=[jax.ShapeDtypeStruct((m, lanes), F32), jax.ShapeDtypeStruct((lanes, m), F32)],
        compiler_params=_params(1),
        name="gate_projection",
    )(a, w_gate, w_gate_t)


def _log_sigmoid(x):
    return jnp.minimum(x, 0.0) - jnp.log1p(jnp.exp(-jnp.abs(x)))


def _mlstm_chain(q, k, v, i_col, f_col, i_row, f_row, before, after, c_ref, m_ref, o_ref, cols):
    chunk, dk = q.shape
    dv = v.shape[1]
    lf_col = _log_sigmoid(f_col)
    lf_row = _log_sigmoid(f_row)
    b_col = jnp.sum(jnp.where(before, lf_row, 0.0), axis=1, keepdims=True)
    b_row = jnp.sum(jnp.where(after, lf_col, 0.0), axis=0, keepdims=True)
    b_last = jnp.sum(lf_row, axis=1, keepdims=True)

    m_prev = m_ref[0:1, 0:1]
    dmat = jnp.where(before, b_col - b_row + i_row, -jnp.inf)
    m_inter = b_col + m_prev
    m_t = jnp.maximum(jnp.max(dmat, axis=1, keepdims=True), m_inter)
    q = (q.astype(F32) * (dk ** -0.5)).astype(BF16)
    qk = lax.dot_general(q, k, (((1,), (1,)), ((), ())), preferred_element_type=F32)
    s = (qk * jnp.exp(dmat - m_t)).astype(BF16)
    v_aug = jnp.concatenate([v, jnp.ones((chunk, V7X_LANES), BF16)], axis=1)
    c_prev = c_ref[...]
    inter = jnp.exp(m_inter - m_t)
    num = (jnp.dot(s, v_aug, preferred_element_type=F32)
           + inter * jnp.dot(q, c_prev.astype(BF16), preferred_element_type=F32))
    den = num[:, dv:dv + 1]
    o_ref[:, cols] = num[:, :dv] / jnp.maximum(jnp.abs(den), jnp.exp(-m_t))

    g_end = b_last - b_col + i_col
    m_new = jnp.maximum(b_last + m_prev, jnp.max(g_end, axis=0, keepdims=True))
    decay = jnp.exp(b_last + m_prev - m_new)
    wv = (jnp.exp(g_end - m_new) * v_aug.astype(F32)).astype(BF16)
    c_ref[...] = decay * c_prev + lax.dot_general(k, wv, (((0,), (0,)), ((), ())), preferred_element_type=F32)
    m_ref[...] = jnp.broadcast_to(m_new, m_ref.shape)


def _mlstm_kernel(qf_ref, kf_ref, vf_ref, gf_ref, gtf_ref, qb_ref, kb_ref, vb_ref, gb_ref, gtb_ref,
                  b_ref, bt_ref, of_ref, ob_ref, c_scr, m_scr):
    nh = MLSTM_HEADS
    chunk = qf_ref.shape[0]
    dk = qf_ref.shape[1] // nh
    dv = vf_ref.shape[1] // nh

    @pl.when(pl.program_id(1) == 0)
    def _():
        c_scr[...] = jnp.zeros_like(c_scr)
        m_scr[...] = jnp.zeros_like(m_scr)

    t_idx = lax.broadcasted_iota(jnp.int32, (chunk, chunk), 0)
    s_idx = lax.broadcasted_iota(jnp.int32, (chunk, chunk), 1)
    dirs = ((qf_ref, kf_ref, vf_ref, gf_ref, gtf_ref, of_ref), (qb_ref, kb_ref, vb_ref, gb_ref, gtb_ref, ob_ref))
    for d, (q_ref, k_ref, v_ref, g_ref, gt_ref, o_ref) in enumerate(dirs):
        before = s_idx <= t_idx if d == 0 else s_idx >= t_idx
        after = s_idx >= t_idx if d == 0 else s_idx <= t_idx
        g = g_ref[...] + b_ref[...]
        gt = gt_ref[...] + bt_ref[...]
        for h in range(nh):
            ci = d * 2 * nh + h
            cf = ci + nh
            _mlstm_chain(
                q_ref[:, h * dk:(h + 1) * dk], k_ref[:, h * dk:(h + 1) * dk], v_ref[:, h * dv:(h + 1) * dv],
                g[:, ci:ci + 1], g[:, cf:cf + 1], gt[ci:ci + 1, :], gt[cf:cf + 1, :], before, after,
                c_scr.at[d * nh + h], m_scr.at[d * nh + h], o_ref, slice(h * dv, (h + 1) * dv))


def _mlstm(z, gates, gates_t, bias, bias_t, n_batch, seq, ctx_len, width):
    r = z.shape[0]
    nh = MLSTM_HEADS
    dv = width // nh
    dk = dv // 2
    qk_w = nh * dk
    chunk = MLSTM_CHUNK
    n_ctx = ctx_len // chunk
    n_lat = seq // chunk
    n_gate_rows = bias_t.shape[0]

    def tile(d, b, step):
        lat = step - n_ctx
        ctx_tile = n_batch * n_lat + b * n_ctx + (step if d == 0 else n_ctx - 1 - step)
        lat_tile = b * n_lat + (lat if d == 0 else n_lat - 1 - lat)
        return jnp.where(step < n_ctx, ctx_tile, lat_tile)

    def dir_specs(d):
        return [
            pl.BlockSpec((chunk, qk_w), lambda b, s: (tile(d, b, s), 0)),
            pl.BlockSpec((chunk, qk_w), lambda b, s: (tile(d, b, s), 1)),
            pl.BlockSpec((chunk, width), lambda b, s: (tile(d, b, s), 1)),
            pl.BlockSpec((chunk, gates.shape[1]), lambda b, s: (tile(d, b, s), 0)),
            pl.BlockSpec((n_gate_rows, chunk), lambda b, s: (0, tile(d, b, s))),
        ]

    return pl.pallas_call(
        _mlstm_kernel,
        grid=(n_batch, n_ctx + n_lat),
        in_specs=dir_specs(0) + dir_specs(1) + [
            pl.BlockSpec(bias.shape, lambda b, s: (0, 0)),
            pl.BlockSpec(bias_t.shape, lambda b, s: (0, 0)),
        ],
        out_specs=[
            pl.BlockSpec((chunk, width), lambda b, s: (tile(0, b, s), 0)),
            pl.BlockSpec((chunk, width), lambda b, s: (tile(1, b, s), 0)),
        ],
        out_shape=[jax.ShapeDtypeStruct((r, width), F32), jax.ShapeDtypeStruct((r, width), F32)],
        scratch_shapes=[
            pltpu.VMEM((N_DIRS * nh, dk, dv + V7X_LANES), F32),
            pltpu.VMEM((N_DIRS * nh, V7X_SUBLANES, V7X_LANES), F32),
        ],
        compiler_params=_params(2),
        name="mlstm_scan",
    )(z, z, z, gates, gates_t, z, z, z, gates, gates_t, bias, bias_t)


def _mix_kernel(hf_ref, hb_ref, o_ref, u_ref, v_ref, bg_ref, cg_ref, xin_ref, mg_ref, ws_ref, bst_ref, gg_ref,
                cw_ref, y_ref, *, n_lat_rows, ctx_len):
    i = pl.program_id(0)
    tm = y_ref.shape[0]
    wm = o_ref.shape[1]
    gw = u_ref.shape[1]
    dv = wm // MLSTM_HEADS
    hdim = gw // GMLP_HEADS

    for hd in range(MLSTM_HEADS):
        cols = slice(hd * dv, (hd + 1) * dv)
        hh = hf_ref[:, cols] + hb_ref[:, cols]
        hh = hh * lax.rsqrt(jnp.mean(hh * hh, axis=-1, keepdims=True) + EPS)
        out = (hh * mg_ref[:, cols]) * jax.nn.sigmoid(o_ref[:, cols].astype(F32))
        y_ref[:, cols] = out.astype(y_ref.dtype)

    u = jax.nn.gelu(u_ref[...].astype(F32))
    v = jax.nn.gelu(v_ref[...].astype(F32))
    v = (v * lax.rsqrt(jnp.mean(v * v, axis=-1, keepdims=True) + EPS)) * gg_ref[...]
    vb = v.astype(BF16)
    for c in range(tm // GMLP_CHUNK):
        rows = slice(c * GMLP_CHUNK, (c + 1) * GMLP_CHUNK)
        for hd in range(GMLP_HEADS):
            cols = slice(hd * hdim, (hd + 1) * hdim)
            sp = jnp.dot(ws_ref[hd], vb[rows, cols], preferred_element_type=F32) + bst_ref[:, hd:hd + 1]
            y_ref[rows, wm + hd * hdim:wm + (hd + 1) * hdim] = (u[rows, cols] * sp).astype(y_ref.dtype)

    yv = cg_ref[...].astype(F32) * xin_ref[...].astype(F32)
    t = lax.broadcasted_iota(jnp.int32, (tm, 1), 0)
    is_lat = i * tm < n_lat_rows
    pos = jnp.where(is_lat, t % GRID_W, t % ctx_len)
    last = jnp.where(is_lat, GRID_W - 1, ctx_len - 1)
    prev = jnp.where(pos != 0, pltpu.roll(yv, 1, axis=0), 0.0)
    nxt = jnp.where(pos != last, pltpu.roll(yv, tm - 1, axis=0), 0.0)
    conv = cw_ref[0:1, :] * prev + cw_ref[1:2, :] * yv + cw_ref[2:3, :] * nxt
    y_ref[:, wm + gw:] = (bg_ref[...].astype(F32) * conv).astype(y_ref.dtype)


def _mix(hf, hb, z_a, z_b, mg, ws, bst, gg, cw, n_rows, n_lat_rows, ctx_len):
    wm = mg.shape[1]
    gw = gg.shape[1]
    cwid = cw.shape[1]
    tm = ROW_TILE
    assert tm % GRID_W == 0 and tm % ctx_len == 0 and tm % GMLP_CHUNK == 0
    assert gw == cwid
    full = lambda i: (0, 0)
    return pl.pallas_call(
        functools.partial(_mix_kernel, n_lat_rows=n_lat_rows, ctx_len=ctx_len),
        grid=(n_rows // tm,),
        in_specs=[
            pl.BlockSpec((tm, wm), lambda i: (i, 0)),
            pl.BlockSpec((tm, wm), lambda i: (i, 0)),
            pl.BlockSpec((tm, wm), lambda i: (i, 2)),
            pl.BlockSpec((tm, gw), lambda i: (i, 0)),
            pl.BlockSpec((tm, gw), lambda i: (i, 1)),
            pl.BlockSpec((tm, gw), lambda i: (i, 2)),
            pl.BlockSpec((tm, gw), lambda i: (i, 3)),
            pl.BlockSpec((tm, gw), lambda i: (i, 4)),
            pl.BlockSpec(mg.shape, full),
            pl.BlockSpec(ws.shape, lambda i: (0, 0, 0)),
            pl.BlockSpec(bst.shape, full),
            pl.BlockSpec(gg.shape, full),
            pl.BlockSpec(cw.shape, full),
        ],
        out_specs=pl.BlockSpec((tm, wm + gw + cwid), lambda i: (i, 0)),
        out_shape=jax.ShapeDtypeStruct((n_rows, wm + gw + cwid), BF16),
        compiler_params=_params(1),
        name="mixer_epilogue",
    )(hf, hb, z_a, z_b, z_b, z_b, z_b, z_b, mg, ws, bst, gg, cw)


def _wout_kernel(y_ref, w_ref, x_ref, g_ref, o_ref, wb_scr):
    @pl.when(pl.program_id(1) == 0)
    def _():
        wb_scr[...] = w_ref[...].astype(BF16)

    acc = jnp.dot(y_ref[...], wb_scr[...], preferred_element_type=F32)
    o_ref[...] = x_ref[...] + g_ref[...] * acc


def _wout(y, w3, layer, rows, mod3, n_lat_rows, seq, n_batch):
    m, k = y.shape
    d = w3.shape[2]
    tm = _largest_divisor(m, (MM_ROW_TILE, 256))
    tn = _largest_divisor(d, (512, 256, 128))
    mrow = functools.partial(_mod_row, tile=tm, n_lat_rows=n_lat_rows, seq=seq, n_batch=n_batch)
    gate_blk0 = (2 * d) // tn
    return pl.pallas_call(
        _wout_kernel,
        grid=(d // tn, m // tm),
        in_specs=[
            pl.BlockSpec((tm, k), lambda j, i: (i, 0)),
            pl.BlockSpec((None, k, tn), lambda j, i: (layer, 0, j)),
            pl.BlockSpec((tm, tn), lambda j, i: (i, j)),
            pl.BlockSpec((None, 1, tn), lambda j, i: (mrow(i), 0, gate_blk0 + j)),
        ],
        out_specs=pl.BlockSpec((tm, tn), lambda j, i: (i, j)),
        out_shape=jax.ShapeDtypeStruct((m, d), F32),
        scratch_shapes=[pltpu.VMEM((k, tn), BF16)],
        compiler_params=_params(2),
        name="out_projection",
    )(y, w3, rows, mod3)


def _router_kernel(x_ref, g_ref, sh_ref, sc_ref, wr_ref, br_ref, h_ref, w_ref, e_ref, *, n_experts):
    h = _norm_modulate(x_ref[...], g_ref[...], sh_ref[...], sc_ref[...])
    h_ref[...] = h
    tm = h.shape[0]
    lanes = w_ref.shape[1]
    h_hi = h.astype(BF16)
    h_lo = (h - h_hi.astype(F32)).astype(BF16)
    prod = jnp.dot(jnp.concatenate([h_hi, h_lo], axis=0), wr_ref[...], preferred_element_type=F32)
    logits = (prod[:tm, :lanes] + (prod[:tm, lanes:] + prod[tm:, :lanes])) + prod[tm:, lanes:]
    scores = jax.nn.sigmoid(logits)
    sel = scores + br_ref[...]
    epg = n_experts // N_EXPERT_GROUPS
    sel_c = [sel[:, e:e + 1] for e in range(n_experts)]
    sc_c = [scores[:, e:e + 1] for e in range(n_experts)]

    best_g = None
    for gi in range(N_EXPERT_GROUPS):
        mem = sel_c[gi * epg:(gi + 1) * epg]
        gs = None
        for a in range(epg):
            for b in range(a + 1, epg):
                pair = mem[a] + mem[b]
                gs = pair if gs is None else jnp.maximum(gs, pair)
        if best_g is None:
            best_g, gidx = gs, jnp.zeros_like(gs, dtype=jnp.int32)
        else:
            upd = gs > best_g
            gidx = jnp.where(upd, gi, gidx)
            best_g = jnp.where(upd, gs, best_g)

    in_sel, in_sc = [], []
    for a in range(epg):
        vs, vr = sel_c[a], sc_c[a]
        for gi in range(1, N_EXPERT_GROUPS):
            vs = jnp.where(gidx == gi, sel_c[gi * epg + a], vs)
            vr = jnp.where(gidx == gi, sc_c[gi * epg + a], vr)
        in_sel.append(vs)
        in_sc.append(vr)

    v1, i1, s1 = in_sel[0], jnp.zeros_like(gidx), in_sc[0]
    for a in range(1, epg):
        upd = in_sel[a] > v1
        i1 = jnp.where(upd, a, i1)
        s1 = jnp.where(upd, in_sc[a], s1)
        v1 = jnp.where(upd, in_sel[a], v1)
    v2 = jnp.full_like(v1, -jnp.inf)
    i2 = jnp.zeros_like(gidx)
    s2 = jnp.zeros_like(s1)
    for a in range(epg):
        upd = jnp.logical_and(i1 != a, in_sel[a] > v2)
        i2 = jnp.where(upd, a, i2)
        s2 = jnp.where(upd, in_sc[a], s2)
        v2 = jnp.where(upd, in_sel[a], v2)
    denom = s1 + s2
    lane = lax.broadcasted_iota(jnp.int32, w_ref.shape, 1)
    w_ref[...] = jnp.where(lane == 0, s1 / denom, jnp.where(lane == 1, s2 / denom, 0.0))
    e_ref[...] = jnp.where(lane == 0, gidx * epg + i1, jnp.where(lane == 1, gidx * epg + i2, 0))


def _router(x_mix, g, mod3, wr2, br, n_experts, n_lat_rows, seq, n_batch):
    r, d = x_mix.shape
    tm = ROW_TILE
    mrow = functools.partial(_mod_row, tile=tm, n_lat_rows=n_lat_rows, seq=seq, n_batch=n_batch)
    lanes = br.shape[1]
    return pl.pallas_call(
        functools.partial(_router_kernel, n_experts=n_experts),
        grid=(r // tm,),
        in_specs=[
            pl.BlockSpec((tm, d), lambda i: (i, 0)),
            pl.BlockSpec((1, d), lambda i: (0, 0)),
            pl.BlockSpec((None, 1, d), lambda i: (mrow(i), 0, 3)),
            pl.BlockSpec((None, 1, d), lambda i: (mrow(i), 0, 4)),
            pl.BlockSpec(wr2.shape, lambda i: (0, 0)),
            pl.BlockSpec(br.shape, lambda i: (0, 0)),
        ],
        out_specs=[
            pl.BlockSpec((tm, d), lambda i: (i, 0)),
            pl.BlockSpec((tm, lanes), lambda i: (i, 0)),
            pl.BlockSpec((tm, lanes), lambda i: (i, 0)),
        ],
        out_shape=[
            jax.ShapeDtypeStruct((r, d), F32),
            jax.ShapeDtypeStruct((r, lanes), F32),
            jax.ShapeDtypeStruct((r, lanes), jnp.int32),
        ],
        compiler_params=_params(1),
        name="moe_router",
    )(x_mix, g.reshape(1, d), mod3, mod3, wr2, br)


def _dispatch_tables(e_pair, n_experts, tm):
    flat_e = e_pair.reshape(-1)
    n_pairs = flat_e.shape[0]
    n_rows = n_pairs + n_experts * tm
    n_tiles = n_rows // tm
    onehot = (flat_e[:, None] == jnp.arange(n_experts, dtype=jnp.int32)[None, :]).astype(jnp.int32)
    rank = jnp.sum((jnp.cumsum(onehot, axis=0) - onehot) * onehot, axis=1)
    counts = jnp.sum(onehot, axis=0)
    padded = ((counts + tm - 1) // tm) * tm
    ends = jnp.cumsum(padded)
    starts = ends - padded
    dest = jnp.sum(onehot * starts[None, :], axis=1) + rank
    n_used = (ends[-1] // tm).astype(jnp.int32)
    tile_start = jnp.minimum(jnp.arange(n_tiles, dtype=jnp.int32), n_used - 1) * tm
    tile_expert = jnp.sum((tile_start[:, None] >= ends[None, :]).astype(jnp.int32), axis=1).astype(jnp.int32)
    tile_first = jnp.concatenate(
        [jnp.ones((1,), jnp.int32), (tile_expert[1:] != tile_expert[:-1]).astype(jnp.int32)])
    return dict(dest=dest.astype(jnp.int32), pad_lo=(starts + counts).astype(jnp.int32),
                pad_hi=ends.astype(jnp.int32), tile_expert=tile_expert, tile_first=tile_first,
                n_used=n_used.reshape(1), n_rows=n_rows)


def _dispatch_kernel(dest_ref, plo_ref, phi_ref, h_ref, xs_hbm, zrow, sem, zsem, *, n_experts):
    i = pl.program_id(0)
    tm = h_ref.shape[0]

    @pl.when(i == 0)
    def _():
        zrow[...] = jnp.zeros_like(zrow)

        def zero_copy(r):
            return pltpu.make_async_copy(zrow.at[pl.ds(0, 1)], xs_hbm.at[pl.ds(r, 1)], zsem.at[0])

        def start(r, carry):
            zero_copy(r).start()
            return carry

        def wait(r, carry):
            zero_copy(r).wait()
            return carry

        def tile_copy(t):
            return pltpu.make_async_copy(zrow, xs_hbm.at[pl.ds(t * tm, tm)], zsem.at[0])

        def start_tile(t, carry):
            tile_copy(t).start()
            return carry

        def wait_tile(t, carry):
            tile_copy(t).wait()
            return carry

        n_tiles = xs_hbm.shape[0] // tm
        first_unused = phi_ref[n_experts - 1] // tm
        for e in range(n_experts):
            lax.fori_loop(plo_ref[e], phi_ref[e], start, 0)
        lax.fori_loop(first_unused, n_tiles, start_tile, 0)
        for e in range(n_experts):
            lax.fori_loop(plo_ref[e], phi_ref[e], wait, 0)
        lax.fori_loop(first_unused, n_tiles, wait_tile, 0)

    def scatter(r, carry):
        for k in range(TOP_K):
            row = dest_ref[(i * tm + r) * TOP_K + k]
            pltpu.make_async_copy(h_ref.at[pl.ds(r, 1)], xs_hbm.at[pl.ds(row, 1)], sem.at[0]).start()
        return carry

    lax.fori_loop(0, tm, scatter, 0)
    for k in range(TOP_K):
        pltpu.make_async_copy(h_ref, xs_hbm.at[pl.ds(0, tm)], sem.at[0]).wait()


def _dispatch(h2, tables, n_experts):
    r, d = h2.shape
    tm = ROW_TILE
    assert MOE_ROW_TILE % tm == 0
    return pl.pallas_call(
        functools.partial(_dispatch_kernel, n_experts=n_experts),
        grid_spec=pltpu.PrefetchScalarGridSpec(
            num_scalar_prefetch=3,
            grid=(r // tm,),
            in_specs=[pl.BlockSpec((tm, d), lambda i, de, lo, hi: (i, 0))],
            out_specs=pl.BlockSpec(memory_space=pl.ANY),
            scratch_shapes=[pltpu.VMEM((tm, d), F32), pltpu.SemaphoreType.DMA((1,)),
                            pltpu.SemaphoreType.DMA((1,))],
        ),
        out_shape=jax.ShapeDtypeStruct((tables["n_rows"], d), F32),
        compiler_params=_params(1),
        name="moe_dispatch",
    )(tables["dest"], tables["pad_lo"], tables["pad_hi"], h2)


def _moe_up_kernel(te_ref, tf_ref, nu_ref, x_ref, wg_ref, wu_ref, a_ref, wgb, wub):
    i = pl.program_id(1)

    @pl.when(i < nu_ref[0])
    def _():
        @pl.when(tf_ref[i] == 1)
        def _():
            wgb[...] = wg_ref[...].astype(BF16)
            wub[...] = wu_ref[...].astype(BF16)

        x = x_ref[...].astype(BF16)
        gate = jnp.dot(x, wgb[...], preferred_element_type=F32)
        up = jnp.dot(x, wub[...], preferred_element_type=F32)
        a_ref[...] = ((gate * jax.nn.sigmoid(gate)) * up).astype(a_ref.dtype)

    @pl.when(i >= nu_ref[0])
    def _():
        a_ref[...] = jnp.zeros_like(a_ref)


def _moe_up(xs, tables, wg4, wu4, layer, tm):
    n_rows, d = xs.shape
    f = wg4.shape[3]
    fs = f // MOE_UP_SPLITS
    used = lambda i, nu: jnp.minimum(i, nu[0] - 1)
    return pl.pallas_call(
        _moe_up_kernel,
        grid_spec=pltpu.PrefetchScalarGridSpec(
            num_scalar_prefetch=3,
            grid=(MOE_UP_SPLITS, n_rows // tm),
            in_specs=[
                pl.BlockSpec((tm, d), lambda j, i, te, tf, nu: (used(i, nu), 0)),
                pl.BlockSpec((None, None, d, fs), lambda j, i, te, tf, nu: (layer, te[i], 0, j)),
                pl.BlockSpec((None, None, d, fs), lambda j, i, te, tf, nu: (layer, te[i], 0, j)),
            ],
            out_specs=pl.BlockSpec((tm, fs), lambda j, i, te, tf, nu: (i, j)),
            scratch_shapes=[pltpu.VMEM((d, fs), BF16), pltpu.VMEM((d, fs), BF16)],
        ),
        out_shape=jax.ShapeDtypeStruct((n_rows, f), BF16),
        compiler_params=_params(2),
        name="moe_up",
    )(tables["tile_expert"], tables["tile_first"], tables["n_used"], xs, wg4, wu4)


def _moe_down_kernel(te_ref, tf_ref, nu_ref, a_ref, wd_ref, y_ref, wdb):
    i = pl.program_id(0)

    @pl.when(i < nu_ref[0])
    def _():
        @pl.when(tf_ref[i] == 1)
        def _():
            wdb[...] = wd_ref[...].astype(BF16)

        y_ref[...] = jnp.dot(a_ref[...], wdb[...], preferred_element_type=F32)

    @pl.when(i >= nu_ref[0])
    def _():
        y_ref[...] = jnp.zeros_like(y_ref)


def _moe_down(a, tables, wd4, layer, tm):
    n_rows, f = a.shape
    d = wd4.shape[3]
    return pl.pallas_call(
        _moe_down_kernel,
        grid_spec=pltpu.PrefetchScalarGridSpec(
            num_scalar_prefetch=3,
            grid=(n_rows // tm,),
            in_specs=[
                pl.BlockSpec((tm, f), lambda i, te, tf, nu: (i, 0)),
                pl.BlockSpec((None, None, f, d), lambda i, te, tf, nu: (layer, te[i], 0, 0)),
            ],
            out_specs=pl.BlockSpec((tm, d), lambda i, te, tf, nu: (i, 0)),
            scratch_shapes=[pltpu.VMEM((f, d), BF16)],
        ),
        out_shape=jax.ShapeDtypeStruct((n_rows, d), F32),
        compiler_params=_params(1),
        name="moe_down",
    )(tables["tile_expert"], tables["tile_first"], tables["n_used"], a, wd4)


def _combine_kernel(pos_ref, x_ref, w_ref, g_ref, fg_ref, y_hbm, o_ref, ybuf, sem, *, final_norm):
    i = pl.program_id(0)
    n = pl.num_programs(0)
    tm = x_ref.shape[0]

    def gather(tile, slot):
        def body(r, carry):
            for k in range(TOP_K):
                row = pos_ref[(tile * tm + r) * TOP_K + k]
                pltpu.make_async_copy(y_hbm.at[pl.ds(row, 1)], ybuf.at[slot, k, pl.ds(r, 1)], sem.at[slot]).start()
            return carry
        lax.fori_loop(0, tm, body, 0)

    @pl.when(i == 0)
    def _():
        gather(0, 0)

    @pl.when(i + 1 < n)
    def _():
        gather(i + 1, (i + 1) % 2)

    slot = i % 2
    for k in range(TOP_K):
        pltpu.make_async_copy(y_hbm.at[pl.ds(0, tm)], ybuf.at[slot, k], sem.at[slot]).wait()
    w = w_ref[...]
    moe = w[:, 0:1] * ybuf[slot, 0] + w[:, 1:2] * ybuf[slot, 1]
    out = x_ref[...] + g_ref[...] * moe
    if final_norm:
        out = (out * lax.rsqrt(jnp.mean(out * out, axis=-1, keepdims=True) + EPS)) * fg_ref[...]
    o_ref[...] = out


def _combine(x_mix, route_w, pos, y, mod3, final_g, n_rows, n_lat_rows, seq, n_batch, final_norm):
    d = x_mix.shape[1]
    tm = ROW_TILE
    lanes = route_w.shape[1]
    mrow = functools.partial(_mod_row, tile=tm, n_lat_rows=n_lat_rows, seq=seq, n_batch=n_batch)
    return pl.pallas_call(
        functools.partial(_combine_kernel, final_norm=final_norm),
        grid_spec=pltpu.PrefetchScalarGridSpec(
            num_scalar_prefetch=1,
            grid=(n_rows // tm,),
            in_specs=[
                pl.BlockSpec((tm, d), lambda i, p: (i, 0)),
                pl.BlockSpec((tm, lanes), lambda i, p: (i, 0)),
                pl.BlockSpec((None, 1, d), lambda i, p: (mrow(i), 0, 5)),
                pl.BlockSpec((1, d), lambda i, p: (0, 0)),
                pl.BlockSpec(memory_space=pl.ANY),
            ],
            out_specs=pl.BlockSpec((tm, d), lambda i, p: (i, 0)),
            scratch_shapes=[pltpu.VMEM((2, TOP_K, tm, d), F32), pltpu.SemaphoreType.DMA((2,))],
        ),
        out_shape=jax.ShapeDtypeStruct((n_rows, d), F32),
        compiler_params=_params(1),
        name="moe_combine",
    )(pos, x_mix, route_w, mod3, final_g.reshape(1, d), y)


def kernel(x, c, ctx, c_ctx, w_mod, b_mod, norm1_g, norm2_g, w_in, b_gates, mlstm_norm_g, gmlp_ws, gmlp_bs,
           gmlp_norm_g, conv_w, w_out, w_router, b_router, w_gate_e, w_up_e, w_down_e, final_g):
    n_batch, seq, d = x.shape
    ctx_len = ctx.shape[1]
    depth = w_mod.shape[0]
    wm = mlstm_norm_g.shape[1]
    n_experts = w_router.shape[1]
    n_lat_rows = n_batch * seq
    n_all_rows = n_lat_rows + n_batch * ctx_len
    n_gate_cols = N_DIRS * 2 * MLSTM_HEADS
    gate_col0 = 3 * wm
    assert seq % ROW_TILE == 0 and seq % MM_ROW_TILE == 0 and (n_batch * ctx_len) % ROW_TILE == 0
    assert n_batch + 1 <= V7X_SUBLANES and n_experts <= V7X_LANES and n_gate_cols <= V7X_LANES

    rows = jnp.concatenate([x.reshape(n_lat_rows, d), ctx.reshape(n_batch * ctx_len, d)], axis=0)

    cond = jnp.zeros((V7X_SUBLANES, d), F32).at[:n_batch].set(c).at[n_batch].set(c_ctx)
    mods = _mod_table(cond, w_mod, b_mod)

    wr = jnp.zeros((d, V7X_LANES), F32).at[:, :n_experts].set(w_router)
    wr_hi = wr.astype(BF16)
    wr2 = jnp.concatenate([wr_hi, (wr - wr_hi.astype(F32)).astype(BF16)], axis=1)
    br = jnp.zeros((1, V7X_LANES), F32).at[0, :n_experts].set(b_router)

    for layer in range(depth):
        last = layer + 1 == depth
        mod3 = mods[layer].reshape(V7X_SUBLANES, 1, N_MOD * d)
        w_in_l = w_in[layer]
        w_rest = w_in_l[:, gate_col0 + n_gate_cols:].astype(BF16)
        w_gate = jnp.zeros((d, V7X_LANES), BF16).at[:, :n_gate_cols].set(
            w_in_l[:, gate_col0:gate_col0 + n_gate_cols].astype(BF16))
        bias = jnp.zeros((1, V7X_LANES), F32).at[0, :n_gate_cols].set(b_gates[layer].reshape(-1))
        bias_t = jnp.broadcast_to(b_gates[layer].reshape(-1, 1), (n_gate_cols, MLSTM_CHUNK))

        h1 = _normmod(rows, norm1_g[layer], mod3, 0, n_lat_rows, seq, n_batch)
        z_a = _matmul_f32w(h1, w_in, layer, gate_col0, BF16)
        z_b = _matmul(h1, w_rest, BF16)
        gates, gates_t = _gates(h1, w_gate, w_gate.T)
        hf, hb = _mlstm(z_a, gates, gates_t, bias, bias_t, n_batch, seq, ctx_len, wm)

        n_rows = n_lat_rows if last else n_all_rows
        y = _mix(hf, hb, z_a, z_b, mlstm_norm_g[layer].reshape(1, wm), gmlp_ws[layer].astype(BF16),
                 gmlp_bs[layer].T, gmlp_norm_g[layer].reshape(1, -1), conv_w[layer], n_rows, n_lat_rows, ctx_len)
        x_mix = _wout(y, w_out, layer, rows, mod3, n_lat_rows, seq, n_batch)

        h2, route_w, route_e = _router(x_mix, norm2_g[layer], mod3, wr2, br, n_experts, n_lat_rows, seq, n_batch)
        tables = _dispatch_tables(route_e[:, :TOP_K], n_experts, MOE_ROW_TILE)
        xs = _dispatch(h2, tables, n_experts)
        act = _moe_up(xs, tables, w_gate_e, w_up_e, layer, MOE_ROW_TILE)
        y_moe = _moe_down(act, tables, w_down_e, layer, MOE_ROW_TILE)
        rows = _combine(x_mix, route_w, tables["dest"], y_moe, mod3, final_g, n_rows, n_lat_rows, seq, n_batch, last)

    return rows.reshape(n_batch, seq, d)
```

```python
import functools

import jax
import jax.numpy as jnp
from jax import lax
from jax.experimental import pallas as pl
from jax.experimental.pallas import tpu as pltpu

F32 = jnp.float32
BF16 = jnp.bfloat16

N_MOD = 6
EPS = 1e-6
N_DIRS = 2
MLSTM_HEADS = 4
MLSTM_CHUNK = 128
GMLP_HEADS = 4
GMLP_CHUNK = 128
GRID_W = 64
N_EXPERT_GROUPS = 4
TOP_K = 2

V7X_LANES = 128
V7X_SUBLANES = 8
V7X_VMEM_BYTES = 64 * 1024 * 1024
VMEM_LIMIT = V7X_VMEM_BYTES - 8 * 1024 * 1024

ROW_TILE = 256
MM_ROW_TILE = 512
MOE_ROW_TILE = 256
MOE_UP_SPLITS = 2


def _params(n_axes):
    return pltpu.CompilerParams(
        dimension_semantics=("arbitrary",) * n_axes, vmem_limit_bytes=VMEM_LIMIT)


def _largest_divisor(n, candidates):
    for c in candidates:
        if n % c == 0:
            return c
    raise ValueError(f"no tile in {candidates} divides {n}")


def _mod_row(i, tile, n_lat_rows, seq, n_batch):
    return jnp.where(i * tile < n_lat_rows, (i * tile) // seq, n_batch)


def _mod_kernel(a_ref, w_ref, b_ref, o_ref):
    a = a_ref[...]
    a = a * jax.nn.sigmoid(a)
    acc = jnp.dot(a.astype(BF16), w_ref[...].astype(BF16), preferred_element_type=F32)
    o_ref[...] = acc + b_ref[...]


def _mod_table(a, w_mod, b_mod):
    depth, d, n = w_mod.shape
    tn = _largest_divisor(n, (1024, 512, 256, 128))
    rows = a.shape[0]
    return pl.pallas_call(
        _mod_kernel,
        grid=(depth, n // tn),
        in_specs=[
            pl.BlockSpec((rows, d), lambda l, j: (0, 0)),
            pl.BlockSpec((None, d, tn), lambda l, j: (l, 0, j)),
            pl.BlockSpec((None, 1, tn), lambda l, j: (l, 0, j)),
        ],
        out_specs=pl.BlockSpec((None, rows, tn), lambda l, j: (l, 0, j)),
        out_shape=jax.ShapeDtypeStruct((depth, rows, n), F32),
        compiler_params=_params(2),
        name="mod_table",
    )(a, w_mod, b_mod.reshape(depth, 1, n))


def _norm_modulate(x, g, shift, scale):
    y = x * lax.rsqrt(jnp.mean(x * x, axis=-1, keepdims=True) + EPS)
    return (y * g) * (1.0 + scale) + shift


def _row_specs(src, tm, tn, n_lat_rows, col):
    lat, ctx, ctx_row0 = src
    n_lat_tiles = n_lat_rows // tm
    assert n_lat_rows % tm == 0 and ctx_row0 % tm == 0
    return [
        pl.BlockSpec((tm, tn), lambda *g: (jnp.minimum(g[-1], n_lat_tiles - 1), col(*g))),
        pl.BlockSpec((tm, tn), lambda *g: (ctx_row0 // tm + jnp.maximum(g[-1] - n_lat_tiles, 0), col(*g))),
    ]


def _normmod_kernel(xl_ref, xc_ref, g_ref, sh_ref, sc_ref, o_ref, *, n_lat_tiles):
    x = jnp.where(pl.program_id(0) < n_lat_tiles, xl_ref[...], xc_ref[...])
    o_ref[...] = _norm_modulate(x, g_ref[...], sh_ref[...], sc_ref[...]).astype(o_ref.dtype)


def _normmod(src, n_rows, g, mod3, shift_col, n_lat_rows, seq, n_batch):
    d = g.shape[0]
    mrow = functools.partial(_mod_row, tile=ROW_TILE, n_lat_rows=n_lat_rows, seq=seq, n_batch=n_batch)
    return pl.pallas_call(
        functools.partial(_normmod_kernel, n_lat_tiles=n_lat_rows // ROW_TILE),
        grid=(n_rows // ROW_TILE,),
        in_specs=_row_specs(src, ROW_TILE, d, n_lat_rows, lambda i: 0) + [
            pl.BlockSpec((1, d), lambda i: (0, 0)),
            pl.BlockSpec((None, 1, d), lambda i: (mrow(i), 0, shift_col)),
            pl.BlockSpec((None, 1, d), lambda i: (mrow(i), 0, shift_col + 1)),
        ],
        out_specs=pl.BlockSpec((ROW_TILE, d), lambda i: (i, 0)),
        out_shape=jax.ShapeDtypeStruct((n_rows, d), BF16),
        compiler_params=_params(1),
        name="norm_modulate",
    )(src[0], src[1], g.reshape(1, d), mod3, mod3)


def _mm_kernel(a_ref, b_ref, o_ref):
    o_ref[...] = jnp.dot(a_ref[...], b_ref[...], preferred_element_type=F32).astype(o_ref.dtype)


def _matmul(a, b, out_dtype):
    m, k = a.shape
    n = b.shape[1]
    tm = _largest_divisor(m, (MM_ROW_TILE, 256, 128))
    tn = _largest_divisor(n, (1024, 512, 256, 128))
    return pl.pallas_call(
        _mm_kernel,
        grid=(n // tn, m // tm),
        in_specs=[
            pl.BlockSpec((tm, k), lambda j, i: (i, 0)),
            pl.BlockSpec((k, tn), lambda j, i: (0, j)),
        ],
        out_specs=pl.BlockSpec((tm, tn), lambda j, i: (i, j)),
        out_shape=jax.ShapeDtypeStruct((m, n), out_dtype),
        compiler_params=_params(2),
        name="projection",
    )(a, b)


def _mm_wt_kernel(a_ref, w_ref, o_ref, wb_scr):
    @pl.when(pl.program_id(1) == 0)
    def _():
        wb_scr[...] = w_ref[...].astype(BF16)

    acc = lax.dot_general(a_ref[...], wb_scr[...], (((1,), (1,)), ((), ())), preferred_element_type=F32)
    o_ref[...] = acc.astype(o_ref.dtype)


def _matmul_wt(a, wt, row0, n_cols, out_dtype):
    m, k = a.shape
    tm = _largest_divisor(m, (MM_ROW_TILE, 256, 128))
    tn = _largest_divisor(n_cols, (1024, 512, 256, 128))
    return pl.pallas_call(
        _mm_wt_kernel,
        grid=(n_cols // tn, m // tm),
        in_specs=[
            pl.BlockSpec((tm, k), lambda j, i: (i, 0)),
            pl.BlockSpec((pl.Element(tn), pl.Element(k)),
                         lambda j, i: (pl.multiple_of(row0 + j * tn, V7X_SUBLANES), 0)),
        ],
        out_specs=pl.BlockSpec((tm, tn), lambda j, i: (i, j)),
        out_shape=jax.ShapeDtypeStruct((m, n_cols), out_dtype),
        scratch_shapes=[pltpu.VMEM((tn, k), BF16)],
        compiler_params=_params(2),
        name="projection_wt",
    )(a, wt)


def _gates_kernel(a_ref, w_ref, wt_ref, g_ref, gt_ref):
    a = a_ref[...]
    g_ref[...] = jnp.dot(a, w_ref[...], preferred_element_type=F32)
    gt_ref[...] = lax.dot_general(wt_ref[...], a, (((1,), (1,)), ((), ())), preferred_element_type=F32)


def _gates(a, w_gate, w_gate_t):
    m, k = a.shape
    lanes = w_gate.shape[1]
    tm = _largest_divisor(m, (MM_ROW_TILE, 256, 128))
    return pl.pallas_call(
        _gates_kernel,
        grid=(m // tm,),
        in_specs=[
            pl.BlockSpec((tm, k), lambda i: (i, 0)),
            pl.BlockSpec((k, lanes), lambda i: (0, 0)),
            pl.BlockSpec((lanes, k), lambda i: (0, 0)),
        ],
        out_specs=[
            pl.BlockSpec((tm, lanes), lambda i: (i, 0)),
            pl.BlockSpec((lanes, tm), lambda i: (0, i)),
        ],
        out_shape=[jax.ShapeDtypeStruct((m, lanes), F32), jax.ShapeDtypeStruct((lanes, m), F32)],
        compiler_params=_params(1),
        name="gate_projection",
    )(a, w_gate, w_gate_t)


def _log_sigmoid(x):
    return jnp.minimum(x, 0.0) - jnp.log1p(jnp.exp(-jnp.abs(x)))


def _mlstm_chain(q, k, v, i_col, f_col, i_row, f_row, before, after, c_ref, m_ref, o_ref, cols):
    chunk, dk = q.shape
    dv = v.shape[1]
    lf_col = _log_sigmoid(f_col)
    lf_row = _log_sigmoid(f_row)
    b_col = jnp.sum(jnp.where(before, lf_row, 0.0), axis=1, keepdims=True)
    b_row = jnp.sum(jnp.where(after, lf_col, 0.0), axis=0, keepdims=True)
    b_last = jnp.sum(lf_row, axis=1, keepdims=True)

    m_prev = m_ref[0:1, 0:1]
    dmat = jnp.where(before, b_col - b_row + i_row, -jnp.inf)
    m_inter = b_col + m_prev
    m_t = jnp.maximum(jnp.max(dmat, axis=1, keepdims=True), m_inter)
    q = (q.astype(F32) * (dk ** -0.5)).astype(BF16)
    qk = lax.dot_general(q, k, (((1,), (1,)), ((), ())), preferred_element_type=F32)
    s = (qk * jnp.exp(dmat - m_t)).astype(BF16)
    v_aug = jnp.concatenate([v, jnp.ones((chunk, V7X_LANES), BF16)], axis=1)
    c_prev = c_ref[...]
    inter = jnp.exp(m_inter - m_t)
    num = (jnp.dot(s, v_aug, preferred_element_type=F32)
           + inter * jnp.dot(q, c_prev.astype(BF16), preferred_element_type=F32))
    den = num[:, dv:dv + 1]
    o_ref[:, cols] = (num[:, :dv] / jnp.maximum(jnp.abs(den), jnp.exp(-m_t))).astype(o_ref.dtype)

    g_end = b_last - b_col + i_col
    m_new = jnp.maximum(b_last + m_prev, jnp.max(g_end, axis=0, keepdims=True))
    decay = jnp.exp(b_last + m_prev - m_new)
    wv = (jnp.exp(g_end - m_new) * v_aug.astype(F32)).astype(BF16)
    c_ref[...] = decay * c_prev + lax.dot_general(k, wv, (((0,), (0,)), ((), ())), preferred_element_type=F32)
    m_ref[...] = jnp.broadcast_to(m_new, m_ref.shape)


def _mlstm_kernel(qf_ref, kf_ref, vf_ref, gf_ref, gtf_ref, qb_ref, kb_ref, vb_ref, gb_ref, gtb_ref,
                  b_ref, bt_ref, of_ref, ob_ref, c_scr, m_scr):
    nh = MLSTM_HEADS
    chunk = qf_ref.shape[0]
    dk = qf_ref.shape[1] // nh
    dv = vf_ref.shape[1] // nh

    @pl.when(pl.program_id(1) == 0)
    def _():
        c_scr[...] = jnp.zeros_like(c_scr)
        m_scr[...] = jnp.zeros_like(m_scr)

    t_idx = lax.broadcasted_iota(jnp.int32, (chunk, chunk), 0)
    s_idx = lax.broadcasted_iota(jnp.int32, (chunk, chunk), 1)
    dirs = ((qf_ref, kf_ref, vf_ref, gf_ref, gtf_ref, of_ref), (qb_ref, kb_ref, vb_ref, gb_ref, gtb_ref, ob_ref))
    for d, (q_ref, k_ref, v_ref, g_ref, gt_ref, o_ref) in enumerate(dirs):
        before = s_idx <= t_idx if d == 0 else s_idx >= t_idx
        after = s_idx >= t_idx if d == 0 else s_idx <= t_idx
        g = g_ref[...] + b_ref[...]
        gt = gt_ref[...] + bt_ref[...]
        for h in range(nh):
            ci = d * 2 * nh + h
            cf = ci + nh
            _mlstm_chain(
                q_ref[:, h * dk:(h + 1) * dk], k_ref[:, h * dk:(h + 1) * dk], v_ref[:, h * dv:(h + 1) * dv],
                g[:, ci:ci + 1], g[:, cf:cf + 1], gt[ci:ci + 1, :], gt[cf:cf + 1, :], before, after,
                c_scr.at[d * nh + h], m_scr.at[d * nh + h], o_ref, slice(h * dv, (h + 1) * dv))


def _mlstm(z, gates, gates_t, bias, bias_t, n_batch, seq, ctx_len, width):
    r = z.shape[0]
    nh = MLSTM_HEADS
    dv = width // nh
    dk = dv // 2
    qk_w = nh * dk
    chunk = MLSTM_CHUNK
    n_ctx = ctx_len // chunk
    n_lat = seq // chunk
    n_gate_rows = bias_t.shape[0]

    def tile(d, b, step):
        lat = step - n_ctx
        ctx_tile = n_batch * n_lat + b * n_ctx + (step if d == 0 else n_ctx - 1 - step)
        lat_tile = b * n_lat + (lat if d == 0 else n_lat - 1 - lat)
        return jnp.where(step < n_ctx, ctx_tile, lat_tile)

    def dir_specs(d):
        return [
            pl.BlockSpec((chunk, qk_w), lambda b, s: (tile(d, b, s), 0)),
            pl.BlockSpec((chunk, qk_w), lambda b, s: (tile(d, b, s), 1)),
            pl.BlockSpec((chunk, width), lambda b, s: (tile(d, b, s), 1)),
            pl.BlockSpec((chunk, gates.shape[1]), lambda b, s: (tile(d, b, s), 0)),
            pl.BlockSpec((n_gate_rows, chunk), lambda b, s: (0, tile(d, b, s))),
        ]

    return pl.pallas_call(
        _mlstm_kernel,
        grid=(n_batch, n_ctx + n_lat),
        in_specs=dir_specs(0) + dir_specs(1) + [
            pl.BlockSpec(bias.shape, lambda b, s: (0, 0)),
            pl.BlockSpec(bias_t.shape, lambda b, s: (0, 0)),
        ],
        out_specs=[
            pl.BlockSpec((chunk, width), lambda b, s: (tile(0, b, s), 0)),
            pl.BlockSpec((chunk, width), lambda b, s: (tile(1, b, s), 0)),
        ],
        out_shape=[jax.ShapeDtypeStruct((r, width), BF16), jax.ShapeDtypeStruct((r, width), BF16)],
        scratch_shapes=[
            pltpu.VMEM((N_DIRS * nh, dk, dv + V7X_LANES), F32),
            pltpu.VMEM((N_DIRS * nh, V7X_SUBLANES, V7X_LANES), F32),
        ],
        compiler_params=_params(2),
        name="mlstm_scan",
    )(z, z, z, gates, gates_t, z, z, z, gates, gates_t, bias, bias_t)


def _mix_kernel(hf_ref, hb_ref, o_ref, u_ref, v_ref, bg_ref, cg_ref, xin_ref, mg_ref, ws_ref, bst_ref, gg_ref,
                cw_ref, y_ref, *, n_lat_rows, ctx_len):
    i = pl.program_id(0)
    tm = y_ref.shape[0]
    wm = o_ref.shape[1]
    gw = u_ref.shape[1]
    dv = wm // MLSTM_HEADS
    hdim = gw // GMLP_HEADS

    for hd in range(MLSTM_HEADS):
        cols = slice(hd * dv, (hd + 1) * dv)
        hh = hf_ref[:, cols].astype(F32) + hb_ref[:, cols].astype(F32)
        hh = hh * lax.rsqrt(jnp.mean(hh * hh, axis=-1, keepdims=True) + EPS)
        out = (hh * mg_ref[:, cols]) * jax.nn.sigmoid(o_ref[:, cols].astype(F32))
        y_ref[:, cols] = out.astype(y_ref.dtype)

    u = jax.nn.gelu(u_ref[...].astype(F32))
    v = jax.nn.gelu(v_ref[...].astype(F32))
    v = (v * lax.rsqrt(jnp.mean(v * v, axis=-1, keepdims=True) + EPS)) * gg_ref[...]
    vb = v.astype(BF16)
    for c in range(tm // GMLP_CHUNK):
        rows = slice(c * GMLP_CHUNK, (c + 1) * GMLP_CHUNK)
        for hd in range(GMLP_HEADS):
            cols = slice(hd * hdim, (hd + 1) * hdim)
            sp = jnp.dot(ws_ref[hd], vb[rows, cols], preferred_element_type=F32) + bst_ref[:, hd:hd + 1]
            y_ref[rows, wm + hd * hdim:wm + (hd + 1) * hdim] = (u[rows, cols] * sp).astype(y_ref.dtype)

    yv = cg_ref[...].astype(F32) * xin_ref[...].astype(F32)
    t = lax.broadcasted_iota(jnp.int32, (tm, 1), 0)
    is_lat = i * tm < n_lat_rows
    pos = jnp.where(is_lat, t % GRID_W, t % ctx_len)
    last = jnp.where(is_lat, GRID_W - 1, ctx_len - 1)
    prev = jnp.where(pos != 0, pltpu.roll(yv, 1, axis=0), 0.0)
    nxt = jnp.where(pos != last, pltpu.roll(yv, tm - 1, axis=0), 0.0)
    conv = cw_ref[0:1, :] * prev + cw_ref[1:2, :] * yv + cw_ref[2:3, :] * nxt
    y_ref[:, wm + gw:] = (bg_ref[...].astype(F32) * conv).astype(y_ref.dtype)


def _mix(hf, hb, z_a, z_b, mg, ws, bst, gg, cw, n_rows, n_lat_rows, ctx_len):
    wm = mg.shape[1]
    gw = gg.shape[1]
    cwid = cw.shape[1]
    tm = ROW_TILE
    assert tm % GRID_W == 0 and tm % ctx_len == 0 and tm % GMLP_CHUNK == 0
    assert gw == cwid
    full = lambda i: (0, 0)
    return pl.pallas_call(
        functools.partial(_mix_kernel, n_lat_rows=n_lat_rows, ctx_len=ctx_len),
        grid=(n_rows // tm,),
        in_specs=[
            pl.BlockSpec((tm, wm), lambda i: (i, 0)),
            pl.BlockSpec((tm, wm), lambda i: (i, 0)),
            pl.BlockSpec((tm, wm), lambda i: (i, 2)),
            pl.BlockSpec((tm, gw), lambda i: (i, 0)),
            pl.BlockSpec((tm, gw), lambda i: (i, 1)),
            pl.BlockSpec((tm, gw), lambda i: (i, 2)),
            pl.BlockSpec((tm, gw), lambda i: (i, 3)),
            pl.BlockSpec((tm, gw), lambda i: (i, 4)),
            pl.BlockSpec(mg.shape, full),
            pl.BlockSpec(ws.shape, lambda i: (0, 0, 0)),
            pl.BlockSpec(bst.shape, full),
            pl.BlockSpec(gg.shape, full),
            pl.BlockSpec(cw.shape, full),
        ],
        out_specs=pl.BlockSpec((tm, wm + gw + cwid), lambda i: (i, 0)),
        out_shape=jax.ShapeDtypeStruct((n_rows, wm + gw + cwid), BF16),
        compiler_params=_params(1),
        name="mixer_epilogue",
    )(hf, hb, z_a, z_b, z_b, z_b, z_b, z_b, mg, ws, bst, gg, cw)


def _wout_kernel(y_ref, w_ref, xl_ref, xc_ref, g_ref, o_ref, wb_scr, *, n_lat_tiles):
    @pl.when(pl.program_id(1) == 0)
    def _():
        wb_scr[...] = w_ref[...].astype(BF16)

    x = jnp.where(pl.program_id(1) < n_lat_tiles, xl_ref[...], xc_ref[...])
    acc = jnp.dot(y_ref[...], wb_scr[...], preferred_element_type=F32)
    o_ref[...] = x + g_ref[...] * acc


def _wout(y, w3, layer, src, mod3, n_lat_rows, seq, n_batch):
    m, k = y.shape
    d = w3.shape[2]
    tm = _largest_divisor(m, (MM_ROW_TILE, 256))
    tn = _largest_divisor(d, (1024, 512, 256, 128))
    mrow = functools.partial(_mod_row, tile=tm, n_lat_rows=n_lat_rows, seq=seq, n_batch=n_batch)
    gate_blk0 = (2 * d) // tn
    return pl.pallas_call(
        functools.partial(_wout_kernel, n_lat_tiles=n_lat_rows // tm),
        grid=(d // tn, m // tm),
        in_specs=[
            pl.BlockSpec((tm, k), lambda j, i: (i, 0)),
            pl.BlockSpec((None, k, tn), lambda j, i: (layer, 0, j), pipeline_mode=pl.Buffered(1)),
        ] + _row_specs(src, tm, tn, n_lat_rows, lambda j, i: j) + [
            pl.BlockSpec((None, 1, tn), lambda j, i: (mrow(i), 0, gate_blk0 + j)),
        ],
        out_specs=pl.BlockSpec((tm, tn), lambda j, i: (i, j)),
        out_shape=jax.ShapeDtypeStruct((m, d), F32),
        scratch_shapes=[pltpu.VMEM((k, tn), BF16)],
        compiler_params=_params(2),
        name="out_projection",
    )(y, w3, src[0], src[1], mod3)


def _router_kernel(x_ref, g_ref, sh_ref, sc_ref, wr_ref, br_ref, h_ref, w_ref, e_ref, *, n_experts):
    h = _norm_modulate(x_ref[...], g_ref[...], sh_ref[...], sc_ref[...])
    h_ref[...] = h
    tm = h.shape[0]
    lanes = w_ref.shape[1]
    h_hi = h.astype(BF16)
    h_lo = (h - h_hi.astype(F32)).astype(BF16)
    prod = jnp.dot(jnp.concatenate([h_hi, h_lo], axis=0), wr_ref[...], preferred_element_type=F32)
    logits = (prod[:tm, :lanes] + (prod[:tm, lanes:] + prod[tm:, :lanes])) + prod[tm:, lanes:]
    scores = jax.nn.sigmoid(logits)
    sel = scores + br_ref[...]
    epg = n_experts // N_EXPERT_GROUPS
    sel_c = [sel[:, e:e + 1] for e in range(n_experts)]
    sc_c = [scores[:, e:e + 1] for e in range(n_experts)]

    best_g = None
    for gi in range(N_EXPERT_GROUPS):
        mem = sel_c[gi * epg:(gi + 1) * epg]
        gs = None
        for a in range(epg):
            for b in range(a + 1, epg):
                pair = mem[a] + mem[b]
                gs = pair if gs is None else jnp.maximum(gs, pair)
        if best_g is None:
            best_g, gidx = gs, jnp.zeros_like(gs, dtype=jnp.int32)
        else:
            upd = gs > best_g
            gidx = jnp.where(upd, gi, gidx)
            best_g = jnp.where(upd, gs, best_g)

    in_sel, in_sc = [], []
    for a in range(epg):
        vs, vr = sel_c[a], sc_c[a]
        for gi in range(1, N_EXPERT_GROUPS):
            vs = jnp.where(gidx == gi, sel_c[gi * epg + a], vs)
            vr = jnp.where(gidx == gi, sc_c[gi * epg + a], vr)
        in_sel.append(vs)
        in_sc.append(vr)

    v1, i1, s1 = in_sel[0], jnp.zeros_like(gidx), in_sc[0]
    for a in range(1, epg):
        upd = in_sel[a] > v1
        i1 = jnp.where(upd, a, i1)
        s1 = jnp.where(upd, in_sc[a], s1)
        v1 = jnp.where(upd, in_sel[a], v1)
    v2 = jnp.full_like(v1, -jnp.inf)
    i2 = jnp.zeros_like(gidx)
    s2 = jnp.zeros_like(s1)
    for a in range(epg):
        upd = jnp.logical_and(i1 != a, in_sel[a] > v2)
        i2 = jnp.where(upd, a, i2)
        s2 = jnp.where(upd, in_sc[a], s2)
        v2 = jnp.where(upd, in_sel[a], v2)
    denom = s1 + s2
    lane = lax.broadcasted_iota(jnp.int32, w_ref.shape, 1)
    w_ref[...] = jnp.where(lane == 0, s1 / denom, jnp.where(lane == 1, s2 / denom, 0.0))
    e_ref[...] = jnp.where(lane == 0, gidx * epg + i1, jnp.where(lane == 1, gidx * epg + i2, 0))


def _router(x_mix, g, mod3, wr2, br, n_experts, n_lat_rows, seq, n_batch):
    r, d = x_mix.shape
    tm = ROW_TILE
    mrow = functools.partial(_mod_row, tile=tm, n_lat_rows=n_lat_rows, seq=seq, n_batch=n_batch)
    lanes = br.shape[1]
    return pl.pallas_call(
        functools.partial(_router_kernel, n_experts=n_experts),
        grid=(r // tm,),
        in_specs=[
            pl.BlockSpec((tm, d), lambda i: (i, 0)),
            pl.BlockSpec((1, d), lambda i: (0, 0)),
            pl.BlockSpec((None, 1, d), lambda i: (mrow(i), 0, 3)),
            pl.BlockSpec((None, 1, d), lambda i: (mrow(i), 0, 4)),
            pl.BlockSpec(wr2.shape, lambda i: (0, 0)),
            pl.BlockSpec(br.shape, lambda i: (0, 0)),
        ],
        out_specs=[
            pl.BlockSpec((tm, d), lambda i: (i, 0)),
            pl.BlockSpec((tm, lanes), lambda i: (i, 0)),
            pl.BlockSpec((tm, lanes), lambda i: (i, 0)),
        ],
        out_shape=[
            jax.ShapeDtypeStruct((r, d), F32),
            jax.ShapeDtypeStruct((r, lanes), F32),
            jax.ShapeDtypeStruct((r, lanes), jnp.int32),
        ],
        compiler_params=_params(1),
        name="moe_router",
    )(x_mix, g.reshape(1, d), mod3, mod3, wr2, br)


def _dispatch_tables(e_pair, n_experts, tm):
    flat_e = e_pair.reshape(-1)
    n_pairs = flat_e.shape[0]
    n_rows = n_pairs + n_experts * tm
    n_tiles = n_rows // tm
    onehot = (flat_e[:, None] == jnp.arange(n_experts, dtype=jnp.int32)[None, :]).astype(jnp.int32)
    rank = jnp.sum((jnp.cumsum(onehot, axis=0) - onehot) * onehot, axis=1)
    counts = jnp.sum(onehot, axis=0)
    padded = ((counts + tm - 1) // tm) * tm
    ends = jnp.cumsum(padded)
    starts = ends - padded
    dest = jnp.sum(onehot * starts[None, :], axis=1) + rank
    n_used = (ends[-1] // tm).astype(jnp.int32)
    tile_start = jnp.minimum(jnp.arange(n_tiles, dtype=jnp.int32), n_used - 1) * tm
    tile_expert = jnp.sum((tile_start[:, None] >= ends[None, :]).astype(jnp.int32), axis=1).astype(jnp.int32)
    tile_first = jnp.concatenate(
        [jnp.ones((1,), jnp.int32), (tile_expert[1:] != tile_expert[:-1]).astype(jnp.int32)])
    return dict(dest=dest.astype(jnp.int32), pad_lo=(starts + counts).astype(jnp.int32),
                pad_hi=ends.astype(jnp.int32), tile_expert=tile_expert, tile_first=tile_first,
                n_used=n_used.reshape(1), n_rows=n_rows)


def _dispatch_kernel(dest_ref, plo_ref, phi_ref, h_ref, xs_hbm, zrow, sem, zsem, *, n_experts):
    i = pl.program_id(0)
    tm = h_ref.shape[0]

    @pl.when(i == 0)
    def _():
        zrow[...] = jnp.zeros_like(zrow)

        def zero_copy(r):
            return pltpu.make_async_copy(zrow.at[pl.ds(0, 1)], xs_hbm.at[pl.ds(r, 1)], zsem.at[0])

        def start(r, carry):
            zero_copy(r).start()
            return carry

        def wait(r, carry):
            zero_copy(r).wait()
            return carry

        def tile_copy(t):
            return pltpu.make_async_copy(zrow, xs_hbm.at[pl.ds(t * tm, tm)], zsem.at[0])

        def start_tile(t, carry):
            tile_copy(t).start()
            return carry

        def wait_tile(t, carry):
            tile_copy(t).wait()
            return carry

        n_tiles = xs_hbm.shape[0] // tm
        first_unused = phi_ref[n_experts - 1] // tm
        for e in range(n_experts):
            lax.fori_loop(plo_ref[e], phi_ref[e], start, 0)
        lax.fori_loop(first_unused, n_tiles, start_tile, 0)
        for e in range(n_experts):
            lax.fori_loop(plo_ref[e], phi_ref[e], wait, 0)
        lax.fori_loop(first_unused, n_tiles, wait_tile, 0)

    def scatter(r, carry):
        for k in range(TOP_K):
            row = dest_ref[(i * tm + r) * TOP_K + k]
            pltpu.make_async_copy(h_ref.at[pl.ds(r, 1)], xs_hbm.at[pl.ds(row, 1)], sem.at[0]).start()
        return carry

    lax.fori_loop(0, tm, scatter, 0)
    for k in range(TOP_K):
        pltpu.make_async_copy(h_ref, xs_hbm.at[pl.ds(0, tm)], sem.at[0]).wait()


def _dispatch(h2, tables, n_experts):
    r, d = h2.shape
    tm = ROW_TILE
    assert MOE_ROW_TILE % tm == 0
    return pl.pallas_call(
        functools.partial(_dispatch_kernel, n_experts=n_experts),
        grid_spec=pltpu.PrefetchScalarGridSpec(
            num_scalar_prefetch=3,
            grid=(r // tm,),
            in_specs=[pl.BlockSpec((tm, d), lambda i, de, lo, hi: (i, 0))],
            out_specs=pl.BlockSpec(memory_space=pl.ANY),
            scratch_shapes=[pltpu.VMEM((tm, d), F32), pltpu.SemaphoreType.DMA((1,)),
                            pltpu.SemaphoreType.DMA((1,))],
        ),
        out_shape=jax.ShapeDtypeStruct((tables["n_rows"], d), F32),
        compiler_params=_params(1),
        name="moe_dispatch",
    )(tables["dest"], tables["pad_lo"], tables["pad_hi"], h2)


def _moe_up_kernel(te_ref, tf_ref, nu_ref, x_ref, wg_ref, wu_ref, a_ref, wgb, wub):
    i = pl.program_id(1)

    @pl.when(i < nu_ref[0])
    def _():
        @pl.when(tf_ref[i] == 1)
        def _():
            wgb[...] = wg_ref[...].astype(BF16)
            wub[...] = wu_ref[...].astype(BF16)

        x = x_ref[...].astype(BF16)
        gate = jnp.dot(x, wgb[...], preferred_element_type=F32)
        up = jnp.dot(x, wub[...], preferred_element_type=F32)
        a_ref[...] = ((gate * jax.nn.sigmoid(gate)) * up).astype(a_ref.dtype)

    @pl.when(i >= nu_ref[0])
    def _():
        a_ref[...] = jnp.zeros_like(a_ref)


def _moe_up(xs, tables, wg4, wu4, layer, tm):
    n_rows, d = xs.shape
    f = wg4.shape[3]
    fs = f // MOE_UP_SPLITS
    used = lambda i, nu: jnp.minimum(i, nu[0] - 1)
    return pl.pallas_call(
        _moe_up_kernel,
        grid_spec=pltpu.PrefetchScalarGridSpec(
            num_scalar_prefetch=3,
            grid=(MOE_UP_SPLITS, n_rows // tm),
            in_specs=[
                pl.BlockSpec((tm, d), lambda j, i, te, tf, nu: (used(i, nu), 0)),
                pl.BlockSpec((None, None, d, fs), lambda j, i, te, tf, nu: (layer, te[i], 0, j)),
                pl.BlockSpec((None, None, d, fs), lambda j, i, te, tf, nu: (layer, te[i], 0, j)),
            ],
            out_specs=pl.BlockSpec((tm, fs), lambda j, i, te, tf, nu: (i, j)),
            scratch_shapes=[pltpu.VMEM((d, fs), BF16), pltpu.VMEM((d, fs), BF16)],
        ),
        out_shape=jax.ShapeDtypeStruct((n_rows, f), BF16),
        compiler_params=_params(2),
        name="moe_up",
    )(tables["tile_expert"], tables["tile_first"], tables["n_used"], xs, wg4, wu4)


def _moe_down_kernel(te_ref, tf_ref, nu_ref, a_ref, wd_ref, y_ref, wdb):
    i = pl.program_id(0)

    @pl.when(i < nu_ref[0])
    def _():
        @pl.when(tf_ref[i] == 1)
        def _():
            wdb[...] = wd_ref[...].astype(BF16)

        y_ref[...] = jnp.dot(a_ref[...], wdb[...], preferred_element_type=F32)

    @pl.when(i >= nu_ref[0])
    def _():
        y_ref[...] = jnp.zeros_like(y_ref)


def _moe_down(a, tables, wd4, layer, tm):
    n_rows, f = a.shape
    d = wd4.shape[3]
    return pl.pallas_call(
        _moe_down_kernel,
        grid_spec=pltpu.PrefetchScalarGridSpec(
            num_scalar_prefetch=3,
            grid=(n_rows // tm,),
            in_specs=[
                pl.BlockSpec((tm, f), lambda i, te, tf, nu: (i, 0)),
                pl.BlockSpec((None, None, f, d), lambda i, te, tf, nu: (layer, te[i], 0, 0)),
            ],
            out_specs=pl.BlockSpec((tm, d), lambda i, te, tf, nu: (i, 0)),
            scratch_shapes=[pltpu.VMEM((f, d), BF16)],
        ),
        out_shape=jax.ShapeDtypeStruct((n_rows, d), F32),
        compiler_params=_params(1),
        name="moe_down",
    )(tables["tile_expert"], tables["tile_first"], tables["n_used"], a, wd4)


def _combine_kernel(pos_ref, x_ref, w_ref, g_ref, fg_ref, y_hbm, o_ref, ybuf, sem, *, final_norm):
    i = pl.program_id(0)
    n = pl.num_programs(0)
    tm = x_ref.shape[0]

    def gather(tile, slot):
        def body(r, carry):
            for k in range(TOP_K):
                row = pos_ref[(tile * tm + r) * TOP_K + k]
                pltpu.make_async_copy(y_hbm.at[pl.ds(row, 1)], ybuf.at[slot, k, pl.ds(r, 1)], sem.at[slot]).start()
            return carry
        lax.fori_loop(0, tm, body, 0)

    @pl.when(i == 0)
    def _():
        gather(0, 0)

    @pl.when(i + 1 < n)
    def _():
        gather(i + 1, (i + 1) % 2)

    slot = i % 2
    for k in range(TOP_K):
        pltpu.make_async_copy(y_hbm.at[pl.ds(0, tm)], ybuf.at[slot, k], sem.at[slot]).wait()
    w = w_ref[...]
    moe = w[:, 0:1] * ybuf[slot, 0] + w[:, 1:2] * ybuf[slot, 1]
    out = x_ref[...] + g_ref[...] * moe
    if final_norm:
        out = (out * lax.rsqrt(jnp.mean(out * out, axis=-1, keepdims=True) + EPS)) * fg_ref[...]
    o_ref[...] = out


def _combine(x_mix, route_w, pos, y, mod3, final_g, n_rows, n_lat_rows, seq, n_batch, final_norm):
    d = x_mix.shape[1]
    tm = ROW_TILE
    lanes = route_w.shape[1]
    mrow = functools.partial(_mod_row, tile=tm, n_lat_rows=n_lat_rows, seq=seq, n_batch=n_batch)
    return pl.pallas_call(
        functools.partial(_combine_kernel, final_norm=final_norm),
        grid_spec=pltpu.PrefetchScalarGridSpec(
            num_scalar_prefetch=1,
            grid=(n_rows // tm,),
            in_specs=[
                pl.BlockSpec((tm, d), lambda i, p: (i, 0)),
                pl.BlockSpec((tm, lanes), lambda i, p: (i, 0)),
                pl.BlockSpec((None, 1, d), lambda i, p: (mrow(i), 0, 5)),
                pl.BlockSpec((1, d), lambda i, p: (0, 0)),
                pl.BlockSpec(memory_space=pl.ANY),
            ],
            out_specs=pl.BlockSpec((tm, d), lambda i, p: (i, 0)),
            scratch_shapes=[pltpu.VMEM((2, TOP_K, tm, d), F32), pltpu.SemaphoreType.DMA((2,))],
        ),
        out_shape=jax.ShapeDtypeStruct((n_rows, d), F32),
        compiler_params=_params(1),
        name="moe_combine",
    )(pos, x_mix, route_w, mod3, final_g.reshape(1, d), y)


def kernel(x, c, ctx, c_ctx, w_mod, b_mod, norm1_g, norm2_g, w_in, b_gates, mlstm_norm_g, gmlp_ws, gmlp_bs,
           gmlp_norm_g, conv_w, w_out, w_router, b_router, w_gate_e, w_up_e, w_down_e, final_g):
    n_batch, seq, d = x.shape
    ctx_len = ctx.shape[1]
    depth = w_mod.shape[0]
    wm = mlstm_norm_g.shape[1]
    n_experts = w_router.shape[1]
    n_lat_rows = n_batch * seq
    n_all_rows = n_lat_rows + n_batch * ctx_len
    n_gate_cols = N_DIRS * 2 * MLSTM_HEADS
    gate_col0 = 3 * wm
    assert seq % ROW_TILE == 0 and seq % MM_ROW_TILE == 0 and (n_batch * ctx_len) % ROW_TILE == 0
    assert n_batch + 1 <= V7X_SUBLANES and n_experts <= V7X_LANES and n_gate_cols <= V7X_LANES

    src = (x.reshape(n_lat_rows, d), ctx.reshape(n_batch * ctx_len, d), 0)
    n_in_cols = w_in.shape[2]
    w_in_t = jnp.swapaxes(w_in, 1, 2).reshape(depth * n_in_cols, d)

    cond = jnp.zeros((V7X_SUBLANES, d), F32).at[:n_batch].set(c).at[n_batch].set(c_ctx)
    mods = _mod_table(cond, w_mod, b_mod)

    wr = jnp.zeros((d, V7X_LANES), F32).at[:, :n_experts].set(w_router)
    wr_hi = wr.astype(BF16)
    wr2 = jnp.concatenate([wr_hi, (wr - wr_hi.astype(F32)).astype(BF16)], axis=1)
    br = jnp.zeros((1, V7X_LANES), F32).at[0, :n_experts].set(b_router)

    for layer in range(depth):
        last = layer + 1 == depth
        mod3 = mods[layer].reshape(V7X_SUBLANES, 1, N_MOD * d)
        w_row0 = layer * n_in_cols
        w_gate_t = jnp.zeros((V7X_LANES, d), BF16).at[:n_gate_cols].set(
            w_in_t[w_row0 + gate_col0:w_row0 + gate_col0 + n_gate_cols].astype(BF16))
        bias = jnp.zeros((1, V7X_LANES), F32).at[0, :n_gate_cols].set(b_gates[layer].reshape(-1))
        bias_t = jnp.broadcast_to(b_gates[layer].reshape(-1, 1), (n_gate_cols, MLSTM_CHUNK))

        h1 = _normmod(src, n_all_rows, norm1_g[layer], mod3, 0, n_lat_rows, seq, n_batch)
        z_a = _matmul_wt(h1, w_in_t, w_row0, gate_col0, BF16)
        z_b = _matmul_wt(h1, w_in_t, w_row0 + gate_col0 + n_gate_cols, n_in_cols - gate_col0 - n_gate_cols, BF16)
        gates, gates_t = _gates(h1, w_gate_t.T, w_gate_t)
        hf, hb = _mlstm(z_a, gates, gates_t, bias, bias_t, n_batch, seq, ctx_len, wm)

        n_rows = n_lat_rows if last else n_all_rows
        y = _mix(hf, hb, z_a, z_b, mlstm_norm_g[layer].reshape(1, wm), gmlp_ws[layer].astype(BF16),
                 gmlp_bs[layer].T, gmlp_norm_g[layer].reshape(1, -1), conv_w[layer], n_rows, n_lat_rows, ctx_len)
        x_mix = _wout(y, w_out, layer, src, mod3, n_lat_rows, seq, n_batch)

        h2, route_w, route_e = _router(x_mix, norm2_g[layer], mod3, wr2, br, n_experts, n_lat_rows, seq, n_batch)
        tables = _dispatch_tables(route_e[:, :TOP_K], n_experts, MOE_ROW_TILE)
        xs = _dispatch(h2, tables, n_experts)
        act = _moe_up(xs, tables, w_gate_e, w_up_e, layer, MOE_ROW_TILE)
        y_moe = _moe_down(act, tables, w_down_e, layer, MOE_ROW_TILE)
        rows = _combine(x_mix, route_w, tables["dest"], y_moe, mod3, final_g, n_rows, n_lat_rows, seq, n_batch, last)
        src = (rows, rows, n_lat_rows)

    return rows.reshape(n_batch, seq, d)
```

```python
import functools

import jax
import jax.numpy as jnp
from jax import lax
from jax.experimental import pallas as pl
from jax.experimental.pallas import tpu as pltpu

F32 = jnp.float32
BF16 = jnp.bfloat16

N_MOD = 6
EPS = 1e-6
N_DIRS = 2
MLSTM_HEADS = 4
MLSTM_CHUNK = 128
GMLP_HEADS = 4
GMLP_CHUNK = 128
GRID_W = 64
N_EXPERT_GROUPS = 4
TOP_K = 2

V7X_LANES = 128
V7X_SUBLANES = 8
V7X_VMEM_BYTES = 64 * 1024 * 1024
VMEM_LIMIT = V7X_VMEM_BYTES - 8 * 1024 * 1024

ROW_TILE = 256
MM_ROW_TILE = 512
MOE_ROW_TILE = 256
MOE_UP_SPLITS = 2


def _params(n_axes):
    return pltpu.CompilerParams(
        dimension_semantics=("arbitrary",) * n_axes, vmem_limit_bytes=VMEM_LIMIT)


def _largest_divisor(n, candidates):
    for c in candidates:
        if n % c == 0:
            return c
    raise ValueError(f"no tile in {candidates} divides {n}")


def _mod_row(i, tile, n_lat_rows, seq, n_batch):
    return jnp.where(i * tile < n_lat_rows, (i * tile) // seq, n_batch)


def _mod_kernel(a_ref, w_ref, b_ref, o_ref):
    a = a_ref[...]
    a = a * jax.nn.sigmoid(a)
    acc = jnp.dot(a.astype(BF16), w_ref[...].astype(BF16), preferred_element_type=F32)
    o_ref[...] = acc + b_ref[...]


def _mod_table(a, w_mod, b_mod):
    depth, d, n = w_mod.shape
    tn = _largest_divisor(n, (1024, 512, 256, 128))
    rows = a.shape[0]
    return pl.pallas_call(
        _mod_kernel,
        grid=(depth, n // tn),
        in_specs=[
            pl.BlockSpec((rows, d), lambda l, j: (0, 0)),
            pl.BlockSpec((None, d, tn), lambda l, j: (l, 0, j)),
            pl.BlockSpec((None, 1, tn), lambda l, j: (l, 0, j)),
        ],
        out_specs=pl.BlockSpec((None, rows, tn), lambda l, j: (l, 0, j)),
        out_shape=jax.ShapeDtypeStruct((depth, rows, n), F32),
        compiler_params=_params(2),
        name="mod_table",
    )(a, w_mod, b_mod.reshape(depth, 1, n))


def _norm_modulate(x, g, shift, scale):
    y = x * lax.rsqrt(jnp.mean(x * x, axis=-1, keepdims=True) + EPS)
    return (y * g) * (1.0 + scale) + shift


def _row_specs(src, tm, tn, n_lat_rows, col):
    lat, ctx, ctx_row0 = src
    n_lat_tiles = n_lat_rows // tm
    assert n_lat_rows % tm == 0 and ctx_row0 % tm == 0
    return [
        pl.BlockSpec((tm, tn), lambda *g: (jnp.minimum(g[-1], n_lat_tiles - 1), col(*g))),
        pl.BlockSpec((tm, tn), lambda *g: (ctx_row0 // tm + jnp.maximum(g[-1] - n_lat_tiles, 0), col(*g))),
    ]


def _normmod_kernel(xl_ref, xc_ref, g_ref, sh_ref, sc_ref, o_ref, *, n_lat_tiles):
    x = jnp.where(pl.program_id(0) < n_lat_tiles, xl_ref[...], xc_ref[...])
    o_ref[...] = _norm_modulate(x, g_ref[...], sh_ref[...], sc_ref[...]).astype(o_ref.dtype)


def _normmod(src, n_rows, g, mod3, shift_col, n_lat_rows, seq, n_batch):
    d = g.shape[0]
    mrow = functools.partial(_mod_row, tile=ROW_TILE, n_lat_rows=n_lat_rows, seq=seq, n_batch=n_batch)
    return pl.pallas_call(
        functools.partial(_normmod_kernel, n_lat_tiles=n_lat_rows // ROW_TILE),
        grid=(n_rows // ROW_TILE,),
        in_specs=_row_specs(src, ROW_TILE, d, n_lat_rows, lambda i: 0) + [
            pl.BlockSpec((1, d), lambda i: (0, 0)),
            pl.BlockSpec((None, 1, d), lambda i: (mrow(i), 0, shift_col)),
            pl.BlockSpec((None, 1, d), lambda i: (mrow(i), 0, shift_col + 1)),
        ],
        out_specs=pl.BlockSpec((ROW_TILE, d), lambda i: (i, 0)),
        out_shape=jax.ShapeDtypeStruct((n_rows, d), BF16),
        compiler_params=_params(1),
        name="norm_modulate",
    )(src[0], src[1], g.reshape(1, d), mod3, mod3)


def _mm_kernel(a_ref, b_ref, o_ref):
    o_ref[...] = jnp.dot(a_ref[...], b_ref[...], preferred_element_type=F32).astype(o_ref.dtype)


def _matmul(a, b, out_dtype):
    m, k = a.shape
    n = b.shape[1]
    tm = _largest_divisor(m, (MM_ROW_TILE, 256, 128))
    tn = _largest_divisor(n, (1024, 512, 256, 128))
    return pl.pallas_call(
        _mm_kernel,
        grid=(n // tn, m // tm),
        in_specs=[
            pl.BlockSpec((tm, k), lambda j, i: (i, 0)),
            pl.BlockSpec((k, tn), lambda j, i: (0, j)),
        ],
        out_specs=pl.BlockSpec((tm, tn), lambda j, i: (i, j)),
        out_shape=jax.ShapeDtypeStruct((m, n), out_dtype),
        compiler_params=_params(2),
        name="projection",
    )(a, b)


def _mm_wt_kernel(a_ref, w_ref, o_ref, wb_scr):
    @pl.when(pl.program_id(1) == 0)
    def _():
        wb_scr[...] = w_ref[...].astype(BF16)

    acc = lax.dot_general(a_ref[...], wb_scr[...], (((1,), (1,)), ((), ())), preferred_element_type=F32)
    o_ref[...] = acc.astype(o_ref.dtype)


def _matmul_wt(a, wt, row0, n_cols, out_dtype):
    m, k = a.shape
    tm = _largest_divisor(m, (MM_ROW_TILE, 256, 128))
    tn = _largest_divisor(n_cols, (1024, 512, 256, 128))
    return pl.pallas_call(
        _mm_wt_kernel,
        grid=(n_cols // tn, m // tm),
        in_specs=[
            pl.BlockSpec((tm, k), lambda j, i: (i, 0)),
            pl.BlockSpec((pl.Element(tn), pl.Element(k)),
                         lambda j, i: (pl.multiple_of(row0 + j * tn, V7X_SUBLANES), 0)),
        ],
        out_specs=pl.BlockSpec((tm, tn), lambda j, i: (i, j)),
        out_shape=jax.ShapeDtypeStruct((m, n_cols), out_dtype),
        scratch_shapes=[pltpu.VMEM((tn, k), BF16)],
        compiler_params=_params(2),
        name="projection_wt",
    )(a, wt)


def _gates_kernel(a_ref, w_ref, wt_ref, g_ref, gt_ref):
    a = a_ref[...]
    g_ref[...] = jnp.dot(a, w_ref[...].astype(BF16), preferred_element_type=F32)
    gt_ref[...] = lax.dot_general(wt_ref[...].astype(BF16), a, (((1,), (1,)), ((), ())),
                                  preferred_element_type=F32)


def _gates(a, w_gate, w_gate_t):
    m, k = a.shape
    lanes = w_gate.shape[1]
    tm = _largest_divisor(m, (MM_ROW_TILE, 256, 128))
    return pl.pallas_call(
        _gates_kernel,
        grid=(m // tm,),
        in_specs=[
            pl.BlockSpec((tm, k), lambda i: (i, 0)),
            pl.BlockSpec((k, lanes), lambda i: (0, 0)),
            pl.BlockSpec((lanes, k), lambda i: (0, 0)),
        ],
        out_specs=[
            pl.BlockSpec((tm, lanes), lambda i: (i, 0)),
            pl.BlockSpec((lanes, tm), lambda i: (0, i)),
        ],
        out_shape=[jax.ShapeDtypeStruct((m, lanes), F32), jax.ShapeDtypeStruct((lanes, m), F32)],
        compiler_params=_params(1),
        name="gate_projection",
    )(a, w_gate, w_gate_t)


def _log_sigmoid(x):
    return jnp.minimum(x, 0.0) - jnp.log1p(jnp.exp(-jnp.abs(x)))


def _mlstm_chain(q, k, v, i_col, f_col, i_row, f_row, before, after, c_ref, m_ref, o_ref, cols):
    chunk, dk = q.shape
    dv = v.shape[1]
    lf_col = _log_sigmoid(f_col)
    lf_row = _log_sigmoid(f_row)
    b_col = jnp.sum(jnp.where(before, lf_row, 0.0), axis=1, keepdims=True)
    b_row = jnp.sum(jnp.where(after, lf_col, 0.0), axis=0, keepdims=True)
    b_last = jnp.sum(lf_row, axis=1, keepdims=True)

    m_prev = m_ref[0:1, 0:1]
    dmat = jnp.where(before, b_col - b_row + i_row, -jnp.inf)
    m_inter = b_col + m_prev
    m_t = jnp.maximum(jnp.max(dmat, axis=1, keepdims=True), m_inter)
    q = (q.astype(F32) * (dk ** -0.5)).astype(BF16)
    qk = lax.dot_general(q, k, (((1,), (1,)), ((), ())), preferred_element_type=F32)
    s = (qk * jnp.exp(dmat - m_t)).astype(BF16)
    v_aug = jnp.concatenate([v, jnp.ones((chunk, V7X_LANES), BF16)], axis=1)
    c_prev = c_ref[...]
    inter = jnp.exp(m_inter - m_t)
    num = (jnp.dot(s, v_aug, preferred_element_type=F32)
           + inter * jnp.dot(q, c_prev.astype(BF16), preferred_element_type=F32))
    den = num[:, dv:dv + 1]
    o_ref[:, cols] = (num[:, :dv] / jnp.maximum(jnp.abs(den), jnp.exp(-m_t))).astype(o_ref.dtype)

    g_end = b_last - b_col + i_col
    m_new = jnp.maximum(b_last + m_prev, jnp.max(g_end, axis=0, keepdims=True))
    decay = jnp.exp(b_last + m_prev - m_new)
    wv = (jnp.exp(g_end - m_new) * v_aug.astype(F32)).astype(BF16)
    c_ref[...] = decay * c_prev + lax.dot_general(k, wv, (((0,), (0,)), ((), ())), preferred_element_type=F32)
    m_ref[...] = jnp.broadcast_to(m_new, m_ref.shape)


def _mlstm_kernel(qf_ref, kf_ref, vf_ref, gf_ref, gtf_ref, qb_ref, kb_ref, vb_ref, gb_ref, gtb_ref,
                  b_ref, bt_ref, of_ref, ob_ref, c_scr, m_scr):
    nh = MLSTM_HEADS
    chunk = qf_ref.shape[0]
    dk = qf_ref.shape[1] // nh
    dv = vf_ref.shape[1] // nh

    @pl.when(pl.program_id(1) == 0)
    def _():
        c_scr[...] = jnp.zeros_like(c_scr)
        m_scr[...] = jnp.zeros_like(m_scr)

    t_idx = lax.broadcasted_iota(jnp.int32, (chunk, chunk), 0)
    s_idx = lax.broadcasted_iota(jnp.int32, (chunk, chunk), 1)
    dirs = ((qf_ref, kf_ref, vf_ref, gf_ref, gtf_ref, of_ref), (qb_ref, kb_ref, vb_ref, gb_ref, gtb_ref, ob_ref))
    for d, (q_ref, k_ref, v_ref, g_ref, gt_ref, o_ref) in enumerate(dirs):
        before = s_idx <= t_idx if d == 0 else s_idx >= t_idx
        after = s_idx >= t_idx if d == 0 else s_idx <= t_idx
        g = g_ref[...] + b_ref[...]
        gt = gt_ref[...] + bt_ref[...]
        for h in range(nh):
            ci = d * 2 * nh + h
            cf = ci + nh
            _mlstm_chain(
                q_ref[:, h * dk:(h + 1) * dk], k_ref[:, h * dk:(h + 1) * dk], v_ref[:, h * dv:(h + 1) * dv],
                g[:, ci:ci + 1], g[:, cf:cf + 1], gt[ci:ci + 1, :], gt[cf:cf + 1, :], before, after,
                c_scr.at[d * nh + h], m_scr.at[d * nh + h], o_ref, slice(h * dv, (h + 1) * dv))


def _mlstm(z, gates, gates_t, bias, bias_t, n_batch, seq, ctx_len, width):
    r = z.shape[0]
    nh = MLSTM_HEADS
    dv = width // nh
    dk = dv // 2
    qk_w = nh * dk
    chunk = MLSTM_CHUNK
    n_ctx = ctx_len // chunk
    n_lat = seq // chunk
    n_gate_rows = bias_t.shape[0]

    def tile(d, b, step):
        lat = step - n_ctx
        ctx_tile = n_batch * n_lat + b * n_ctx + (step if d == 0 else n_ctx - 1 - step)
        lat_tile = b * n_lat + (lat if d == 0 else n_lat - 1 - lat)
        return jnp.where(step < n_ctx, ctx_tile, lat_tile)

    def dir_specs(d):
        return [
            pl.BlockSpec((chunk, qk_w), lambda b, s: (tile(d, b, s), 0)),
            pl.BlockSpec((chunk, qk_w), lambda b, s: (tile(d, b, s), 1)),
            pl.BlockSpec((chunk, width), lambda b, s: (tile(d, b, s), 1)),
            pl.BlockSpec((chunk, gates.shape[1]), lambda b, s: (tile(d, b, s), 0)),
            pl.BlockSpec((n_gate_rows, chunk), lambda b, s: (0, tile(d, b, s))),
        ]

    return pl.pallas_call(
        _mlstm_kernel,
        grid=(n_batch, n_ctx + n_lat),
        in_specs=dir_specs(0) + dir_specs(1) + [
            pl.BlockSpec(bias.shape, lambda b, s: (0, 0)),
            pl.BlockSpec(bias_t.shape, lambda b, s: (0, 0)),
        ],
        out_specs=[
            pl.BlockSpec((chunk, width), lambda b, s: (tile(0, b, s), 0)),
            pl.BlockSpec((chunk, width), lambda b, s: (tile(1, b, s), 0)),
        ],
        out_shape=[jax.ShapeDtypeStruct((r, width), BF16), jax.ShapeDtypeStruct((r, width), BF16)],
        scratch_shapes=[
            pltpu.VMEM((N_DIRS * nh, dk, dv + V7X_LANES), F32),
            pltpu.VMEM((N_DIRS * nh, V7X_SUBLANES, V7X_LANES), F32),
        ],
        compiler_params=_params(2),
        name="mlstm_scan",
    )(z, z, z, gates, gates_t, z, z, z, gates, gates_t, bias, bias_t)


def _mix_kernel(hf_ref, hb_ref, o_ref, u_ref, v_ref, bg_ref, cg_ref, xin_ref, mg_ref, ws_ref, bst_ref, gg_ref,
                cw_ref, y_ref, *, n_lat_rows, ctx_len):
    i = pl.program_id(0)
    tm = y_ref.shape[0]
    wm = o_ref.shape[1]
    gw = u_ref.shape[1]
    dv = wm // MLSTM_HEADS
    hdim = gw // GMLP_HEADS

    for hd in range(MLSTM_HEADS):
        cols = slice(hd * dv, (hd + 1) * dv)
        hh = hf_ref[:, cols].astype(F32) + hb_ref[:, cols].astype(F32)
        hh = hh * lax.rsqrt(jnp.mean(hh * hh, axis=-1, keepdims=True) + EPS)
        out = (hh * mg_ref[:, cols]) * jax.nn.sigmoid(o_ref[:, cols].astype(F32))
        y_ref[:, cols] = out.astype(y_ref.dtype)

    u = jax.nn.gelu(u_ref[...].astype(F32))
    v = jax.nn.gelu(v_ref[...].astype(F32))
    v = (v * lax.rsqrt(jnp.mean(v * v, axis=-1, keepdims=True) + EPS)) * gg_ref[...]
    vb = v.astype(BF16)
    for c in range(tm // GMLP_CHUNK):
        rows = slice(c * GMLP_CHUNK, (c + 1) * GMLP_CHUNK)
        for hd in range(GMLP_HEADS):
            cols = slice(hd * hdim, (hd + 1) * hdim)
            sp = jnp.dot(ws_ref[hd], vb[rows, cols], preferred_element_type=F32) + bst_ref[:, hd:hd + 1]
            y_ref[rows, wm + hd * hdim:wm + (hd + 1) * hdim] = (u[rows, cols] * sp).astype(y_ref.dtype)

    yv = cg_ref[...].astype(F32) * xin_ref[...].astype(F32)
    t = lax.broadcasted_iota(jnp.int32, (tm, 1), 0)
    is_lat = i * tm < n_lat_rows
    pos = jnp.where(is_lat, t % GRID_W, t % ctx_len)
    last = jnp.where(is_lat, GRID_W - 1, ctx_len - 1)
    prev = jnp.where(pos != 0, pltpu.roll(yv, 1, axis=0), 0.0)
    nxt = jnp.where(pos != last, pltpu.roll(yv, tm - 1, axis=0), 0.0)
    conv = cw_ref[0:1, :] * prev + cw_ref[1:2, :] * yv + cw_ref[2:3, :] * nxt
    y_ref[:, wm + gw:] = (bg_ref[...].astype(F32) * conv).astype(y_ref.dtype)


def _mix(hf, hb, z_a, z_b, mg, ws, bst, gg, cw, n_rows, n_lat_rows, ctx_len):
    wm = mg.shape[1]
    gw = gg.shape[1]
    cwid = cw.shape[1]
    tm = ROW_TILE
    assert tm % GRID_W == 0 and tm % ctx_len == 0 and tm % GMLP_CHUNK == 0
    assert gw == cwid
    full = lambda i: (0, 0)
    return pl.pallas_call(
        functools.partial(_mix_kernel, n_lat_rows=n_lat_rows, ctx_len=ctx_len),
        grid=(n_rows // tm,),
        in_specs=[
            pl.BlockSpec((tm, wm), lambda i: (i, 0)),
            pl.BlockSpec((tm, wm), lambda i: (i, 0)),
            pl.BlockSpec((tm, wm), lambda i: (i, 2)),
            pl.BlockSpec((tm, gw), lambda i: (i, 0)),
            pl.BlockSpec((tm, gw), lambda i: (i, 1)),
            pl.BlockSpec((tm, gw), lambda i: (i, 2)),
            pl.BlockSpec((tm, gw), lambda i: (i, 3)),
            pl.BlockSpec((tm, gw), lambda i: (i, 4)),
            pl.BlockSpec(mg.shape, full),
            pl.BlockSpec(ws.shape, lambda i: (0, 0, 0)),
            pl.BlockSpec(bst.shape, full),
            pl.BlockSpec(gg.shape, full),
            pl.BlockSpec(cw.shape, full),
        ],
        out_specs=pl.BlockSpec((tm, wm + gw + cwid), lambda i: (i, 0)),
        out_shape=jax.ShapeDtypeStruct((n_rows, wm + gw + cwid), BF16),
        compiler_params=_params(1),
        name="mixer_epilogue",
    )(hf, hb, z_a, z_b, z_b, z_b, z_b, z_b, mg, ws, bst, gg, cw)


def _wout_kernel(y_ref, w_ref, xl_ref, xc_ref, g_ref, o_ref, wb_scr, *, n_lat_tiles):
    @pl.when(pl.program_id(1) == 0)
    def _():
        wb_scr[...] = w_ref[...].astype(BF16)

    x = jnp.where(pl.program_id(1) < n_lat_tiles, xl_ref[...], xc_ref[...])
    acc = jnp.dot(y_ref[...], wb_scr[...], preferred_element_type=F32)
    o_ref[...] = x + g_ref[...] * acc


def _wout(y, w3, layer, src, mod3, n_lat_rows, seq, n_batch):
    m, k = y.shape
    d = w3.shape[2]
    tm = _largest_divisor(m, (MM_ROW_TILE, 256))
    tn = _largest_divisor(d, (1024, 512, 256, 128))
    mrow = functools.partial(_mod_row, tile=tm, n_lat_rows=n_lat_rows, seq=seq, n_batch=n_batch)
    gate_blk0 = (2 * d) // tn
    return pl.pallas_call(
        functools.partial(_wout_kernel, n_lat_tiles=n_lat_rows // tm),
        grid=(d // tn, m // tm),
        in_specs=[
            pl.BlockSpec((tm, k), lambda j, i: (i, 0)),
            pl.BlockSpec((None, k, tn), lambda j, i: (layer, 0, j), pipeline_mode=pl.Buffered(1)),
        ] + _row_specs(src, tm, tn, n_lat_rows, lambda j, i: j) + [
            pl.BlockSpec((None, 1, tn), lambda j, i: (mrow(i), 0, gate_blk0 + j)),
        ],
        out_specs=pl.BlockSpec((tm, tn), lambda j, i: (i, j)),
        out_shape=jax.ShapeDtypeStruct((m, d), F32),
        scratch_shapes=[pltpu.VMEM((k, tn), BF16)],
        compiler_params=_params(2),
        name="out_projection",
    )(y, w3, src[0], src[1], mod3)


def _router_kernel(x_ref, g_ref, sh_ref, sc_ref, wr_ref, br_ref, h_ref, w_ref, e_ref, *, n_experts):
    h = _norm_modulate(x_ref[...], g_ref[...], sh_ref[...], sc_ref[...])
    h_ref[...] = h
    tm = h.shape[0]
    lanes = w_ref.shape[1]
    h_hi = h.astype(BF16)
    h_lo = (h - h_hi.astype(F32)).astype(BF16)
    prod = jnp.dot(jnp.concatenate([h_hi, h_lo], axis=0), wr_ref[...], preferred_element_type=F32)
    logits = (prod[:tm, :lanes] + (prod[:tm, lanes:] + prod[tm:, :lanes])) + prod[tm:, lanes:]
    scores = jax.nn.sigmoid(logits)
    sel = scores + br_ref[...]
    epg = n_experts // N_EXPERT_GROUPS
    sel_c = [sel[:, e:e + 1] for e in range(n_experts)]
    sc_c = [scores[:, e:e + 1] for e in range(n_experts)]

    best_g = None
    for gi in range(N_EXPERT_GROUPS):
        mem = sel_c[gi * epg:(gi + 1) * epg]
        gs = None
        for a in range(epg):
            for b in range(a + 1, epg):
                pair = mem[a] + mem[b]
                gs = pair if gs is None else jnp.maximum(gs, pair)
        if best_g is None:
            best_g, gidx = gs, jnp.zeros_like(gs, dtype=jnp.int32)
        else:
            upd = gs > best_g
            gidx = jnp.where(upd, gi, gidx)
            best_g = jnp.where(upd, gs, best_g)

    in_sel, in_sc = [], []
    for a in range(epg):
        vs, vr = sel_c[a], sc_c[a]
        for gi in range(1, N_EXPERT_GROUPS):
            vs = jnp.where(gidx == gi, sel_c[gi * epg + a], vs)
            vr = jnp.where(gidx == gi, sc_c[gi * epg + a], vr)
        in_sel.append(vs)
        in_sc.append(vr)

    v1, i1, s1 = in_sel[0], jnp.zeros_like(gidx), in_sc[0]
    for a in range(1, epg):
        upd = in_sel[a] > v1
        i1 = jnp.where(upd, a, i1)
        s1 = jnp.where(upd, in_sc[a], s1)
        v1 = jnp.where(upd, in_sel[a], v1)
    v2 = jnp.full_like(v1, -jnp.inf)
    i2 = jnp.zeros_like(gidx)
    s2 = jnp.zeros_like(s1)
    for a in range(epg):
        upd = jnp.logical_and(i1 != a, in_sel[a] > v2)
        i2 = jnp.where(upd, a, i2)
        s2 = jnp.where(upd, in_sc[a], s2)
        v2 = jnp.where(upd, in_sel[a], v2)
    denom = s1 + s2
    lane = lax.broadcasted_iota(jnp.int32, w_ref.shape, 1)
    w_ref[...] = jnp.where(lane == 0, s1 / denom, jnp.where(lane == 1, s2 / denom, 0.0))
    e_ref[...] = jnp.where(lane == 0, gidx * epg + i1, jnp.where(lane == 1, gidx * epg + i2, 0))


def _router(x_mix, g, mod3, wr2, br, n_experts, n_lat_rows, seq, n_batch):
    r, d = x_mix.shape
    tm = ROW_TILE
    mrow = functools.partial(_mod_row, tile=tm, n_lat_rows=n_lat_rows, seq=seq, n_batch=n_batch)
    lanes = br.shape[1]
    return pl.pallas_call(
        functools.partial(_router_kernel, n_experts=n_experts),
        grid=(r // tm,),
        in_specs=[
            pl.BlockSpec((tm, d), lambda i: (i, 0)),
            pl.BlockSpec((1, d), lambda i: (0, 0)),
            pl.BlockSpec((None, 1, d), lambda i: (mrow(i), 0, 3)),
            pl.BlockSpec((None, 1, d), lambda i: (mrow(i), 0, 4)),
            pl.BlockSpec(wr2.shape, lambda i: (0, 0)),
            pl.BlockSpec(br.shape, lambda i: (0, 0)),
        ],
        out_specs=[
            pl.BlockSpec((tm, d), lambda i: (i, 0)),
            pl.BlockSpec((tm, lanes), lambda i: (i, 0)),
            pl.BlockSpec((tm, lanes), lambda i: (i, 0)),
        ],
        out_shape=[
            jax.ShapeDtypeStruct((r, d), F32),
            jax.ShapeDtypeStruct((r, lanes), F32),
            jax.ShapeDtypeStruct((r, lanes), jnp.int32),
        ],
        compiler_params=_params(1),
        name="moe_router",
    )(x_mix, g.reshape(1, d), mod3, mod3, wr2, br)


def _dispatch_tables(e_pair, n_experts, tm):
    flat_e = e_pair.reshape(-1)
    n_pairs = flat_e.shape[0]
    n_rows = n_pairs + n_experts * tm
    onehot = (flat_e[:, None] == jnp.arange(n_experts, dtype=jnp.int32)[None, :]).astype(jnp.int32)
    rank = jnp.sum((jnp.cumsum(onehot, axis=0) - onehot) * onehot, axis=1)
    counts = jnp.sum(onehot, axis=0)
    padded = ((counts + tm - 1) // tm) * tm
    ends = jnp.cumsum(padded)
    starts = ends - padded
    dest = jnp.sum(onehot * starts[None, :], axis=1) + rank
    n_used = (ends[-1] // tm).astype(jnp.int32)
    return dict(dest=dest.astype(jnp.int32), pad_lo=(starts + counts).astype(jnp.int32),
                pad_hi=ends.astype(jnp.int32), seg_tile0=(starts // tm).astype(jnp.int32),
                seg_tiles=(padded // tm).astype(jnp.int32), n_used=n_used.reshape(1), n_rows=n_rows)


def _dispatch_kernel(dest_ref, plo_ref, phi_ref, h_ref, xs_hbm, zrow, sem, zsem, *, n_experts):
    i = pl.program_id(0)
    tm = h_ref.shape[0]

    @pl.when(i == 0)
    def _():
        zrow[...] = jnp.zeros_like(zrow)

        def zero_copy(r):
            return pltpu.make_async_copy(zrow.at[pl.ds(0, 1)], xs_hbm.at[pl.ds(r, 1)], zsem.at[0])

        def start(r, carry):
            zero_copy(r).start()
            return carry

        def wait(r, carry):
            zero_copy(r).wait()
            return carry

        def tile_copy(t):
            return pltpu.make_async_copy(zrow, xs_hbm.at[pl.ds(t * tm, tm)], zsem.at[0])

        def start_tile(t, carry):
            tile_copy(t).start()
            return carry

        def wait_tile(t, carry):
            tile_copy(t).wait()
            return carry

        n_tiles = xs_hbm.shape[0] // tm
        first_unused = phi_ref[n_experts - 1] // tm
        for e in range(n_experts):
            lax.fori_loop(plo_ref[e], phi_ref[e], start, 0)
        lax.fori_loop(first_unused, n_tiles, start_tile, 0)
        for e in range(n_experts):
            lax.fori_loop(plo_ref[e], phi_ref[e], wait, 0)
        lax.fori_loop(first_unused, n_tiles, wait_tile, 0)

    def scatter(r, carry):
        for k in range(TOP_K):
            row = dest_ref[(i * tm + r) * TOP_K + k]
            pltpu.make_async_copy(h_ref.at[pl.ds(r, 1)], xs_hbm.at[pl.ds(row, 1)], sem.at[0]).start()
        return carry

    lax.fori_loop(0, tm, scatter, 0)
    for k in range(TOP_K):
        pltpu.make_async_copy(h_ref, xs_hbm.at[pl.ds(0, tm)], sem.at[0]).wait()


def _dispatch(h2, tables, n_experts):
    r, d = h2.shape
    tm = ROW_TILE
    assert MOE_ROW_TILE % tm == 0
    return pl.pallas_call(
        functools.partial(_dispatch_kernel, n_experts=n_experts),
        grid_spec=pltpu.PrefetchScalarGridSpec(
            num_scalar_prefetch=3,
            grid=(r // tm,),
            in_specs=[pl.BlockSpec((tm, d), lambda i, de, lo, hi: (i, 0))],
            out_specs=pl.BlockSpec(memory_space=pl.ANY),
            scratch_shapes=[pltpu.VMEM((tm, d), F32), pltpu.SemaphoreType.DMA((1,)),
                            pltpu.SemaphoreType.DMA((1,))],
        ),
        out_shape=jax.ShapeDtypeStruct((tables["n_rows"], d), F32),
        compiler_params=_params(1),
        name="moe_dispatch",
    )(tables["dest"], tables["pad_lo"], tables["pad_hi"], h2)


def _segment_loop(n_tiles, load, compute, store):
    @pl.when(n_tiles > 0)
    def _():
        load(0, 0).start()

        def body(t, carry):
            slot = t % 2
            load(t, slot).wait()

            @pl.when(t + 1 < n_tiles)
            def _():
                load(t + 1, 1 - slot).start()

            @pl.when(t >= 2)
            def _():
                store(t - 2, slot).wait()

            compute(slot)
            store(t, slot).start()
            return carry

        lax.fori_loop(0, n_tiles, body, 0)

        @pl.when(n_tiles >= 2)
        def _():
            store(n_tiles - 2, n_tiles % 2).wait()

        store(n_tiles - 1, (n_tiles - 1) % 2).wait()


def _zero_tail_tiles(first_tile, n_tiles, zero_src, tile_copy):
    zero_src[...] = jnp.zeros_like(zero_src)

    def start(t, carry):
        tile_copy(t).start()
        return carry

    def wait(t, carry):
        tile_copy(t).wait()
        return carry

    lax.fori_loop(first_tile, n_tiles, start, 0)
    lax.fori_loop(first_tile, n_tiles, wait, 0)


def _moe_up_kernel(t0_ref, nt_ref, nu_ref, wg_ref, wu_ref, xs_hbm, a_hbm, wgb, wub, xbuf, obuf, xsem, osem):
    j = pl.program_id(0)
    e = pl.program_id(1)
    tm = xbuf.shape[1]
    t0 = t0_ref[e]

    def load(t, slot):
        return pltpu.make_async_copy(xs_hbm.at[pl.ds((t0 + t) * tm, tm)], xbuf.at[slot], xsem.at[slot])

    def store(t, slot):
        return pltpu.make_async_copy(obuf.at[slot], a_hbm.at[j, pl.ds((t0 + t) * tm, tm)], osem.at[slot])

    def compute(slot):
        x = xbuf[slot].astype(BF16)
        gate = jnp.dot(x, wgb[...], preferred_element_type=F32)
        up = jnp.dot(x, wub[...], preferred_element_type=F32)
        obuf[slot] = ((gate * jax.nn.sigmoid(gate)) * up).astype(obuf.dtype)

    @pl.when(nt_ref[e] > 0)
    def _():
        wgb[...] = wg_ref[...].astype(BF16)
        wub[...] = wu_ref[...].astype(BF16)

    _segment_loop(nt_ref[e], load, compute, store)

    @pl.when(e == pl.num_programs(1) - 1)
    def _():
        _zero_tail_tiles(
            nu_ref[0], a_hbm.shape[1] // tm, obuf.at[0],
            lambda t: pltpu.make_async_copy(obuf.at[0], a_hbm.at[j, pl.ds(t * tm, tm)], osem.at[0]))


def _moe_up(xs, tables, wg4, wu4, layer, tm):
    n_rows, d = xs.shape
    n_experts, f = wg4.shape[1], wg4.shape[3]
    fs = f // MOE_UP_SPLITS
    w_spec = pl.BlockSpec((None, None, d, fs), lambda j, e, t0, nt, nu: (layer, e, 0, j))
    return pl.pallas_call(
        _moe_up_kernel,
        grid_spec=pltpu.PrefetchScalarGridSpec(
            num_scalar_prefetch=3,
            grid=(MOE_UP_SPLITS, n_experts),
            in_specs=[w_spec, w_spec, pl.BlockSpec(memory_space=pl.ANY)],
            out_specs=pl.BlockSpec(memory_space=pl.ANY),
            scratch_shapes=[
                pltpu.VMEM((d, fs), BF16), pltpu.VMEM((d, fs), BF16),
                pltpu.VMEM((2, tm, d), F32), pltpu.VMEM((2, tm, fs), BF16),
                pltpu.SemaphoreType.DMA((2,)), pltpu.SemaphoreType.DMA((2,)),
            ],
        ),
        out_shape=jax.ShapeDtypeStruct((MOE_UP_SPLITS, n_rows, fs), BF16),
        compiler_params=_params(2),
        name="moe_up",
    )(tables["seg_tile0"], tables["seg_tiles"], tables["n_used"], wg4, wu4, xs)


def _moe_down_kernel(t0_ref, nt_ref, nu_ref, wd_ref, a_hbm, y_hbm, wdb, abuf, ybuf, asem, ysem):
    e = pl.program_id(0)
    n_splits, tm, fs = abuf.shape[1:]
    t0 = t0_ref[e]

    def load(t, slot):
        return pltpu.make_async_copy(a_hbm.at[:, pl.ds((t0 + t) * tm, tm)], abuf.at[slot], asem.at[slot])

    def store(t, slot):
        return pltpu.make_async_copy(ybuf.at[slot], y_hbm.at[pl.ds((t0 + t) * tm, tm)], ysem.at[slot])

    def compute(slot):
        acc = jnp.dot(abuf[slot, 0], wdb[0:fs, :], preferred_element_type=F32)
        for s in range(1, n_splits):
            acc = acc + jnp.dot(abuf[slot, s], wdb[s * fs:(s + 1) * fs, :], preferred_element_type=F32)
        ybuf[slot] = acc

    @pl.when(nt_ref[e] > 0)
    def _():
        wdb[...] = wd_ref[...].astype(BF16)

    _segment_loop(nt_ref[e], load, compute, store)

    @pl.when(e == pl.num_programs(0) - 1)
    def _():
        _zero_tail_tiles(
            nu_ref[0], y_hbm.shape[0] // tm, ybuf.at[0],
            lambda t: pltpu.make_async_copy(ybuf.at[0], y_hbm.at[pl.ds(t * tm, tm)], ysem.at[0]))


def _moe_down(a, tables, wd4, layer, tm):
    n_splits, n_rows, fs = a.shape
    n_experts, f, d = wd4.shape[1:]
    return pl.pallas_call(
        _moe_down_kernel,
        grid_spec=pltpu.PrefetchScalarGridSpec(
            num_scalar_prefetch=3,
            grid=(n_experts,),
            in_specs=[
                pl.BlockSpec((None, None, f, d), lambda e, t0, nt, nu: (layer, e, 0, 0)),
                pl.BlockSpec(memory_space=pl.ANY),
            ],
            out_specs=pl.BlockSpec(memory_space=pl.ANY),
            scratch_shapes=[
                pltpu.VMEM((f, d), BF16),
                pltpu.VMEM((2, n_splits, tm, fs), BF16), pltpu.VMEM((2, tm, d), F32),
                pltpu.SemaphoreType.DMA((2,)), pltpu.SemaphoreType.DMA((2,)),
            ],
        ),
        out_shape=jax.ShapeDtypeStruct((n_rows, d), F32),
        compiler_params=_params(1),
        name="moe_down",
    )(tables["seg_tile0"], tables["seg_tiles"], tables["n_used"], wd4, a)


def _combine_kernel(pos_ref, x_ref, w_ref, g_ref, fg_ref, y_hbm, o_ref, ybuf, sem, *, final_norm):
    i = pl.program_id(0)
    n = pl.num_programs(0)
    tm = x_ref.shape[0]

    def gather(tile, slot):
        def body(r, carry):
            for k in range(TOP_K):
                row = pos_ref[(tile * tm + r) * TOP_K + k]
                pltpu.make_async_copy(y_hbm.at[pl.ds(row, 1)], ybuf.at[slot, k, pl.ds(r, 1)], sem.at[slot]).start()
            return carry
        lax.fori_loop(0, tm, body, 0)

    @pl.when(i == 0)
    def _():
        gather(0, 0)

    @pl.when(i + 1 < n)
    def _():
        gather(i + 1, (i + 1) % 2)

    slot = i % 2
    for k in range(TOP_K):
        pltpu.make_async_copy(y_hbm.at[pl.ds(0, tm)], ybuf.at[slot, k], sem.at[slot]).wait()
    w = w_ref[...]
    moe = w[:, 0:1] * ybuf[slot, 0] + w[:, 1:2] * ybuf[slot, 1]
    out = x_ref[...] + g_ref[...] * moe
    if final_norm:
        out = (out * lax.rsqrt(jnp.mean(out * out, axis=-1, keepdims=True) + EPS)) * fg_ref[...]
    o_ref[...] = out


def _combine(x_mix, route_w, pos, y, mod3, final_g, n_rows, n_lat_rows, seq, n_batch, final_norm):
    d = x_mix.shape[1]
    tm = ROW_TILE
    lanes = route_w.shape[1]
    mrow = functools.partial(_mod_row, tile=tm, n_lat_rows=n_lat_rows, seq=seq, n_batch=n_batch)
    return pl.pallas_call(
        functools.partial(_combine_kernel, final_norm=final_norm),
        grid_spec=pltpu.PrefetchScalarGridSpec(
            num_scalar_prefetch=1,
            grid=(n_rows // tm,),
            in_specs=[
                pl.BlockSpec((tm, d), lambda i, p: (i, 0)),
                pl.BlockSpec((tm, lanes), lambda i, p: (i, 0)),
                pl.BlockSpec((None, 1, d), lambda i, p: (mrow(i), 0, 5)),
                pl.BlockSpec((1, d), lambda i, p: (0, 0)),
                pl.BlockSpec(memory_space=pl.ANY),
            ],
            out_specs=pl.BlockSpec((tm, d), lambda i, p: (i, 0)),
            scratch_shapes=[pltpu.VMEM((2, TOP_K, tm, d), F32), pltpu.SemaphoreType.DMA((2,))],
        ),
        out_shape=jax.ShapeDtypeStruct((n_rows, d), F32),
        compiler_params=_params(1),
        name="moe_combine",
    )(pos, x_mix, route_w, mod3, final_g.reshape(1, d), y)


def kernel(x, c, ctx, c_ctx, w_mod, b_mod, norm1_g, norm2_g, w_in, b_gates, mlstm_norm_g, gmlp_ws, gmlp_bs,
           gmlp_norm_g, conv_w, w_out, w_router, b_router, w_gate_e, w_up_e, w_down_e, final_g):
    n_batch, seq, d = x.shape
    ctx_len = ctx.shape[1]
    depth = w_mod.shape[0]
    wm = mlstm_norm_g.shape[1]
    n_experts = w_router.shape[1]
    n_lat_rows = n_batch * seq
    n_all_rows = n_lat_rows + n_batch * ctx_len
    n_gate_cols = N_DIRS * 2 * MLSTM_HEADS
    gate_col0 = 3 * wm
    assert seq % ROW_TILE == 0 and seq % MM_ROW_TILE == 0 and (n_batch * ctx_len) % ROW_TILE == 0
    assert n_batch + 1 <= V7X_SUBLANES and n_experts <= V7X_LANES and n_gate_cols <= V7X_LANES

    src = (x.reshape(n_lat_rows, d), ctx.reshape(n_batch * ctx_len, d), 0)
    n_in_cols = w_in.shape[2]
    w_in_t = jnp.swapaxes(w_in, 1, 2).reshape(depth * n_in_cols, d)

    cond = jnp.zeros((V7X_SUBLANES, d), F32).at[:n_batch].set(c).at[n_batch].set(c_ctx)
    mods = _mod_table(cond, w_mod, b_mod)

    wr = jnp.zeros((d, V7X_LANES), F32).at[:, :n_experts].set(w_router)
    wr_hi = wr.astype(BF16)
    wr2 = jnp.concatenate([wr_hi, (wr - wr_hi.astype(F32)).astype(BF16)], axis=1)
    br = jnp.zeros((1, V7X_LANES), F32).at[0, :n_experts].set(b_router)

    for layer in range(depth):
        last = layer + 1 == depth
        mod3 = mods[layer].reshape(V7X_SUBLANES, 1, N_MOD * d)
        w_row0 = layer * n_in_cols
        w_gate_t = jnp.zeros((V7X_LANES, d), F32).at[:n_gate_cols].set(
            w_in_t[w_row0 + gate_col0:w_row0 + gate_col0 + n_gate_cols])
        bias = jnp.zeros((1, V7X_LANES), F32).at[0, :n_gate_cols].set(b_gates[layer].reshape(-1))
        bias_t = jnp.broadcast_to(b_gates[layer].reshape(-1, 1), (n_gate_cols, MLSTM_CHUNK))

        h1 = _normmod(src, n_all_rows, norm1_g[layer], mod3, 0, n_lat_rows, seq, n_batch)
        z_a = _matmul_wt(h1, w_in_t, w_row0, gate_col0, BF16)
        z_b = _matmul_wt(h1, w_in_t, w_row0 + gate_col0 + n_gate_cols, n_in_cols - gate_col0 - n_gate_cols, BF16)
        gates, gates_t = _gates(h1, w_gate_t.T, w_gate_t)
        hf, hb = _mlstm(z_a, gates, gates_t, bias, bias_t, n_batch, seq, ctx_len, wm)

        n_rows = n_lat_rows if last else n_all_rows
        y = _mix(hf, hb, z_a, z_b, mlstm_norm_g[layer].reshape(1, wm), gmlp_ws[layer].astype(BF16),
                 gmlp_bs[layer].T, gmlp_norm_g[layer].reshape(1, -1), conv_w[layer], n_rows, n_lat_rows, ctx_len)
        x_mix = _wout(y, w_out, layer, src, mod3, n_lat_rows, seq, n_batch)

        h2, route_w, route_e = _router(x_mix, norm2_g[layer], mod3, wr2, br, n_experts, n_lat_rows, seq, n_batch)
        tables = _dispatch_tables(route_e[:, :TOP_K], n_experts, MOE_ROW_TILE)
        xs = _dispatch(h2, tables, n_experts)
        act = _moe_up(xs, tables, w_gate_e, w_up_e, layer, MOE_ROW_TILE)
        y_moe = _moe_down(act, tables, w_down_e, layer, MOE_ROW_TILE)
        rows = _combine(x_mix, route_w, tables["dest"], y_moe, mod3, final_g, n_rows, n_lat_rows, seq, n_batch, last)
        src = (rows, rows, n_lat_rows)

    return rows.reshape(n_batch, seq, d)
```

```python
import functools

import jax
import jax.numpy as jnp
from jax import lax
from jax.experimental import pallas as pl
from jax.experimental.pallas import tpu as pltpu

F32 = jnp.float32
BF16 = jnp.bfloat16

N_MOD = 6
EPS = 1e-6
N_DIRS = 2
MLSTM_HEADS = 4
MLSTM_CHUNK = 128
GMLP_HEADS = 4
GMLP_CHUNK = 128
GRID_W = 64
N_EXPERT_GROUPS = 4
TOP_K = 2

V7X_LANES = 128
V7X_SUBLANES = 8
V7X_VMEM_BYTES = 64 * 1024 * 1024
VMEM_LIMIT = V7X_VMEM_BYTES - 8 * 1024 * 1024

ROW_TILE = 256
MM_ROW_TILE = 512
MOE_ROW_TILE = 256
MOE_UP_SPLITS = 2


def _params(n_axes):
    return pltpu.CompilerParams(
        dimension_semantics=("arbitrary",) * n_axes, vmem_limit_bytes=VMEM_LIMIT)


def _largest_divisor(n, candidates):
    for c in candidates:
        if n % c == 0:
            return c
    raise ValueError(f"no tile in {candidates} divides {n}")


def _mod_row(i, tile, n_lat_rows, seq, n_batch):
    return jnp.where(i * tile < n_lat_rows, (i * tile) // seq, n_batch)


def _mod_kernel(a_ref, w_ref, b_ref, o_ref):
    a = a_ref[...]
    a = a * jax.nn.sigmoid(a)
    acc = jnp.dot(a.astype(BF16), w_ref[...].astype(BF16), preferred_element_type=F32)
    o_ref[...] = acc + b_ref[...]


def _mod_table(a, w_mod, b_mod):
    depth, d, n = w_mod.shape
    tn = _largest_divisor(n, (1024, 512, 256, 128))
    rows = a.shape[0]
    return pl.pallas_call(
        _mod_kernel,
        grid=(depth, n // tn),
        in_specs=[
            pl.BlockSpec((rows, d), lambda l, j: (0, 0)),
            pl.BlockSpec((None, d, tn), lambda l, j: (l, 0, j)),
            pl.BlockSpec((None, 1, tn), lambda l, j: (l, 0, j)),
        ],
        out_specs=pl.BlockSpec((None, rows, tn), lambda l, j: (l, 0, j)),
        out_shape=jax.ShapeDtypeStruct((depth, rows, n), F32),
        compiler_params=_params(2),
        name="mod_table",
    )(a, w_mod, b_mod.reshape(depth, 1, n))


def _norm_modulate(x, g, shift, scale):
    y = x * lax.rsqrt(jnp.mean(x * x, axis=-1, keepdims=True) + EPS)
    return (y * g) * (1.0 + scale) + shift


def _pack_halves(x):
    half = x.shape[1] // 2
    return pltpu.pack_elementwise([x[:, :half], x[:, half:]], packed_dtype=BF16)


def _unpack_halves(p):
    return tuple(pltpu.unpack_elementwise(p, index=k, packed_dtype=BF16, unpacked_dtype=F32) for k in range(2))


def _row_specs(src, tm, tn, n_lat_rows, col):
    lat, ctx, ctx_row0 = src
    n_lat_tiles = n_lat_rows // tm
    assert n_lat_rows % tm == 0 and ctx_row0 % tm == 0
    return [
        pl.BlockSpec((tm, tn), lambda *g: (jnp.minimum(g[-1], n_lat_tiles - 1), col(*g))),
        pl.BlockSpec((tm, tn), lambda *g: (ctx_row0 // tm + jnp.maximum(g[-1] - n_lat_tiles, 0), col(*g))),
    ]


def _normmod_kernel(xl_ref, xc_ref, g_ref, sh_ref, sc_ref, o_ref, *, n_lat_tiles):
    x = jnp.where(pl.program_id(0) < n_lat_tiles, xl_ref[...], xc_ref[...])
    o_ref[...] = _norm_modulate(x, g_ref[...], sh_ref[...], sc_ref[...]).astype(o_ref.dtype)


def _normmod(src, n_rows, g, mod3, shift_col, n_lat_rows, seq, n_batch):
    d = g.shape[0]
    mrow = functools.partial(_mod_row, tile=ROW_TILE, n_lat_rows=n_lat_rows, seq=seq, n_batch=n_batch)
    return pl.pallas_call(
        functools.partial(_normmod_kernel, n_lat_tiles=n_lat_rows // ROW_TILE),
        grid=(n_rows // ROW_TILE,),
        in_specs=_row_specs(src, ROW_TILE, d, n_lat_rows, lambda i: 0) + [
            pl.BlockSpec((1, d), lambda i: (0, 0)),
            pl.BlockSpec((None, 1, d), lambda i: (mrow(i), 0, shift_col)),
            pl.BlockSpec((None, 1, d), lambda i: (mrow(i), 0, shift_col + 1)),
        ],
        out_specs=pl.BlockSpec((ROW_TILE, d), lambda i: (i, 0)),
        out_shape=jax.ShapeDtypeStruct((n_rows, d), BF16),
        compiler_params=_params(1),
        name="norm_modulate",
    )(src[0], src[1], g.reshape(1, d), mod3, mod3)


def _mm_kernel(a_ref, b_ref, o_ref):
    o_ref[...] = jnp.dot(a_ref[...], b_ref[...], preferred_element_type=F32).astype(o_ref.dtype)


def _matmul(a, b, out_dtype):
    m, k = a.shape
    n = b.shape[1]
    tm = _largest_divisor(m, (MM_ROW_TILE, 256, 128))
    tn = _largest_divisor(n, (1024, 512, 256, 128))
    return pl.pallas_call(
        _mm_kernel,
        grid=(n // tn, m // tm),
        in_specs=[
            pl.BlockSpec((tm, k), lambda j, i: (i, 0)),
            pl.BlockSpec((k, tn), lambda j, i: (0, j)),
        ],
        out_specs=pl.BlockSpec((tm, tn), lambda j, i: (i, j)),
        out_shape=jax.ShapeDtypeStruct((m, n), out_dtype),
        compiler_params=_params(2),
        name="projection",
    )(a, b)


def _mm_wt_kernel(a_ref, w_ref, o_ref, wb_scr):
    @pl.when(pl.program_id(1) == 0)
    def _():
        wb_scr[...] = w_ref[...].astype(BF16)

    acc = lax.dot_general(a_ref[...], wb_scr[...], (((1,), (1,)), ((), ())), preferred_element_type=F32)
    o_ref[...] = acc.astype(o_ref.dtype)


def _matmul_wt(a, wt, row0, n_cols, out_dtype):
    m, k = a.shape
    tm = _largest_divisor(m, (MM_ROW_TILE, 256, 128))
    tn = _largest_divisor(n_cols, (1024, 512, 256, 128))
    return pl.pallas_call(
        _mm_wt_kernel,
        grid=(n_cols // tn, m // tm),
        in_specs=[
            pl.BlockSpec((tm, k), lambda j, i: (i, 0)),
            pl.BlockSpec((pl.Element(tn), pl.Element(k)),
                         lambda j, i: (pl.multiple_of(row0 + j * tn, V7X_SUBLANES), 0)),
        ],
        out_specs=pl.BlockSpec((tm, tn), lambda j, i: (i, j)),
        out_shape=jax.ShapeDtypeStruct((m, n_cols), out_dtype),
        scratch_shapes=[pltpu.VMEM((tn, k), BF16)],
        compiler_params=_params(2),
        name="projection_wt",
    )(a, wt)


def _gates_kernel(a_ref, w_ref, wt_ref, g_ref, gt_ref):
    a = a_ref[...]
    g_ref[...] = jnp.dot(a, w_ref[...].astype(BF16), preferred_element_type=F32)
    gt_ref[...] = lax.dot_general(wt_ref[...].astype(BF16), a, (((1,), (1,)), ((), ())),
                                  preferred_element_type=F32)


def _gates(a, w_gate, w_gate_t):
    m, k = a.shape
    lanes = w_gate.shape[1]
    tm = _largest_divisor(m, (MM_ROW_TILE, 256, 128))
    return pl.pallas_call(
        _gates_kernel,
        grid=(m // tm,),
        in_specs=[
            pl.BlockSpec((tm, k), lambda i: (i, 0)),
            pl.BlockSpec((k, lanes), lambda i: (0, 0)),
            pl.BlockSpec((lanes, k), lambda i: (0, 0)),
        ],
        out_specs=[
            pl.BlockSpec((tm, lanes), lambda i: (i, 0)),
            pl.BlockSpec((lanes, tm), lambda i: (0, i)),
        ],
        out_shape=[jax.ShapeDtypeStruct((m, lanes), F32), jax.ShapeDtypeStruct((lanes, m), F32)],
        compiler_params=_params(1),
        name="gate_projection",
    )(a, w_gate, w_gate_t)


def _log_sigmoid(x):
    return jnp.minimum(x, 0.0) - jnp.log1p(jnp.exp(-jnp.abs(x)))


def _mlstm_chain(q, k, v, i_col, f_col, i_row, f_row, before, after, c_ref, m_ref, o_ref, cols):
    chunk, dk = q.shape
    dv = v.shape[1]
    lf_col = _log_sigmoid(f_col)
    lf_row = _log_sigmoid(f_row)
    b_col = jnp.sum(jnp.where(before, lf_row, 0.0), axis=1, keepdims=True)
    b_row = jnp.sum(jnp.where(after, lf_col, 0.0), axis=0, keepdims=True)
    b_last = jnp.sum(lf_row, axis=1, keepdims=True)

    m_prev = m_ref[0:1, 0:1]
    dmat = jnp.where(before, b_col - b_row + i_row, -jnp.inf)
    m_inter = b_col + m_prev
    m_t = jnp.maximum(jnp.max(dmat, axis=1, keepdims=True), m_inter)
    scale = dk ** -0.5
    qk = lax.dot_general(q, k, (((1,), (1,)), ((), ())), preferred_element_type=F32)
    s = (qk * (scale * jnp.exp(dmat - m_t))).astype(BF16)
    v_aug = jnp.concatenate([v, jnp.ones((chunk, V7X_LANES), BF16)], axis=1)
    c_prev = c_ref[...]
    inter = scale * jnp.exp(m_inter - m_t)
    num = (jnp.dot(s, v_aug, preferred_element_type=F32)
           + inter * jnp.dot(q, c_prev.astype(BF16), preferred_element_type=F32))
    den = num[:, dv:dv + 1]
    o_ref[:, cols] = (num[:, :dv] / jnp.maximum(jnp.abs(den), jnp.exp(-m_t))).astype(o_ref.dtype)

    g_end = b_last - b_col + i_col
    m_new = jnp.maximum(b_last + m_prev, jnp.max(g_end, axis=0, keepdims=True))
    decay = jnp.exp(b_last + m_prev - m_new)
    wv = (jnp.exp(g_end - m_new) * v_aug.astype(F32)).astype(BF16)
    c_ref[...] = decay * c_prev + lax.dot_general(k, wv, (((0,), (0,)), ((), ())), preferred_element_type=F32)
    m_ref[...] = jnp.broadcast_to(m_new, m_ref.shape)


def _mlstm_kernel(qf_ref, kf_ref, vf_ref, gf_ref, gtf_ref, qb_ref, kb_ref, vb_ref, gb_ref, gtb_ref,
                  b_ref, bt_ref, of_ref, ob_ref, c_scr, m_scr):
    nh = MLSTM_HEADS
    chunk = qf_ref.shape[0]
    dk = qf_ref.shape[1] // nh
    dv = vf_ref.shape[1] // nh

    @pl.when(pl.program_id(1) == 0)
    def _():
        c_scr[...] = jnp.zeros_like(c_scr)
        m_scr[...] = jnp.zeros_like(m_scr)

    t_idx = lax.broadcasted_iota(jnp.int32, (chunk, chunk), 0)
    s_idx = lax.broadcasted_iota(jnp.int32, (chunk, chunk), 1)
    dirs = ((qf_ref, kf_ref, vf_ref, gf_ref, gtf_ref, of_ref), (qb_ref, kb_ref, vb_ref, gb_ref, gtb_ref, ob_ref))
    for d, (q_ref, k_ref, v_ref, g_ref, gt_ref, o_ref) in enumerate(dirs):
        before = s_idx <= t_idx if d == 0 else s_idx >= t_idx
        after = s_idx >= t_idx if d == 0 else s_idx <= t_idx
        g = g_ref[...] + b_ref[...]
        gt = gt_ref[...] + bt_ref[...]
        for h in range(nh):
            ci = d * 2 * nh + h
            cf = ci + nh
            _mlstm_chain(
                q_ref[:, h * dk:(h + 1) * dk], k_ref[:, h * dk:(h + 1) * dk], v_ref[:, h * dv:(h + 1) * dv],
                g[:, ci:ci + 1], g[:, cf:cf + 1], gt[ci:ci + 1, :], gt[cf:cf + 1, :], before, after,
                c_scr.at[d * nh + h], m_scr.at[d * nh + h], o_ref, slice(h * dv, (h + 1) * dv))


def _mlstm(z, gates, gates_t, bias, bias_t, n_batch, seq, ctx_len, width):
    r = z.shape[0]
    nh = MLSTM_HEADS
    dv = width // nh
    dk = dv // 2
    qk_w = nh * dk
    chunk = MLSTM_CHUNK
    n_ctx = ctx_len // chunk
    n_lat = seq // chunk
    n_gate_rows = bias_t.shape[0]

    def tile(d, b, step):
        lat = step - n_ctx
        ctx_tile = n_batch * n_lat + b * n_ctx + (step if d == 0 else n_ctx - 1 - step)
        lat_tile = b * n_lat + (lat if d == 0 else n_lat - 1 - lat)
        return jnp.where(step < n_ctx, ctx_tile, lat_tile)

    def dir_specs(d):
        return [
            pl.BlockSpec((chunk, qk_w), lambda b, s: (tile(d, b, s), 0)),
            pl.BlockSpec((chunk, qk_w), lambda b, s: (tile(d, b, s), 1)),
            pl.BlockSpec((chunk, width), lambda b, s: (tile(d, b, s), 1)),
            pl.BlockSpec((chunk, gates.shape[1]), lambda b, s: (tile(d, b, s), 0)),
            pl.BlockSpec((n_gate_rows, chunk), lambda b, s: (0, tile(d, b, s))),
        ]

    return pl.pallas_call(
        _mlstm_kernel,
        grid=(n_batch, n_ctx + n_lat),
        in_specs=dir_specs(0) + dir_specs(1) + [
            pl.BlockSpec(bias.shape, lambda b, s: (0, 0)),
            pl.BlockSpec(bias_t.shape, lambda b, s: (0, 0)),
        ],
        out_specs=[
            pl.BlockSpec((chunk, width), lambda b, s: (tile(0, b, s), 0)),
            pl.BlockSpec((chunk, width), lambda b, s: (tile(1, b, s), 0)),
        ],
        out_shape=[jax.ShapeDtypeStruct((r, width), BF16), jax.ShapeDtypeStruct((r, width), BF16)],
        scratch_shapes=[
            pltpu.VMEM((N_DIRS * nh, dk, dv + V7X_LANES), F32),
            pltpu.VMEM((N_DIRS * nh, V7X_SUBLANES, V7X_LANES), F32),
        ],
        compiler_params=_params(2),
        name="mlstm_scan",
    )(z, z, z, gates, gates_t, z, z, z, gates, gates_t, bias, bias_t)


def _mix_kernel(hf_ref, hb_ref, o_ref, u_ref, v_ref, bg_ref, cg_ref, xin_ref, mg_ref, ws_ref, bst_ref, gg_ref,
                cw_ref, y_ref, *, n_lat_rows, ctx_len):
    i = pl.program_id(0)
    tm = y_ref.shape[0]
    wm = o_ref.shape[1]
    gw = u_ref.shape[1]
    dv = wm // MLSTM_HEADS
    hdim = gw // GMLP_HEADS

    for hd in range(MLSTM_HEADS):
        cols = slice(hd * dv, (hd + 1) * dv)
        hh = hf_ref[:, cols].astype(F32) + hb_ref[:, cols].astype(F32)
        hh = hh * lax.rsqrt(jnp.mean(hh * hh, axis=-1, keepdims=True) + EPS)
        out = (hh * mg_ref[:, cols]) * jax.nn.sigmoid(o_ref[:, cols].astype(F32))
        y_ref[:, cols] = out.astype(y_ref.dtype)

    u = jax.nn.gelu(u_ref[...].astype(F32))
    v = jax.nn.gelu(v_ref[...].astype(F32))
    v = (v * lax.rsqrt(jnp.mean(v * v, axis=-1, keepdims=True) + EPS)) * gg_ref[...]
    vb = v.astype(BF16)
    for c in range(tm // GMLP_CHUNK):
        rows = slice(c * GMLP_CHUNK, (c + 1) * GMLP_CHUNK)
        for hd in range(GMLP_HEADS):
            cols = slice(hd * hdim, (hd + 1) * hdim)
            sp = jnp.dot(ws_ref[hd], vb[rows, cols], preferred_element_type=F32) + bst_ref[:, hd:hd + 1]
            y_ref[rows, wm + hd * hdim:wm + (hd + 1) * hdim] = (u[rows, cols] * sp).astype(y_ref.dtype)

    yv = cg_ref[...].astype(F32) * xin_ref[...].astype(F32)
    t = lax.broadcasted_iota(jnp.int32, (tm, 1), 0)
    is_lat = i * tm < n_lat_rows
    pos = jnp.where(is_lat, t % GRID_W, t % ctx_len)
    last = jnp.where(is_lat, GRID_W - 1, ctx_len - 1)
    prev = jnp.where(pos != 0, pltpu.roll(yv, 1, axis=0), 0.0)
    nxt = jnp.where(pos != last, pltpu.roll(yv, tm - 1, axis=0), 0.0)
    conv = cw_ref[0:1, :] * prev + cw_ref[1:2, :] * yv + cw_ref[2:3, :] * nxt
    y_ref[:, wm + gw:] = (bg_ref[...].astype(F32) * conv).astype(y_ref.dtype)


def _mix(hf, hb, z_a, z_b, mg, ws, bst, gg, cw, n_rows, n_lat_rows, ctx_len):
    wm = mg.shape[1]
    gw = gg.shape[1]
    cwid = cw.shape[1]
    tm = ROW_TILE
    assert tm % GRID_W == 0 and tm % ctx_len == 0 and tm % GMLP_CHUNK == 0
    assert gw == cwid
    full = lambda i: (0, 0)
    return pl.pallas_call(
        functools.partial(_mix_kernel, n_lat_rows=n_lat_rows, ctx_len=ctx_len),
        grid=(n_rows // tm,),
        in_specs=[
            pl.BlockSpec((tm, wm), lambda i: (i, 0)),
            pl.BlockSpec((tm, wm), lambda i: (i, 0)),
            pl.BlockSpec((tm, wm), lambda i: (i, 2)),
            pl.BlockSpec((tm, gw), lambda i: (i, 0)),
            pl.BlockSpec((tm, gw), lambda i: (i, 1)),
            pl.BlockSpec((tm, gw), lambda i: (i, 2)),
            pl.BlockSpec((tm, gw), lambda i: (i, 3)),
            pl.BlockSpec((tm, gw), lambda i: (i, 4)),
            pl.BlockSpec(mg.shape, full),
            pl.BlockSpec(ws.shape, lambda i: (0, 0, 0)),
            pl.BlockSpec(bst.shape, full),
            pl.BlockSpec(gg.shape, full),
            pl.BlockSpec(cw.shape, full),
        ],
        out_specs=pl.BlockSpec((tm, wm + gw + cwid), lambda i: (i, 0)),
        out_shape=jax.ShapeDtypeStruct((n_rows, wm + gw + cwid), BF16),
        compiler_params=_params(1),
        name="mixer_epilogue",
    )(hf, hb, z_a, z_b, z_b, z_b, z_b, z_b, mg, ws, bst, gg, cw)


def _wout_kernel(y_ref, w_ref, xl_ref, xc_ref, g_ref, o_ref, wb_scr, *, n_lat_tiles):
    @pl.when(pl.program_id(1) == 0)
    def _():
        wb_scr[...] = w_ref[...].astype(BF16)

    x = jnp.where(pl.program_id(1) < n_lat_tiles, xl_ref[...], xc_ref[...])
    acc = jnp.dot(y_ref[...], wb_scr[...], preferred_element_type=F32)
    o_ref[...] = x + g_ref[...] * acc


def _wout(y, w3, layer, src, mod3, n_lat_rows, seq, n_batch):
    m, k = y.shape
    d = w3.shape[2]
    tm = _largest_divisor(m, (MM_ROW_TILE, 256))
    tn = _largest_divisor(d, (1024, 512, 256, 128))
    mrow = functools.partial(_mod_row, tile=tm, n_lat_rows=n_lat_rows, seq=seq, n_batch=n_batch)
    gate_blk0 = (2 * d) // tn
    return pl.pallas_call(
        functools.partial(_wout_kernel, n_lat_tiles=n_lat_rows // tm),
        grid=(d // tn, m // tm),
        in_specs=[
            pl.BlockSpec((tm, k), lambda j, i: (i, 0)),
            pl.BlockSpec((None, k, tn), lambda j, i: (layer, 0, j), pipeline_mode=pl.Buffered(1)),
        ] + _row_specs(src, tm, tn, n_lat_rows, lambda j, i: j) + [
            pl.BlockSpec((None, 1, tn), lambda j, i: (mrow(i), 0, gate_blk0 + j)),
        ],
        out_specs=pl.BlockSpec((tm, tn), lambda j, i: (i, j)),
        out_shape=jax.ShapeDtypeStruct((m, d), F32),
        scratch_shapes=[pltpu.VMEM((k, tn), BF16)],
        compiler_params=_params(2),
        name="out_projection",
    )(y, w3, src[0], src[1], mod3)


def _router_kernel(x_ref, g_ref, sh_ref, sc_ref, wr_ref, br_ref, h_ref, w_ref, e_ref, *, n_experts):
    h = _norm_modulate(x_ref[...], g_ref[...], sh_ref[...], sc_ref[...])
    h_ref[...] = _pack_halves(h)
    tm = h.shape[0]
    lanes = w_ref.shape[1]
    h_hi = h.astype(BF16)
    h_lo = (h - h_hi.astype(F32)).astype(BF16)
    prod = jnp.dot(jnp.concatenate([h_hi, h_lo], axis=0), wr_ref[...], preferred_element_type=F32)
    logits = (prod[:tm, :lanes] + (prod[:tm, lanes:] + prod[tm:, :lanes])) + prod[tm:, lanes:]
    scores = jax.nn.sigmoid(logits)
    sel = scores + br_ref[...]
    epg = n_experts // N_EXPERT_GROUPS
    sel_c = [sel[:, e:e + 1] for e in range(n_experts)]
    sc_c = [scores[:, e:e + 1] for e in range(n_experts)]

    best_g = None
    for gi in range(N_EXPERT_GROUPS):
        mem = sel_c[gi * epg:(gi + 1) * epg]
        gs = None
        for a in range(epg):
            for b in range(a + 1, epg):
                pair = mem[a] + mem[b]
                gs = pair if gs is None else jnp.maximum(gs, pair)
        if best_g is None:
            best_g, gidx = gs, jnp.zeros_like(gs, dtype=jnp.int32)
        else:
            upd = gs > best_g
            gidx = jnp.where(upd, gi, gidx)
            best_g = jnp.where(upd, gs, best_g)

    in_sel, in_sc = [], []
    for a in range(epg):
        vs, vr = sel_c[a], sc_c[a]
        for gi in range(1, N_EXPERT_GROUPS):
            vs = jnp.where(gidx == gi, sel_c[gi * epg + a], vs)
            vr = jnp.where(gidx == gi, sc_c[gi * epg + a], vr)
        in_sel.append(vs)
        in_sc.append(vr)

    v1, i1, s1 = in_sel[0], jnp.zeros_like(gidx), in_sc[0]
    for a in range(1, epg):
        upd = in_sel[a] > v1
        i1 = jnp.where(upd, a, i1)
        s1 = jnp.where(upd, in_sc[a], s1)
        v1 = jnp.where(upd, in_sel[a], v1)
    v2 = jnp.full_like(v1, -jnp.inf)
    i2 = jnp.zeros_like(gidx)
    s2 = jnp.zeros_like(s1)
    for a in range(epg):
        upd = jnp.logical_and(i1 != a, in_sel[a] > v2)
        i2 = jnp.where(upd, a, i2)
        s2 = jnp.where(upd, in_sc[a], s2)
        v2 = jnp.where(upd, in_sel[a], v2)
    denom = s1 + s2
    lane = lax.broadcasted_iota(jnp.int32, w_ref.shape, 1)
    w_ref[...] = jnp.where(lane == 0, s1 / denom, jnp.where(lane == 1, s2 / denom, 0.0))
    e_ref[...] = jnp.where(lane == 0, gidx * epg + i1, jnp.where(lane == 1, gidx * epg + i2, 0))


def _router(x_mix, g, mod3, wr2, br, n_experts, n_lat_rows, seq, n_batch):
    r, d = x_mix.shape
    tm = ROW_TILE
    mrow = functools.partial(_mod_row, tile=tm, n_lat_rows=n_lat_rows, seq=seq, n_batch=n_batch)
    lanes = br.shape[1]
    return pl.pallas_call(
        functools.partial(_router_kernel, n_experts=n_experts),
        grid=(r // tm,),
        in_specs=[
            pl.BlockSpec((tm, d), lambda i: (i, 0)),
            pl.BlockSpec((1, d), lambda i: (0, 0)),
            pl.BlockSpec((None, 1, d), lambda i: (mrow(i), 0, 3)),
            pl.BlockSpec((None, 1, d), lambda i: (mrow(i), 0, 4)),
            pl.BlockSpec(wr2.shape, lambda i: (0, 0)),
            pl.BlockSpec(br.shape, lambda i: (0, 0)),
        ],
        out_specs=[
            pl.BlockSpec((tm, d // 2), lambda i: (i, 0)),
            pl.BlockSpec((tm, lanes), lambda i: (i, 0)),
            pl.BlockSpec((tm, lanes), lambda i: (i, 0)),
        ],
        out_shape=[
            jax.ShapeDtypeStruct((r, d // 2), jnp.uint32),
            jax.ShapeDtypeStruct((r, lanes), F32),
            jax.ShapeDtypeStruct((r, lanes), jnp.int32),
        ],
        compiler_params=_params(1),
        name="moe_router",
    )(x_mix, g.reshape(1, d), mod3, mod3, wr2, br)


def _dispatch_tables(e_pair, n_experts, tm):
    flat_e = e_pair.reshape(-1)
    n_pairs = flat_e.shape[0]
    n_rows = n_pairs + n_experts * tm
    onehot = (flat_e[:, None] == jnp.arange(n_experts, dtype=jnp.int32)[None, :]).astype(jnp.int32)
    rank = jnp.sum((jnp.cumsum(onehot, axis=0) - onehot) * onehot, axis=1)
    counts = jnp.sum(onehot, axis=0)
    padded = ((counts + tm - 1) // tm) * tm
    ends = jnp.cumsum(padded)
    starts = ends - padded
    dest = jnp.sum(onehot * starts[None, :], axis=1) + rank
    n_used = (ends[-1] // tm).astype(jnp.int32)
    return dict(dest=dest.astype(jnp.int32), pad_lo=(starts + counts).astype(jnp.int32),
                pad_hi=ends.astype(jnp.int32), seg_tile0=(starts // tm).astype(jnp.int32),
                seg_tiles=(padded // tm).astype(jnp.int32), n_used=n_used.reshape(1), n_rows=n_rows)


def _dispatch_kernel(dest_ref, plo_ref, phi_ref, h_ref, xs_hbm, zrow, sem, zsem, *, n_experts):
    i = pl.program_id(0)
    tm = h_ref.shape[0]

    @pl.when(i == 0)
    def _():
        zrow[...] = jnp.zeros_like(zrow)

        def zero_copy(r):
            return pltpu.make_async_copy(zrow.at[pl.ds(0, 1)], xs_hbm.at[pl.ds(r, 1)], zsem.at[0])

        def start(r, carry):
            zero_copy(r).start()
            return carry

        def wait(r, carry):
            zero_copy(r).wait()
            return carry

        def tile_copy(t):
            return pltpu.make_async_copy(zrow, xs_hbm.at[pl.ds(t * tm, tm)], zsem.at[0])

        def start_tile(t, carry):
            tile_copy(t).start()
            return carry

        def wait_tile(t, carry):
            tile_copy(t).wait()
            return carry

        n_tiles = xs_hbm.shape[0] // tm
        first_unused = phi_ref[n_experts - 1] // tm
        for e in range(n_experts):
            lax.fori_loop(plo_ref[e], phi_ref[e], start, 0)
        lax.fori_loop(first_unused, n_tiles, start_tile, 0)
        for e in range(n_experts):
            lax.fori_loop(plo_ref[e], phi_ref[e], wait, 0)
        lax.fori_loop(first_unused, n_tiles, wait_tile, 0)

    def scatter(r, carry):
        for k in range(TOP_K):
            row = dest_ref[(i * tm + r) * TOP_K + k]
            pltpu.make_async_copy(h_ref.at[pl.ds(r, 1)], xs_hbm.at[pl.ds(row, 1)], sem.at[0]).start()
        return carry

    lax.fori_loop(0, tm, scatter, 0)
    for k in range(TOP_K):
        pltpu.make_async_copy(h_ref, xs_hbm.at[pl.ds(0, tm)], sem.at[0]).wait()


def _dispatch(h2, tables, n_experts):
    r, d = h2.shape
    tm = ROW_TILE
    assert MOE_ROW_TILE % tm == 0
    return pl.pallas_call(
        functools.partial(_dispatch_kernel, n_experts=n_experts),
        grid_spec=pltpu.PrefetchScalarGridSpec(
            num_scalar_prefetch=3,
            grid=(r // tm,),
            in_specs=[pl.BlockSpec((tm, d), lambda i, de, lo, hi: (i, 0))],
            out_specs=pl.BlockSpec(memory_space=pl.ANY),
            scratch_shapes=[pltpu.VMEM((tm, d), h2.dtype), pltpu.SemaphoreType.DMA((1,)),
                            pltpu.SemaphoreType.DMA((1,))],
        ),
        out_shape=jax.ShapeDtypeStruct((tables["n_rows"], d), h2.dtype),
        compiler_params=_params(1),
        name="moe_dispatch",
    )(tables["dest"], tables["pad_lo"], tables["pad_hi"], h2)


def _segment_loop(n_tiles, load, compute, store):
    @pl.when(n_tiles > 0)
    def _():
        load(0, 0).start()

        def body(t, carry):
            slot = t % 2
            load(t, slot).wait()

            @pl.when(t + 1 < n_tiles)
            def _():
                load(t + 1, 1 - slot).start()

            @pl.when(t >= 2)
            def _():
                store(t - 2, slot).wait()

            compute(slot)
            store(t, slot).start()
            return carry

        lax.fori_loop(0, n_tiles, body, 0)

        @pl.when(n_tiles >= 2)
        def _():
            store(n_tiles - 2, n_tiles % 2).wait()

        store(n_tiles - 1, (n_tiles - 1) % 2).wait()


def _zero_tail_tiles(first_tile, n_tiles, zero_src, tile_copy):
    zero_src[...] = jnp.zeros_like(zero_src)

    def start(t, carry):
        tile_copy(t).start()
        return carry

    def wait(t, carry):
        tile_copy(t).wait()
        return carry

    lax.fori_loop(first_tile, n_tiles, start, 0)
    lax.fori_loop(first_tile, n_tiles, wait, 0)


def _moe_up_kernel(t0_ref, nt_ref, nu_ref, wg_ref, wu_ref, xs_hbm, a_hbm, wgb, wub, xbuf, obuf, xsem, osem):
    j = pl.program_id(0)
    e = pl.program_id(1)
    tm = xbuf.shape[1]
    t0 = t0_ref[e]

    def load(t, slot):
        return pltpu.make_async_copy(xs_hbm.at[pl.ds((t0 + t) * tm, tm)], xbuf.at[slot], xsem.at[slot])

    def store(t, slot):
        return pltpu.make_async_copy(obuf.at[slot], a_hbm.at[j, pl.ds((t0 + t) * tm, tm)], osem.at[slot])

    def compute(slot):
        half = xbuf.shape[2]
        x_lo, x_hi = (h.astype(BF16) for h in _unpack_halves(xbuf[slot]))

        def project(w):
            return (jnp.dot(x_lo, w[0:half, :], preferred_element_type=F32)
                    + jnp.dot(x_hi, w[half:2 * half, :], preferred_element_type=F32))

        gate = project(wgb)
        up = project(wub)
        obuf[slot] = ((gate * jax.nn.sigmoid(gate)) * up).astype(obuf.dtype)

    @pl.when(nt_ref[e] > 0)
    def _():
        wgb[...] = wg_ref[...].astype(BF16)
        wub[...] = wu_ref[...].astype(BF16)

    _segment_loop(nt_ref[e], load, compute, store)

    @pl.when(e == pl.num_programs(1) - 1)
    def _():
        _zero_tail_tiles(
            nu_ref[0], a_hbm.shape[1] // tm, obuf.at[0],
            lambda t: pltpu.make_async_copy(obuf.at[0], a_hbm.at[j, pl.ds(t * tm, tm)], osem.at[0]))


def _moe_up(xs, tables, wg4, wu4, layer, tm):
    n_rows = xs.shape[0]
    n_experts, d, f = wg4.shape[1:]
    fs = f // MOE_UP_SPLITS
    w_spec = pl.BlockSpec((None, None, d, fs), lambda j, e, t0, nt, nu: (layer, e, 0, j))
    return pl.pallas_call(
        _moe_up_kernel,
        grid_spec=pltpu.PrefetchScalarGridSpec(
            num_scalar_prefetch=3,
            grid=(MOE_UP_SPLITS, n_experts),
            in_specs=[w_spec, w_spec, pl.BlockSpec(memory_space=pl.ANY)],
            out_specs=pl.BlockSpec(memory_space=pl.ANY),
            scratch_shapes=[
                pltpu.VMEM((d, fs), BF16), pltpu.VMEM((d, fs), BF16),
                pltpu.VMEM((2, tm, xs.shape[1]), xs.dtype), pltpu.VMEM((2, tm, fs), BF16),
                pltpu.SemaphoreType.DMA((2,)), pltpu.SemaphoreType.DMA((2,)),
            ],
        ),
        out_shape=jax.ShapeDtypeStruct((MOE_UP_SPLITS, n_rows, fs), BF16),
        compiler_params=_params(2),
        name="moe_up",
    )(tables["seg_tile0"], tables["seg_tiles"], tables["n_used"], wg4, wu4, xs)


def _moe_down_kernel(t0_ref, nt_ref, nu_ref, wd_ref, a_hbm, y_hbm, wdb, abuf, ybuf, asem, ysem):
    e = pl.program_id(0)
    n_splits, tm, fs = abuf.shape[1:]
    t0 = t0_ref[e]

    def load(t, slot):
        return pltpu.make_async_copy(a_hbm.at[:, pl.ds((t0 + t) * tm, tm)], abuf.at[slot], asem.at[slot])

    def store(t, slot):
        return pltpu.make_async_copy(ybuf.at[slot], y_hbm.at[pl.ds((t0 + t) * tm, tm)], ysem.at[slot])

    def compute(slot):
        acc = jnp.dot(abuf[slot, 0], wdb[0:fs, :], preferred_element_type=F32)
        for s in range(1, n_splits):
            acc = acc + jnp.dot(abuf[slot, s], wdb[s * fs:(s + 1) * fs, :], preferred_element_type=F32)
        ybuf[slot] = _pack_halves(acc)

    @pl.when(nt_ref[e] > 0)
    def _():
        wdb[...] = wd_ref[...].astype(BF16)

    _segment_loop(nt_ref[e], load, compute, store)

    @pl.when(e == pl.num_programs(0) - 1)
    def _():
        _zero_tail_tiles(
            nu_ref[0], y_hbm.shape[0] // tm, ybuf.at[0],
            lambda t: pltpu.make_async_copy(ybuf.at[0], y_hbm.at[pl.ds(t * tm, tm)], ysem.at[0]))


def _moe_down(a, tables, wd4, layer, tm):
    n_splits, n_rows, fs = a.shape
    n_experts, f, d = wd4.shape[1:]
    return pl.pallas_call(
        _moe_down_kernel,
        grid_spec=pltpu.PrefetchScalarGridSpec(
            num_scalar_prefetch=3,
            grid=(n_experts,),
            in_specs=[
                pl.BlockSpec((None, None, f, d), lambda e, t0, nt, nu: (layer, e, 0, 0)),
                pl.BlockSpec(memory_space=pl.ANY),
            ],
            out_specs=pl.BlockSpec(memory_space=pl.ANY),
            scratch_shapes=[
                pltpu.VMEM((f, d), BF16),
                pltpu.VMEM((2, n_splits, tm, fs), BF16), pltpu.VMEM((2, tm, d // 2), jnp.uint32),
                pltpu.SemaphoreType.DMA((2,)), pltpu.SemaphoreType.DMA((2,)),
            ],
        ),
        out_shape=jax.ShapeDtypeStruct((n_rows, d // 2), jnp.uint32),
        compiler_params=_params(1),
        name="moe_down",
    )(tables["seg_tile0"], tables["seg_tiles"], tables["n_used"], wd4, a)


def _combine_kernel(pos_ref, x_ref, w_ref, g_ref, fg_ref, y_hbm, o_ref, ybuf, sem, *, final_norm):
    i = pl.program_id(0)
    n = pl.num_programs(0)
    tm = x_ref.shape[0]

    def gather(tile, slot):
        def body(r, carry):
            for k in range(TOP_K):
                row = pos_ref[(tile * tm + r) * TOP_K + k]
                pltpu.make_async_copy(y_hbm.at[pl.ds(row, 1)], ybuf.at[slot, k, pl.ds(r, 1)], sem.at[slot]).start()
            return carry
        lax.fori_loop(0, tm, body, 0)

    @pl.when(i == 0)
    def _():
        gather(0, 0)

    @pl.when(i + 1 < n)
    def _():
        gather(i + 1, (i + 1) % 2)

    slot = i % 2
    for k in range(TOP_K):
        pltpu.make_async_copy(y_hbm.at[pl.ds(0, tm)], ybuf.at[slot, k], sem.at[slot]).wait()
    w = w_ref[...]
    y_first = jnp.concatenate(_unpack_halves(ybuf[slot, 0]), axis=1)
    y_second = jnp.concatenate(_unpack_halves(ybuf[slot, 1]), axis=1)
    moe = w[:, 0:1] * y_first + w[:, 1:2] * y_second
    out = x_ref[...] + g_ref[...] * moe
    if final_norm:
        out = (out * lax.rsqrt(jnp.mean(out * out, axis=-1, keepdims=True) + EPS)) * fg_ref[...]
    o_ref[...] = out


def _combine(x_mix, route_w, pos, y, mod3, final_g, n_rows, n_lat_rows, seq, n_batch, final_norm):
    d = x_mix.shape[1]
    tm = ROW_TILE
    lanes = route_w.shape[1]
    mrow = functools.partial(_mod_row, tile=tm, n_lat_rows=n_lat_rows, seq=seq, n_batch=n_batch)
    return pl.pallas_call(
        functools.partial(_combine_kernel, final_norm=final_norm),
        grid_spec=pltpu.PrefetchScalarGridSpec(
            num_scalar_prefetch=1,
            grid=(n_rows // tm,),
            in_specs=[
                pl.BlockSpec((tm, d), lambda i, p: (i, 0)),
                pl.BlockSpec((tm, lanes), lambda i, p: (i, 0)),
                pl.BlockSpec((None, 1, d), lambda i, p: (mrow(i), 0, 5)),
                pl.BlockSpec((1, d), lambda i, p: (0, 0)),
                pl.BlockSpec(memory_space=pl.ANY),
            ],
            out_specs=pl.BlockSpec((tm, d), lambda i, p: (i, 0)),
            scratch_shapes=[pltpu.VMEM((2, TOP_K, tm, y.shape[1]), y.dtype), pltpu.SemaphoreType.DMA((2,))],
        ),
        out_shape=jax.ShapeDtypeStruct((n_rows, d), F32),
        compiler_params=_params(1),
        name="moe_combine",
    )(pos, x_mix, route_w, mod3, final_g.reshape(1, d), y)


def kernel(x, c, ctx, c_ctx, w_mod, b_mod, norm1_g, norm2_g, w_in, b_gates, mlstm_norm_g, gmlp_ws, gmlp_bs,
           gmlp_norm_g, conv_w, w_out, w_router, b_router, w_gate_e, w_up_e, w_down_e, final_g):
    n_batch, seq, d = x.shape
    ctx_len = ctx.shape[1]
    depth = w_mod.shape[0]
    wm = mlstm_norm_g.shape[1]
    n_experts = w_router.shape[1]
    n_lat_rows = n_batch * seq
    n_all_rows = n_lat_rows + n_batch * ctx_len
    n_gate_cols = N_DIRS * 2 * MLSTM_HEADS
    gate_col0 = 3 * wm
    assert seq % ROW_TILE == 0 and seq % MM_ROW_TILE == 0 and (n_batch * ctx_len) % ROW_TILE == 0
    assert n_batch + 1 <= V7X_SUBLANES and n_experts <= V7X_LANES and n_gate_cols <= V7X_LANES

    src = (x.reshape(n_lat_rows, d), ctx.reshape(n_batch * ctx_len, d), 0)
    n_in_cols = w_in.shape[2]
    w_in_t = jnp.swapaxes(w_in, 1, 2).reshape(depth * n_in_cols, d)

    cond = jnp.zeros((V7X_SUBLANES, d), F32).at[:n_batch].set(c).at[n_batch].set(c_ctx)
    mods = _mod_table(cond, w_mod, b_mod)

    wr = jnp.zeros((d, V7X_LANES), F32).at[:, :n_experts].set(w_router)
    wr_hi = wr.astype(BF16)
    wr2 = jnp.concatenate([wr_hi, (wr - wr_hi.astype(F32)).astype(BF16)], axis=1)
    br = jnp.zeros((1, V7X_LANES), F32).at[0, :n_experts].set(b_router)

    for layer in range(depth):
        last = layer + 1 == depth
        mod3 = mods[layer].reshape(V7X_SUBLANES, 1, N_MOD * d)
        w_row0 = layer * n_in_cols
        w_gate_t = jnp.zeros((V7X_LANES, d), F32).at[:n_gate_cols].set(
            w_in_t[w_row0 + gate_col0:w_row0 + gate_col0 + n_gate_cols])
        bias = jnp.zeros((1, V7X_LANES), F32).at[0, :n_gate_cols].set(b_gates[layer].reshape(-1))
        bias_t = jnp.broadcast_to(b_gates[layer].reshape(-1, 1), (n_gate_cols, MLSTM_CHUNK))

        h1 = _normmod(src, n_all_rows, norm1_g[layer], mod3, 0, n_lat_rows, seq, n_batch)
        z_a = _matmul_wt(h1, w_in_t, w_row0, gate_col0, BF16)
        z_b = _matmul_wt(h1, w_in_t, w_row0 + gate_col0 + n_gate_cols, n_in_cols - gate_col0 - n_gate_cols, BF16)
        gates, gates_t = _gates(h1, w_gate_t.T, w_gate_t)
        hf, hb = _mlstm(z_a, gates, gates_t, bias, bias_t, n_batch, seq, ctx_len, wm)

        n_rows = n_lat_rows if last else n_all_rows
        y = _mix(hf, hb, z_a, z_b, mlstm_norm_g[layer].reshape(1, wm), gmlp_ws[layer].astype(BF16),
                 gmlp_bs[layer].T, gmlp_norm_g[layer].reshape(1, -1), conv_w[layer], n_rows, n_lat_rows, ctx_len)
        x_mix = _wout(y, w_out, layer, src, mod3, n_lat_rows, seq, n_batch)

        h2, route_w, route_e = _router(x_mix, norm2_g[layer], mod3, wr2, br, n_experts, n_lat_rows, seq, n_batch)
        tables = _dispatch_tables(route_e[:, :TOP_K], n_experts, MOE_ROW_TILE)
        xs = _dispatch(h2, tables, n_experts)
        act = _moe_up(xs, tables, w_gate_e, w_up_e, layer, MOE_ROW_TILE)
        y_moe = _moe_down(act, tables, w_down_e, layer, MOE_ROW_TILE)
        rows = _combine(x_mix, route_w, tables["dest"], y_moe, mod3, final_g, n_rows, n_lat_rows, seq, n_batch, last)
        src = (rows, rows, n_lat_rows)

    return rows.reshape(n_batch, seq, d)
```

```python
import functools

import jax
import jax.numpy as jnp
from jax import lax
from jax.experimental import pallas as pl
from jax.experimental.pallas import tpu as pltpu

F32 = jnp.float32
BF16 = jnp.bfloat16

N_MOD = 6
EPS = 1e-6
N_DIRS = 2
MLSTM_HEADS = 4
MLSTM_CHUNK = 128
GMLP_HEADS = 4
GMLP_CHUNK = 128
GRID_W = 64
N_EXPERT_GROUPS = 4
TOP_K = 2

V7X_LANES = 128
V7X_SUBLANES = 8
V7X_VMEM_BYTES = 64 * 1024 * 1024
VMEM_LIMIT = V7X_VMEM_BYTES - 8 * 1024 * 1024

ROW_TILE = 256
MM_ROW_TILE = 512
MOE_ROW_TILE = 256
MOE_UP_SPLITS = 2
TILE_DMA_PRIORITY = 1


def _params(n_axes):
    return pltpu.CompilerParams(
        dimension_semantics=("arbitrary",) * n_axes, vmem_limit_bytes=VMEM_LIMIT)


def _largest_divisor(n, candidates):
    for c in candidates:
        if n % c == 0:
            return c
    raise ValueError(f"no tile in {candidates} divides {n}")


def _mod_row(i, tile, n_lat_rows, seq, n_batch):
    return jnp.where(i * tile < n_lat_rows, (i * tile) // seq, n_batch)


def _mod_kernel(a_ref, w_ref, b_ref, o_ref):
    a = a_ref[...]
    a = a * jax.nn.sigmoid(a)
    acc = jnp.dot(a.astype(BF16), w_ref[...].astype(BF16), preferred_element_type=F32)
    o_ref[...] = acc + b_ref[...]


def _mod_table(a, w_mod, b_mod):
    depth, d, n = w_mod.shape
    tn = _largest_divisor(n, (1024, 512, 256, 128))
    rows = a.shape[0]
    return pl.pallas_call(
        _mod_kernel,
        grid=(depth, n // tn),
        in_specs=[
            pl.BlockSpec((rows, d), lambda l, j: (0, 0)),
            pl.BlockSpec((None, d, tn), lambda l, j: (l, 0, j)),
            pl.BlockSpec((None, 1, tn), lambda l, j: (l, 0, j)),
        ],
        out_specs=pl.BlockSpec((None, rows, tn), lambda l, j: (l, 0, j)),
        out_shape=jax.ShapeDtypeStruct((depth, rows, n), F32),
        compiler_params=_params(2),
        name="mod_table",
    )(a, w_mod, b_mod.reshape(depth, 1, n))


def _norm_modulate(x, g, shift, scale):
    y = x * lax.rsqrt(jnp.mean(x * x, axis=-1, keepdims=True) + EPS)
    return (y * g) * (1.0 + scale) + shift


def _pack_halves(x):
    half = x.shape[1] // 2
    return pltpu.pack_elementwise([x[:, :half], x[:, half:]], packed_dtype=BF16)


def _unpack_halves(p):
    return tuple(pltpu.unpack_elementwise(p, index=k, packed_dtype=BF16, unpacked_dtype=F32) for k in range(2))


def _zeros_like_ref(ref):
    if ref.dtype == jnp.uint32:
        return _pack_halves(jnp.zeros(ref.shape[:-1] + (2 * ref.shape[-1],), F32))
    return jnp.zeros(ref.shape, ref.dtype)


def _row_specs(src, tm, tn, n_lat_rows, col):
    lat, ctx, ctx_row0 = src
    n_lat_tiles = n_lat_rows // tm
    assert n_lat_rows % tm == 0 and ctx_row0 % tm == 0
    return [
        pl.BlockSpec((tm, tn), lambda *g: (jnp.minimum(g[-1], n_lat_tiles - 1), col(*g))),
        pl.BlockSpec((tm, tn), lambda *g: (ctx_row0 // tm + jnp.maximum(g[-1] - n_lat_tiles, 0), col(*g))),
    ]


def _normmod_kernel(xl_ref, xc_ref, g_ref, sh_ref, sc_ref, o_ref, *, n_lat_tiles):
    x = jnp.where(pl.program_id(0) < n_lat_tiles, xl_ref[...], xc_ref[...])
    o_ref[...] = _norm_modulate(x, g_ref[...], sh_ref[...], sc_ref[...]).astype(o_ref.dtype)


def _normmod(src, n_rows, g, mod3, shift_col, n_lat_rows, seq, n_batch):
    d = g.shape[0]
    mrow = functools.partial(_mod_row, tile=ROW_TILE, n_lat_rows=n_lat_rows, seq=seq, n_batch=n_batch)
    return pl.pallas_call(
        functools.partial(_normmod_kernel, n_lat_tiles=n_lat_rows // ROW_TILE),
        grid=(n_rows // ROW_TILE,),
        in_specs=_row_specs(src, ROW_TILE, d, n_lat_rows, lambda i: 0) + [
            pl.BlockSpec((1, d), lambda i: (0, 0)),
            pl.BlockSpec((None, 1, d), lambda i: (mrow(i), 0, shift_col)),
            pl.BlockSpec((None, 1, d), lambda i: (mrow(i), 0, shift_col + 1)),
        ],
        out_specs=pl.BlockSpec((ROW_TILE, d), lambda i: (i, 0)),
        out_shape=jax.ShapeDtypeStruct((n_rows, d), BF16),
        compiler_params=_params(1),
        name="norm_modulate",
    )(src[0], src[1], g.reshape(1, d), mod3, mod3)


def _mm_kernel(a_ref, b_ref, o_ref):
    o_ref[...] = jnp.dot(a_ref[...], b_ref[...], preferred_element_type=F32).astype(o_ref.dtype)


def _matmul(a, b, out_dtype):
    m, k = a.shape
    n = b.shape[1]
    tm = _largest_divisor(m, (MM_ROW_TILE, 256, 128))
    tn = _largest_divisor(n, (1024, 512, 256, 128))
    return pl.pallas_call(
        _mm_kernel,
        grid=(n // tn, m // tm),
        in_specs=[
            pl.BlockSpec((tm, k), lambda j, i: (i, 0)),
            pl.BlockSpec((k, tn), lambda j, i: (0, j)),
        ],
        out_specs=pl.BlockSpec((tm, tn), lambda j, i: (i, j)),
        out_shape=jax.ShapeDtypeStruct((m, n), out_dtype),
        compiler_params=_params(2),
        name="projection",
    )(a, b)


def _mm_wt_kernel(a_ref, w_ref, o_ref, wb_scr):
    @pl.when(pl.program_id(1) == 0)
    def _():
        wb_scr[...] = w_ref[...].astype(BF16)

    acc = lax.dot_general(a_ref[...], wb_scr[...], (((1,), (1,)), ((), ())), preferred_element_type=F32)
    o_ref[...] = acc.astype(o_ref.dtype)


def _matmul_wt(a, wt, row0, n_cols, out_dtype):
    m, k = a.shape
    tm = _largest_divisor(m, (MM_ROW_TILE, 256, 128))
    tn = _largest_divisor(n_cols, (1024, 512, 256, 128))
    return pl.pallas_call(
        _mm_wt_kernel,
        grid=(n_cols // tn, m // tm),
        in_specs=[
            pl.BlockSpec((tm, k), lambda j, i: (i, 0)),
            pl.BlockSpec((pl.Element(tn), pl.Element(k)),
                         lambda j, i: (pl.multiple_of(row0 + j * tn, V7X_SUBLANES), 0)),
        ],
        out_specs=pl.BlockSpec((tm, tn), lambda j, i: (i, j)),
        out_shape=jax.ShapeDtypeStruct((m, n_cols), out_dtype),
        scratch_shapes=[pltpu.VMEM((tn, k), BF16)],
        compiler_params=_params(2),
        name="projection_wt",
    )(a, wt)


def _gates_kernel(a_ref, w_ref, wt_ref, g_ref, gt_ref):
    a = a_ref[...]
    g_ref[...] = jnp.dot(a, w_ref[...].astype(BF16), preferred_element_type=F32)
    gt_ref[...] = lax.dot_general(wt_ref[...].astype(BF16), a, (((1,), (1,)), ((), ())),
                                  preferred_element_type=F32)


def _gates(a, w_gate, w_gate_t):
    m, k = a.shape
    lanes = w_gate.shape[1]
    tm = _largest_divisor(m, (MM_ROW_TILE, 256, 128))
    return pl.pallas_call(
        _gates_kernel,
        grid=(m // tm,),
        in_specs=[
            pl.BlockSpec((tm, k), lambda i: (i, 0)),
            pl.BlockSpec((k, lanes), lambda i: (0, 0)),
            pl.BlockSpec((lanes, k), lambda i: (0, 0)),
        ],
        out_specs=[
            pl.BlockSpec((tm, lanes), lambda i: (i, 0)),
            pl.BlockSpec((lanes, tm), lambda i: (0, i)),
        ],
        out_shape=[jax.ShapeDtypeStruct((m, lanes), F32), jax.ShapeDtypeStruct((lanes, m), F32)],
        compiler_params=_params(1),
        name="gate_projection",
    )(a, w_gate, w_gate_t)


def _log_sigmoid(x):
    return jnp.minimum(x, 0.0) - jnp.log1p(jnp.exp(-jnp.abs(x)))


def _mlstm_chain(q, k, v, i_col, f_col, i_row, f_row, before, after, c_ref, m_ref, o_ref, cols):
    chunk, dk = q.shape
    dv = v.shape[1]
    lf_col = _log_sigmoid(f_col)
    lf_row = _log_sigmoid(f_row)
    b_col = jnp.sum(jnp.where(before, lf_row, 0.0), axis=1, keepdims=True)
    b_row = jnp.sum(jnp.where(after, lf_col, 0.0), axis=0, keepdims=True)
    b_last = jnp.sum(lf_row, axis=1, keepdims=True)

    m_prev = m_ref[0:1, 0:1]
    dmat = jnp.where(before, b_col - b_row + i_row, -jnp.inf)
    m_inter = b_col + m_prev
    m_t = jnp.maximum(jnp.max(dmat, axis=1, keepdims=True), m_inter)
    scale = dk ** -0.5
    qk = lax.dot_general(q, k, (((1,), (1,)), ((), ())), preferred_element_type=F32)
    s = (qk * (scale * jnp.exp(dmat - m_t))).astype(BF16)
    v_aug = jnp.concatenate([v, jnp.ones((chunk, V7X_LANES), BF16)], axis=1)
    c_prev = c_ref[...]
    inter = scale * jnp.exp(m_inter - m_t)
    num = (jnp.dot(s, v_aug, preferred_element_type=F32)
           + inter * jnp.dot(q, c_prev.astype(BF16), preferred_element_type=F32))
    den = num[:, dv:dv + 1]
    o_ref[:, cols] = (num[:, :dv] / jnp.maximum(jnp.abs(den), jnp.exp(-m_t))).astype(o_ref.dtype)

    g_end = b_last - b_col + i_col
    m_new = jnp.maximum(b_last + m_prev, jnp.max(g_end, axis=0, keepdims=True))
    decay = jnp.exp(b_last + m_prev - m_new)
    wv = (jnp.exp(g_end - m_new) * v_aug.astype(F32)).astype(BF16)
    c_ref[...] = decay * c_prev + lax.dot_general(k, wv, (((0,), (0,)), ((), ())), preferred_element_type=F32)
    m_ref[...] = jnp.broadcast_to(m_new, m_ref.shape)


def _mlstm_kernel(qf_ref, kf_ref, vf_ref, gf_ref, gtf_ref, qb_ref, kb_ref, vb_ref, gb_ref, gtb_ref,
                  b_ref, bt_ref, of_ref, ob_ref, c_scr, m_scr):
    nh = MLSTM_HEADS
    chunk = qf_ref.shape[0]
    dk = qf_ref.shape[1] // nh
    dv = vf_ref.shape[1] // nh

    @pl.when(pl.program_id(1) == 0)
    def _():
        c_scr[...] = jnp.zeros_like(c_scr)
        m_scr[...] = jnp.zeros_like(m_scr)

    t_idx = lax.broadcasted_iota(jnp.int32, (chunk, chunk), 0)
    s_idx = lax.broadcasted_iota(jnp.int32, (chunk, chunk), 1)
    dirs = ((qf_ref, kf_ref, vf_ref, gf_ref, gtf_ref, of_ref), (qb_ref, kb_ref, vb_ref, gb_ref, gtb_ref, ob_ref))
    for d, (q_ref, k_ref, v_ref, g_ref, gt_ref, o_ref) in enumerate(dirs):
        before = s_idx <= t_idx if d == 0 else s_idx >= t_idx
        after = s_idx >= t_idx if d == 0 else s_idx <= t_idx
        g = g_ref[...] + b_ref[...]
        gt = gt_ref[...] + bt_ref[...]
        for h in range(nh):
            ci = d * 2 * nh + h
            cf = ci + nh
            _mlstm_chain(
                q_ref[:, h * dk:(h + 1) * dk], k_ref[:, h * dk:(h + 1) * dk], v_ref[:, h * dv:(h + 1) * dv],
                g[:, ci:ci + 1], g[:, cf:cf + 1], gt[ci:ci + 1, :], gt[cf:cf + 1, :], before, after,
                c_scr.at[d * nh + h], m_scr.at[d * nh + h], o_ref, slice(h * dv, (h + 1) * dv))


def _mlstm(z, gates, gates_t, bias, bias_t, n_batch, seq, ctx_len, width):
    r = z.shape[0]
    nh = MLSTM_HEADS
    dv = width // nh
    dk = dv // 2
    qk_w = nh * dk
    chunk = MLSTM_CHUNK
    n_ctx = ctx_len // chunk
    n_lat = seq // chunk
    n_gate_rows = bias_t.shape[0]

    def tile(d, b, step):
        lat = step - n_ctx
        ctx_tile = n_batch * n_lat + b * n_ctx + (step if d == 0 else n_ctx - 1 - step)
        lat_tile = b * n_lat + (lat if d == 0 else n_lat - 1 - lat)
        return jnp.where(step < n_ctx, ctx_tile, lat_tile)

    def dir_specs(d):
        return [
            pl.BlockSpec((chunk, qk_w), lambda b, s: (tile(d, b, s), 0)),
            pl.BlockSpec((chunk, qk_w), lambda b, s: (tile(d, b, s), 1)),
            pl.BlockSpec((chunk, width), lambda b, s: (tile(d, b, s), 1)),
            pl.BlockSpec((chunk, gates.shape[1]), lambda b, s: (tile(d, b, s), 0)),
            pl.BlockSpec((n_gate_rows, chunk), lambda b, s: (0, tile(d, b, s))),
        ]

    return pl.pallas_call(
        _mlstm_kernel,
        grid=(n_batch, n_ctx + n_lat),
        in_specs=dir_specs(0) + dir_specs(1) + [
            pl.BlockSpec(bias.shape, lambda b, s: (0, 0)),
            pl.BlockSpec(bias_t.shape, lambda b, s: (0, 0)),
        ],
        out_specs=[
            pl.BlockSpec((chunk, width), lambda b, s: (tile(0, b, s), 0)),
            pl.BlockSpec((chunk, width), lambda b, s: (tile(1, b, s), 0)),
        ],
        out_shape=[jax.ShapeDtypeStruct((r, width), BF16), jax.ShapeDtypeStruct((r, width), BF16)],
        scratch_shapes=[
            pltpu.VMEM((N_DIRS * nh, dk, dv + V7X_LANES), F32),
            pltpu.VMEM((N_DIRS * nh, V7X_SUBLANES, V7X_LANES), F32),
        ],
        compiler_params=_params(2),
        name="mlstm_scan",
    )(z, z, z, gates, gates_t, z, z, z, gates, gates_t, bias, bias_t)


def _mix_kernel(hf_ref, hb_ref, o_ref, u_ref, v_ref, bg_ref, cg_ref, xin_ref, mg_ref, ws_ref, bst_ref, gg_ref,
                cw_ref, y_ref, *, n_lat_rows, ctx_len):
    i = pl.program_id(0)
    tm = y_ref.shape[0]
    wm = o_ref.shape[1]
    gw = u_ref.shape[1]
    dv = wm // MLSTM_HEADS
    hdim = gw // GMLP_HEADS

    for hd in range(MLSTM_HEADS):
        cols = slice(hd * dv, (hd + 1) * dv)
        hh = hf_ref[:, cols].astype(F32) + hb_ref[:, cols].astype(F32)
        hh = hh * lax.rsqrt(jnp.mean(hh * hh, axis=-1, keepdims=True) + EPS)
        out = (hh * mg_ref[:, cols]) * jax.nn.sigmoid(o_ref[:, cols].astype(F32))
        y_ref[:, cols] = out.astype(y_ref.dtype)

    u = jax.nn.gelu(u_ref[...].astype(F32))
    v = jax.nn.gelu(v_ref[...].astype(F32))
    v = (v * lax.rsqrt(jnp.mean(v * v, axis=-1, keepdims=True) + EPS)) * gg_ref[...]
    vb = v.astype(BF16)
    for c in range(tm // GMLP_CHUNK):
        rows = slice(c * GMLP_CHUNK, (c + 1) * GMLP_CHUNK)
        for hd in range(GMLP_HEADS):
            cols = slice(hd * hdim, (hd + 1) * hdim)
            sp = jnp.dot(ws_ref[hd], vb[rows, cols], preferred_element_type=F32) + bst_ref[:, hd:hd + 1]
            y_ref[rows, wm + hd * hdim:wm + (hd + 1) * hdim] = (u[rows, cols] * sp).astype(y_ref.dtype)

    yv = cg_ref[...].astype(F32) * xin_ref[...].astype(F32)
    t = lax.broadcasted_iota(jnp.int32, (tm, 1), 0)
    is_lat = i * tm < n_lat_rows
    pos = jnp.where(is_lat, t % GRID_W, t % ctx_len)
    last = jnp.where(is_lat, GRID_W - 1, ctx_len - 1)
    prev = jnp.where(pos != 0, pltpu.roll(yv, 1, axis=0), 0.0)
    nxt = jnp.where(pos != last, pltpu.roll(yv, tm - 1, axis=0), 0.0)
    conv = cw_ref[0:1, :] * prev + cw_ref[1:2, :] * yv + cw_ref[2:3, :] * nxt
    y_ref[:, wm + gw:] = (bg_ref[...].astype(F32) * conv).astype(y_ref.dtype)


def _mix(hf, hb, z_a, z_b, mg, ws, bst, gg, cw, n_rows, n_lat_rows, ctx_len):
    wm = mg.shape[1]
    gw = gg.shape[1]
    cwid = cw.shape[1]
    tm = ROW_TILE
    assert tm % GRID_W == 0 and tm % ctx_len == 0 and tm % GMLP_CHUNK == 0
    assert gw == cwid
    full = lambda i: (0, 0)
    return pl.pallas_call(
        functools.partial(_mix_kernel, n_lat_rows=n_lat_rows, ctx_len=ctx_len),
        grid=(n_rows // tm,),
        in_specs=[
            pl.BlockSpec((tm, wm), lambda i: (i, 0)),
            pl.BlockSpec((tm, wm), lambda i: (i, 0)),
            pl.BlockSpec((tm, wm), lambda i: (i, 2)),
            pl.BlockSpec((tm, gw), lambda i: (i, 0)),
            pl.BlockSpec((tm, gw), lambda i: (i, 1)),
            pl.BlockSpec((tm, gw), lambda i: (i, 2)),
            pl.BlockSpec((tm, gw), lambda i: (i, 3)),
            pl.BlockSpec((tm, gw), lambda i: (i, 4)),
            pl.BlockSpec(mg.shape, full),
            pl.BlockSpec(ws.shape, lambda i: (0, 0, 0)),
            pl.BlockSpec(bst.shape, full),
            pl.BlockSpec(gg.shape, full),
            pl.BlockSpec(cw.shape, full),
        ],
        out_specs=pl.BlockSpec((tm, wm + gw + cwid), lambda i: (i, 0)),
        out_shape=jax.ShapeDtypeStruct((n_rows, wm + gw + cwid), BF16),
        compiler_params=_params(1),
        name="mixer_epilogue",
    )(hf, hb, z_a, z_b, z_b, z_b, z_b, z_b, mg, ws, bst, gg, cw)


def _wout_kernel(y_ref, w_ref, xl_ref, xc_ref, g_ref, o_ref, wb_scr, *, n_lat_tiles):
    @pl.when(pl.program_id(1) == 0)
    def _():
        wb_scr[...] = w_ref[...].astype(BF16)

    x = jnp.where(pl.program_id(1) < n_lat_tiles, xl_ref[...], xc_ref[...])
    acc = jnp.dot(y_ref[...], wb_scr[...], preferred_element_type=F32)
    o_ref[...] = x + g_ref[...] * acc


def _wout(y, w3, layer, src, mod3, n_lat_rows, seq, n_batch):
    m, k = y.shape
    d = w3.shape[2]
    tm = _largest_divisor(m, (MM_ROW_TILE, 256))
    tn = _largest_divisor(d, (1024, 512, 256, 128))
    mrow = functools.partial(_mod_row, tile=tm, n_lat_rows=n_lat_rows, seq=seq, n_batch=n_batch)
    gate_blk0 = (2 * d) // tn
    return pl.pallas_call(
        functools.partial(_wout_kernel, n_lat_tiles=n_lat_rows // tm),
        grid=(d // tn, m // tm),
        in_specs=[
            pl.BlockSpec((tm, k), lambda j, i: (i, 0)),
            pl.BlockSpec((None, k, tn), lambda j, i: (layer, 0, j), pipeline_mode=pl.Buffered(1)),
        ] + _row_specs(src, tm, tn, n_lat_rows, lambda j, i: j) + [
            pl.BlockSpec((None, 1, tn), lambda j, i: (mrow(i), 0, gate_blk0 + j)),
        ],
        out_specs=pl.BlockSpec((tm, tn), lambda j, i: (i, j)),
        out_shape=jax.ShapeDtypeStruct((m, d), F32),
        scratch_shapes=[pltpu.VMEM((k, tn), BF16)],
        compiler_params=_params(2),
        name="out_projection",
    )(y, w3, src[0], src[1], mod3)


def _router_kernel(x_ref, g_ref, sh_ref, sc_ref, wr_ref, br_ref, h_ref, w_ref, e_ref, *, n_experts):
    h = _norm_modulate(x_ref[...], g_ref[...], sh_ref[...], sc_ref[...])
    h_ref[...] = _pack_halves(h)
    tm = h.shape[0]
    lanes = w_ref.shape[1]
    h_hi = h.astype(BF16)
    h_lo = (h - h_hi.astype(F32)).astype(BF16)
    prod = jnp.dot(jnp.concatenate([h_hi, h_lo], axis=0), wr_ref[...], preferred_element_type=F32)
    logits = (prod[:tm, :lanes] + (prod[:tm, lanes:] + prod[tm:, :lanes])) + prod[tm:, lanes:]
    scores = jax.nn.sigmoid(logits)
    sel = scores + br_ref[...]
    epg = n_experts // N_EXPERT_GROUPS
    sel_c = [sel[:, e:e + 1] for e in range(n_experts)]
    sc_c = [scores[:, e:e + 1] for e in range(n_experts)]

    best_g = None
    for gi in range(N_EXPERT_GROUPS):
        mem = sel_c[gi * epg:(gi + 1) * epg]
        gs = None
        for a in range(epg):
            for b in range(a + 1, epg):
                pair = mem[a] + mem[b]
                gs = pair if gs is None else jnp.maximum(gs, pair)
        if best_g is None:
            best_g, gidx = gs, jnp.zeros_like(gs, dtype=jnp.int32)
        else:
            upd = gs > best_g
            gidx = jnp.where(upd, gi, gidx)
            best_g = jnp.where(upd, gs, best_g)

    in_sel, in_sc = [], []
    for a in range(epg):
        vs, vr = sel_c[a], sc_c[a]
        for gi in range(1, N_EXPERT_GROUPS):
            vs = jnp.where(gidx == gi, sel_c[gi * epg + a], vs)
            vr = jnp.where(gidx == gi, sc_c[gi * epg + a], vr)
        in_sel.append(vs)
        in_sc.append(vr)

    v1, i1, s1 = in_sel[0], jnp.zeros_like(gidx), in_sc[0]
    for a in range(1, epg):
        upd = in_sel[a] > v1
        i1 = jnp.where(upd, a, i1)
        s1 = jnp.where(upd, in_sc[a], s1)
        v1 = jnp.where(upd, in_sel[a], v1)
    v2 = jnp.full_like(v1, -jnp.inf)
    i2 = jnp.zeros_like(gidx)
    s2 = jnp.zeros_like(s1)
    for a in range(epg):
        upd = jnp.logical_and(i1 != a, in_sel[a] > v2)
        i2 = jnp.where(upd, a, i2)
        s2 = jnp.where(upd, in_sc[a], s2)
        v2 = jnp.where(upd, in_sel[a], v2)
    denom = s1 + s2
    lane = lax.broadcasted_iota(jnp.int32, w_ref.shape, 1)
    w_ref[...] = jnp.where(lane == 0, s1 / denom, jnp.where(lane == 1, s2 / denom, 0.0))
    e_ref[...] = jnp.where(lane == 0, gidx * epg + i1, jnp.where(lane == 1, gidx * epg + i2, 0))


def _router(x_mix, g, mod3, wr2, br, n_experts, n_lat_rows, seq, n_batch):
    r, d = x_mix.shape
    tm = ROW_TILE
    mrow = functools.partial(_mod_row, tile=tm, n_lat_rows=n_lat_rows, seq=seq, n_batch=n_batch)
    lanes = br.shape[1]
    return pl.pallas_call(
        functools.partial(_router_kernel, n_experts=n_experts),
        grid=(r // tm,),
        in_specs=[
            pl.BlockSpec((tm, d), lambda i: (i, 0)),
            pl.BlockSpec((1, d), lambda i: (0, 0)),
            pl.BlockSpec((None, 1, d), lambda i: (mrow(i), 0, 3)),
            pl.BlockSpec((None, 1, d), lambda i: (mrow(i), 0, 4)),
            pl.BlockSpec(wr2.shape, lambda i: (0, 0)),
            pl.BlockSpec(br.shape, lambda i: (0, 0)),
        ],
        out_specs=[
            pl.BlockSpec((tm, d // 2), lambda i: (i, 0)),
            pl.BlockSpec((tm, lanes), lambda i: (i, 0)),
            pl.BlockSpec((tm, lanes), lambda i: (i, 0)),
        ],
        out_shape=[
            jax.ShapeDtypeStruct((r, d // 2), jnp.uint32),
            jax.ShapeDtypeStruct((r, lanes), F32),
            jax.ShapeDtypeStruct((r, lanes), jnp.int32),
        ],
        compiler_params=_params(1),
        name="moe_router",
    )(x_mix, g.reshape(1, d), mod3, mod3, wr2, br)


def _dispatch_tables(e_pair, n_experts, tm):
    flat_e = e_pair.reshape(-1)
    n_pairs = flat_e.shape[0]
    n_rows = n_pairs + n_experts * tm
    onehot = (flat_e[:, None] == jnp.arange(n_experts, dtype=jnp.int32)[None, :]).astype(jnp.int32)
    rank = jnp.sum((jnp.cumsum(onehot, axis=0) - onehot) * onehot, axis=1)
    counts = jnp.sum(onehot, axis=0)
    padded = ((counts + tm - 1) // tm) * tm
    ends = jnp.cumsum(padded)
    starts = ends - padded
    dest = jnp.sum(onehot * starts[None, :], axis=1) + rank
    n_used = (ends[-1] // tm).astype(jnp.int32)
    return dict(dest=dest.astype(jnp.int32), pad_lo=(starts + counts).astype(jnp.int32),
                pad_hi=ends.astype(jnp.int32), seg_tile0=(starts // tm).astype(jnp.int32),
                seg_tiles=(padded // tm).astype(jnp.int32), n_used=n_used.reshape(1), n_rows=n_rows)


def _dispatch_kernel(dest_ref, plo_ref, phi_ref, h_ref, xs_hbm, zrow, sem, zsem, *, n_experts):
    i = pl.program_id(0)
    tm = h_ref.shape[0]

    @pl.when(i == 0)
    def _():
        zrow[...] = _zeros_like_ref(zrow)

        def zero_copy(r):
            return pltpu.make_async_copy(zrow.at[pl.ds(0, 1)], xs_hbm.at[pl.ds(r, 1)], zsem.at[0])

        def start(r, carry):
            zero_copy(r).start()
            return carry

        def wait(r, carry):
            zero_copy(r).wait()
            return carry

        def tile_copy(t):
            return pltpu.make_async_copy(zrow, xs_hbm.at[pl.ds(t * tm, tm)], zsem.at[0])

        def start_tile(t, carry):
            tile_copy(t).start()
            return carry

        def wait_tile(t, carry):
            tile_copy(t).wait()
            return carry

        n_tiles = xs_hbm.shape[0] // tm
        first_unused = phi_ref[n_experts - 1] // tm
        for e in range(n_experts):
            lax.fori_loop(plo_ref[e], phi_ref[e], start, 0)
        lax.fori_loop(first_unused, n_tiles, start_tile, 0)
        for e in range(n_experts):
            lax.fori_loop(plo_ref[e], phi_ref[e], wait, 0)
        lax.fori_loop(first_unused, n_tiles, wait_tile, 0)

    def scatter(r, carry):
        for k in range(TOP_K):
            row = dest_ref[(i * tm + r) * TOP_K + k]
            pltpu.make_async_copy(h_ref.at[pl.ds(r, 1)], xs_hbm.at[pl.ds(row, 1)], sem.at[0]).start(priority=k)
        return carry

    lax.fori_loop(0, tm, scatter, 0)
    for k in range(TOP_K):
        pltpu.make_async_copy(h_ref, xs_hbm.at[pl.ds(0, tm)], sem.at[0]).wait()


def _dispatch(h2, tables, n_experts):
    r, d = h2.shape
    tm = ROW_TILE
    assert MOE_ROW_TILE % tm == 0
    return pl.pallas_call(
        functools.partial(_dispatch_kernel, n_experts=n_experts),
        grid_spec=pltpu.PrefetchScalarGridSpec(
            num_scalar_prefetch=3,
            grid=(r // tm,),
            in_specs=[pl.BlockSpec((tm, d), lambda i, de, lo, hi: (i, 0))],
            out_specs=pl.BlockSpec(memory_space=pl.ANY),
            scratch_shapes=[pltpu.VMEM((tm, d), h2.dtype), pltpu.SemaphoreType.DMA((1,)),
                            pltpu.SemaphoreType.DMA((1,))],
        ),
        out_shape=jax.ShapeDtypeStruct((tables["n_rows"], d), h2.dtype),
        compiler_params=_params(1),
        name="moe_dispatch",
    )(tables["dest"], tables["pad_lo"], tables["pad_hi"], h2)


def _segment_loop(n_tiles, load, compute, store):
    @pl.when(n_tiles > 0)
    def _():
        load(0, 0).start(priority=TILE_DMA_PRIORITY)

        def body(t, carry):
            slot = t % 2
            load(t, slot).wait()

            @pl.when(t + 1 < n_tiles)
            def _():
                load(t + 1, 1 - slot).start(priority=TILE_DMA_PRIORITY)

            @pl.when(t >= 2)
            def _():
                store(t - 2, slot).wait()

            compute(slot)
            store(t, slot).start(priority=TILE_DMA_PRIORITY)
            return carry

        lax.fori_loop(0, n_tiles, body, 0)

        @pl.when(n_tiles >= 2)
        def _():
            store(n_tiles - 2, n_tiles % 2).wait()

        store(n_tiles - 1, (n_tiles - 1) % 2).wait()


def _zero_tail_tiles(first_tile, n_tiles, zero_src, tile_copy):
    zero_src[...] = _zeros_like_ref(zero_src)

    def start(t, carry):
        tile_copy(t).start()
        return carry

    def wait(t, carry):
        tile_copy(t).wait()
        return carry

    lax.fori_loop(first_tile, n_tiles, start, 0)
    lax.fori_loop(first_tile, n_tiles, wait, 0)


def _moe_up_kernel(t0_ref, nt_ref, nu_ref, wg_ref, wu_ref, xs_hbm, a_hbm, wgb, wub, xbuf, obuf, xsem, osem):
    j = pl.program_id(0)
    e = pl.program_id(1)
    tm = xbuf.shape[1]
    t0 = t0_ref[e]

    def load(t, slot):
        return pltpu.make_async_copy(xs_hbm.at[pl.ds((t0 + t) * tm, tm)], xbuf.at[slot], xsem.at[slot])

    def store(t, slot):
        return pltpu.make_async_copy(obuf.at[slot], a_hbm.at[j, pl.ds((t0 + t) * tm, tm)], osem.at[slot])

    def compute(slot):
        half = xbuf.shape[2]
        x_lo, x_hi = (h.astype(BF16) for h in _unpack_halves(xbuf[slot]))

        def project(w):
            return (jnp.dot(x_lo, w[0:half, :], preferred_element_type=F32)
                    + jnp.dot(x_hi, w[half:2 * half, :], preferred_element_type=F32))

        gate = project(wgb)
        up = project(wub)
        obuf[slot] = ((gate * jax.nn.sigmoid(gate)) * up).astype(obuf.dtype)

    @pl.when(nt_ref[e] > 0)
    def _():
        wgb[...] = wg_ref[...].astype(BF16)
        wub[...] = wu_ref[...].astype(BF16)

    _segment_loop(nt_ref[e], load, compute, store)

    @pl.when(e == pl.num_programs(1) - 1)
    def _():
        _zero_tail_tiles(
            nu_ref[0], a_hbm.shape[1] // tm, obuf.at[0],
            lambda t: pltpu.make_async_copy(obuf.at[0], a_hbm.at[j, pl.ds(t * tm, tm)], osem.at[0]))


def _moe_up(xs, tables, wg4, wu4, layer, tm):
    n_rows = xs.shape[0]
    n_experts, d, f = wg4.shape[1:]
    fs = f // MOE_UP_SPLITS
    w_spec = pl.BlockSpec((None, None, d, fs), lambda j, e, t0, nt, nu: (layer, e, 0, j))
    return pl.pallas_call(
        _moe_up_kernel,
        grid_spec=pltpu.PrefetchScalarGridSpec(
            num_scalar_prefetch=3,
            grid=(MOE_UP_SPLITS, n_experts),
            in_specs=[w_spec, w_spec, pl.BlockSpec(memory_space=pl.ANY)],
            out_specs=pl.BlockSpec(memory_space=pl.ANY),
            scratch_shapes=[
                pltpu.VMEM((d, fs), BF16), pltpu.VMEM((d, fs), BF16),
                pltpu.VMEM((2, tm, xs.shape[1]), xs.dtype), pltpu.VMEM((2, tm, fs), BF16),
                pltpu.SemaphoreType.DMA((2,)), pltpu.SemaphoreType.DMA((2,)),
            ],
        ),
        out_shape=jax.ShapeDtypeStruct((MOE_UP_SPLITS, n_rows, fs), BF16),
        compiler_params=_params(2),
        name="moe_up",
    )(tables["seg_tile0"], tables["seg_tiles"], tables["n_used"], wg4, wu4, xs)


def _moe_down_kernel(t0_ref, nt_ref, nu_ref, wd_ref, a_hbm, y_hbm, wdb, abuf, ybuf, asem, ysem):
    e = pl.program_id(0)
    n_splits, tm, fs = abuf.shape[1:]
    t0 = t0_ref[e]

    def load(t, slot):
        return pltpu.make_async_copy(a_hbm.at[:, pl.ds((t0 + t) * tm, tm)], abuf.at[slot], asem.at[slot])

    def store(t, slot):
        return pltpu.make_async_copy(ybuf.at[slot], y_hbm.at[pl.ds((t0 + t) * tm, tm)], ysem.at[slot])

    def compute(slot):
        acc = jnp.dot(abuf[slot, 0], wdb[0:fs, :], preferred_element_type=F32)
        for s in range(1, n_splits):
            acc = acc + jnp.dot(abuf[slot, s], wdb[s * fs:(s + 1) * fs, :], preferred_element_type=F32)
        ybuf[slot] = _pack_halves(acc)

    @pl.when(nt_ref[e] > 0)
    def _():
        wdb[...] = wd_ref[...].astype(BF16)

    _segment_loop(nt_ref[e], load, compute, store)

    @pl.when(e == pl.num_programs(0) - 1)
    def _():
        _zero_tail_tiles(
            nu_ref[0], y_hbm.shape[0] // tm, ybuf.at[0],
            lambda t: pltpu.make_async_copy(ybuf.at[0], y_hbm.at[pl.ds(t * tm, tm)], ysem.at[0]))


def _moe_down(a, tables, wd4, layer, tm):
    n_splits, n_rows, fs = a.shape
    n_experts, f, d = wd4.shape[1:]
    return pl.pallas_call(
        _moe_down_kernel,
        grid_spec=pltpu.PrefetchScalarGridSpec(
            num_scalar_prefetch=3,
            grid=(n_experts,),
            in_specs=[
                pl.BlockSpec((None, None, f, d), lambda e, t0, nt, nu: (layer, e, 0, 0)),
                pl.BlockSpec(memory_space=pl.ANY),
            ],
            out_specs=pl.BlockSpec(memory_space=pl.ANY),
            scratch_shapes=[
                pltpu.VMEM((f, d), BF16),
                pltpu.VMEM((2, n_splits, tm, fs), BF16), pltpu.VMEM((2, tm, d // 2), jnp.uint32),
                pltpu.SemaphoreType.DMA((2,)), pltpu.SemaphoreType.DMA((2,)),
            ],
        ),
        out_shape=jax.ShapeDtypeStruct((n_rows, d // 2), jnp.uint32),
        compiler_params=_params(1),
        name="moe_down",
    )(tables["seg_tile0"], tables["seg_tiles"], tables["n_used"], wd4, a)


def _combine_kernel(pos_ref, x_ref, w_ref, g_ref, fg_ref, y_hbm, o_ref, ybuf, sem, *, final_norm):
    i = pl.program_id(0)
    n = pl.num_programs(0)
    tm = x_ref.shape[0]

    def gather(tile, slot):
        def body(r, carry):
            for k in range(TOP_K):
                row = pos_ref[(tile * tm + r) * TOP_K + k]
                pltpu.make_async_copy(
                    y_hbm.at[pl.ds(row, 1)], ybuf.at[slot, k, pl.ds(r, 1)], sem.at[slot]).start(priority=k)
            return carry
        lax.fori_loop(0, tm, body, 0)

    @pl.when(i == 0)
    def _():
        gather(0, 0)

    @pl.when(i + 1 < n)
    def _():
        gather(i + 1, (i + 1) % 2)

    slot = i % 2
    for k in range(TOP_K):
        pltpu.make_async_copy(y_hbm.at[pl.ds(0, tm)], ybuf.at[slot, k], sem.at[slot]).wait()
    w = w_ref[...]
    y_first = jnp.concatenate(_unpack_halves(ybuf[slot, 0]), axis=1)
    y_second = jnp.concatenate(_unpack_halves(ybuf[slot, 1]), axis=1)
    moe = w[:, 0:1] * y_first + w[:, 1:2] * y_second
    out = x_ref[...] + g_ref[...] * moe
    if final_norm:
        out = (out * lax.rsqrt(jnp.mean(out * out, axis=-1, keepdims=True) + EPS)) * fg_ref[...]
    o_ref[...] = out


def _combine(x_mix, route_w, pos, y, mod3, final_g, n_rows, n_lat_rows, seq, n_batch, final_norm):
    d = x_mix.shape[1]
    tm = ROW_TILE
    lanes = route_w.shape[1]
    mrow = functools.partial(_mod_row, tile=tm, n_lat_rows=n_lat_rows, seq=seq, n_batch=n_batch)
    return pl.pallas_call(
        functools.partial(_combine_kernel, final_norm=final_norm),
        grid_spec=pltpu.PrefetchScalarGridSpec(
            num_scalar_prefetch=1,
            grid=(n_rows // tm,),
            in_specs=[
                pl.BlockSpec((tm, d), lambda i, p: (i, 0)),
                pl.BlockSpec((tm, lanes), lambda i, p: (i, 0)),
                pl.BlockSpec((None, 1, d), lambda i, p: (mrow(i), 0, 5)),
                pl.BlockSpec((1, d), lambda i, p: (0, 0)),
                pl.BlockSpec(memory_space=pl.ANY),
            ],
            out_specs=pl.BlockSpec((tm, d), lambda i, p: (i, 0)),
            scratch_shapes=[pltpu.VMEM((2, TOP_K, tm, y.shape[1]), y.dtype), pltpu.SemaphoreType.DMA((2,))],
        ),
        out_shape=jax.ShapeDtypeStruct((n_rows, d), F32),
        compiler_params=_params(1),
        name="moe_combine",
    )(pos, x_mix, route_w, mod3, final_g.reshape(1, d), y)


def kernel(x, c, ctx, c_ctx, w_mod, b_mod, norm1_g, norm2_g, w_in, b_gates, mlstm_norm_g, gmlp_ws, gmlp_bs,
           gmlp_norm_g, conv_w, w_out, w_router, b_router, w_gate_e, w_up_e, w_down_e, final_g):
    n_batch, seq, d = x.shape
    ctx_len = ctx.shape[1]
    depth = w_mod.shape[0]
    wm = mlstm_norm_g.shape[1]
    n_experts = w_router.shape[1]
    n_lat_rows = n_batch * seq
    n_all_rows = n_lat_rows + n_batch * ctx_len
    n_gate_cols = N_DIRS * 2 * MLSTM_HEADS
    gate_col0 = 3 * wm
    assert seq % ROW_TILE == 0 and seq % MM_ROW_TILE == 0 and (n_batch * ctx_len) % ROW_TILE == 0
    assert n_batch + 1 <= V7X_SUBLANES and n_experts <= V7X_LANES and n_gate_cols <= V7X_LANES

    src = (x.reshape(n_lat_rows, d), ctx.reshape(n_batch * ctx_len, d), 0)
    n_in_cols = w_in.shape[2]
    w_in_t = jnp.swapaxes(w_in, 1, 2).reshape(depth * n_in_cols, d)

    cond = jnp.zeros((V7X_SUBLANES, d), F32).at[:n_batch].set(c).at[n_batch].set(c_ctx)
    mods = _mod_table(cond, w_mod, b_mod)

    wr = jnp.zeros((d, V7X_LANES), F32).at[:, :n_experts].set(w_router)
    wr_hi = wr.astype(BF16)
    wr2 = jnp.concatenate([wr_hi, (wr - wr_hi.astype(F32)).astype(BF16)], axis=1)
    br = jnp.zeros((1, V7X_LANES), F32).at[0, :n_experts].set(b_router)

    for layer in range(depth):
        last = layer + 1 == depth
        mod3 = mods[layer].reshape(V7X_SUBLANES, 1, N_MOD * d)
        w_row0 = layer * n_in_cols
        w_gate_t = jnp.zeros((V7X_LANES, d), F32).at[:n_gate_cols].set(
            w_in_t[w_row0 + gate_col0:w_row0 + gate_col0 + n_gate_cols])
        bias = jnp.zeros((1, V7X_LANES), F32).at[0, :n_gate_cols].set(b_gates[layer].reshape(-1))
        bias_t = jnp.broadcast_to(b_gates[layer].reshape(-1, 1), (n_gate_cols, MLSTM_CHUNK))

        h1 = _normmod(src, n_all_rows, norm1_g[layer], mod3, 0, n_lat_rows, seq, n_batch)
        z_a = _matmul_wt(h1, w_in_t, w_row0, gate_col0, BF16)
        z_b = _matmul_wt(h1, w_in_t, w_row0 + gate_col0 + n_gate_cols, n_in_cols - gate_col0 - n_gate_cols, BF16)
        gates, gates_t = _gates(h1, w_gate_t.T, w_gate_t)
        hf, hb = _mlstm(z_a, gates, gates_t, bias, bias_t, n_batch, seq, ctx_len, wm)

        n_rows = n_lat_rows if last else n_all_rows
        y = _mix(hf, hb, z_a, z_b, mlstm_norm_g[layer].reshape(1, wm), gmlp_ws[layer].astype(BF16),
                 gmlp_bs[layer].T, gmlp_norm_g[layer].reshape(1, -1), conv_w[layer], n_rows, n_lat_rows, ctx_len)
        x_mix = _wout(y, w_out, layer, src, mod3, n_lat_rows, seq, n_batch)

        h2, route_w, route_e = _router(x_mix, norm2_g[layer], mod3, wr2, br, n_experts, n_lat_rows, seq, n_batch)
        tables = _dispatch_tables(route_e[:, :TOP_K], n_experts, MOE_ROW_TILE)
        xs = _dispatch(h2, tables, n_experts)
        act = _moe_up(xs, tables, w_gate_e, w_up_e, layer, MOE_ROW_TILE)
        y_moe = _moe_down(act, tables, w_down_e, layer, MOE_ROW_TILE)
        rows = _combine(x_mix, route_w, tables["dest"], y_moe, mod3, final_g, n_rows, n_lat_rows, seq, n_batch, last)
        src = (rows, rows, n_lat_rows)

    return rows.reshape(n_batch, seq, d)
```

```python
import functools

import jax
import jax.numpy as jnp
from jax import lax
from jax.experimental import pallas as pl
from jax.experimental.pallas import tpu as pltpu

F32 = jnp.float32
BF16 = jnp.bfloat16

N_MOD = 6
EPS = 1e-6
N_DIRS = 2
MLSTM_HEADS = 4
MLSTM_CHUNK = 128
GMLP_HEADS = 4
GMLP_CHUNK = 128
GRID_W = 64
N_EXPERT_GROUPS = 4
TOP_K = 2

V7X_LANES = 128
V7X_SUBLANES = 8
V7X_VMEM_BYTES = 64 * 1024 * 1024
VMEM_LIMIT = V7X_VMEM_BYTES - 8 * 1024 * 1024

ROW_TILE = 256
MM_ROW_TILE = 512
MOE_ROW_TILE = 256
MOE_UP_SPLITS = 2
TILE_DMA_PRIORITY = 1
TILE_LOOKAHEAD = 2
N_TILE_SLOTS = TILE_LOOKAHEAD + 1


def _params(n_axes):
    return pltpu.CompilerParams(
        dimension_semantics=("arbitrary",) * n_axes, vmem_limit_bytes=VMEM_LIMIT)


def _largest_divisor(n, candidates):
    for c in candidates:
        if n % c == 0:
            return c
    raise ValueError(f"no tile in {candidates} divides {n}")


def _mod_row(i, tile, n_lat_rows, seq, n_batch):
    return jnp.where(i * tile < n_lat_rows, (i * tile) // seq, n_batch)


def _mod_kernel(a_ref, w_ref, b_ref, o_ref):
    a = a_ref[...]
    a = a * jax.nn.sigmoid(a)
    acc = jnp.dot(a.astype(BF16), w_ref[...].astype(BF16), preferred_element_type=F32)
    o_ref[...] = acc + b_ref[...]


def _mod_table(a, w_mod, b_mod):
    depth, d, n = w_mod.shape
    tn = _largest_divisor(n, (1024, 512, 256, 128))
    rows = a.shape[0]
    return pl.pallas_call(
        _mod_kernel,
        grid=(depth, n // tn),
        in_specs=[
            pl.BlockSpec((rows, d), lambda l, j: (0, 0)),
            pl.BlockSpec((None, d, tn), lambda l, j: (l, 0, j)),
            pl.BlockSpec((None, 1, tn), lambda l, j: (l, 0, j)),
        ],
        out_specs=pl.BlockSpec((None, rows, tn), lambda l, j: (l, 0, j)),
        out_shape=jax.ShapeDtypeStruct((depth, rows, n), F32),
        compiler_params=_params(2),
        name="mod_table",
    )(a, w_mod, b_mod.reshape(depth, 1, n))


def _norm_modulate(x, g, shift, scale):
    y = x * lax.rsqrt(jnp.mean(x * x, axis=-1, keepdims=True) + EPS)
    return (y * g) * (1.0 + scale) + shift


def _pack_halves(x):
    half = x.shape[1] // 2
    return pltpu.pack_elementwise([x[:, :half], x[:, half:]], packed_dtype=BF16)


def _unpack_halves(p):
    return tuple(pltpu.unpack_elementwise(p, index=k, packed_dtype=BF16, unpacked_dtype=F32) for k in range(2))


def _zeros_like_ref(ref):
    if ref.dtype == jnp.uint32:
        return _pack_halves(jnp.zeros(ref.shape[:-1] + (2 * ref.shape[-1],), F32))
    return jnp.zeros(ref.shape, ref.dtype)


def _row_specs(src, tm, tn, n_lat_rows, col):
    lat, ctx, ctx_row0 = src
    n_lat_tiles = n_lat_rows // tm
    assert n_lat_rows % tm == 0 and ctx_row0 % tm == 0
    return [
        pl.BlockSpec((tm, tn), lambda *g: (jnp.minimum(g[-1], n_lat_tiles - 1), col(*g))),
        pl.BlockSpec((tm, tn), lambda *g: (ctx_row0 // tm + jnp.maximum(g[-1] - n_lat_tiles, 0), col(*g))),
    ]


def _normmod_kernel(xl_ref, xc_ref, g_ref, sh_ref, sc_ref, o_ref, *, n_lat_tiles):
    x = jnp.where(pl.program_id(0) < n_lat_tiles, xl_ref[...], xc_ref[...])
    o_ref[...] = _norm_modulate(x, g_ref[...], sh_ref[...], sc_ref[...]).astype(o_ref.dtype)


def _normmod(src, n_rows, g, mod3, shift_col, n_lat_rows, seq, n_batch):
    d = g.shape[0]
    mrow = functools.partial(_mod_row, tile=ROW_TILE, n_lat_rows=n_lat_rows, seq=seq, n_batch=n_batch)
    return pl.pallas_call(
        functools.partial(_normmod_kernel, n_lat_tiles=n_lat_rows // ROW_TILE),
        grid=(n_rows // ROW_TILE,),
        in_specs=_row_specs(src, ROW_TILE, d, n_lat_rows, lambda i: 0) + [
            pl.BlockSpec((1, d), lambda i: (0, 0)),
            pl.BlockSpec((None, 1, d), lambda i: (mrow(i), 0, shift_col)),
            pl.BlockSpec((None, 1, d), lambda i: (mrow(i), 0, shift_col + 1)),
        ],
        out_specs=pl.BlockSpec((ROW_TILE, d), lambda i: (i, 0)),
        out_shape=jax.ShapeDtypeStruct((n_rows, d), BF16),
        compiler_params=_params(1),
        name="norm_modulate",
    )(src[0], src[1], g.reshape(1, d), mod3, mod3)


def _mm_wt_kernel(a_ref, w_ref, o_ref, wb_scr):
    @pl.when(pl.program_id(1) == 0)
    def _():
        wb_scr[...] = w_ref[...].astype(BF16)

    acc = lax.dot_general(a_ref[...], wb_scr[...], (((1,), (1,)), ((), ())), preferred_element_type=F32)
    o_ref[...] = acc.astype(o_ref.dtype)


def _matmul_wt(a, wt, row0, n_cols, out_dtype):
    m, k = a.shape
    tm = _largest_divisor(m, (MM_ROW_TILE, 256, 128))
    tn = _largest_divisor(n_cols, (1024, 512, 256, 128))
    return pl.pallas_call(
        _mm_wt_kernel,
        grid=(n_cols // tn, m // tm),
        in_specs=[
            pl.BlockSpec((tm, k), lambda j, i: (i, 0)),
            pl.BlockSpec((pl.Element(tn), pl.Element(k)),
                         lambda j, i: (pl.multiple_of(row0 + j * tn, V7X_SUBLANES), 0)),
        ],
        out_specs=pl.BlockSpec((tm, tn), lambda j, i: (i, j)),
        out_shape=jax.ShapeDtypeStruct((m, n_cols), out_dtype),
        scratch_shapes=[pltpu.VMEM((tn, k), BF16)],
        compiler_params=_params(2),
        name="projection_wt",
    )(a, wt)


def _gates_kernel(a_ref, w_ref, wt_ref, g_ref, gt_ref):
    a = a_ref[...]
    g_ref[...] = jnp.dot(a, w_ref[...].astype(BF16), preferred_element_type=F32)
    gt_ref[...] = lax.dot_general(wt_ref[...].astype(BF16), a, (((1,), (1,)), ((), ())),
                                  preferred_element_type=F32)


def _gates(a, w_gate, w_gate_t):
    m, k = a.shape
    lanes = w_gate.shape[1]
    tm = _largest_divisor(m, (MM_ROW_TILE, 256, 128))
    return pl.pallas_call(
        _gates_kernel,
        grid=(m // tm,),
        in_specs=[
            pl.BlockSpec((tm, k), lambda i: (i, 0)),
            pl.BlockSpec((k, lanes), lambda i: (0, 0)),
            pl.BlockSpec((lanes, k), lambda i: (0, 0)),
        ],
        out_specs=[
            pl.BlockSpec((tm, lanes), lambda i: (i, 0)),
            pl.BlockSpec((lanes, tm), lambda i: (0, i)),
        ],
        out_shape=[jax.ShapeDtypeStruct((m, lanes), F32), jax.ShapeDtypeStruct((lanes, m), F32)],
        compiler_params=_params(1),
        name="gate_projection",
    )(a, w_gate, w_gate_t)


def _log_sigmoid(x):
    return jnp.minimum(x, 0.0) - jnp.log1p(jnp.exp(-jnp.abs(x)))


def _mlstm_chain(q, k, v, i_col, f_col, i_row, f_row, before, after, c_ref, m_ref, o_ref, cols):
    chunk, dk = q.shape
    dv = v.shape[1]
    lf_col = _log_sigmoid(f_col)
    lf_row = _log_sigmoid(f_row)
    b_col = jnp.sum(jnp.where(before, lf_row, 0.0), axis=1, keepdims=True)
    b_row = jnp.sum(jnp.where(after, lf_col, 0.0), axis=0, keepdims=True)
    b_last = jnp.sum(lf_row, axis=1, keepdims=True)

    m_prev = m_ref[0:1, 0:1]
    dmat = jnp.where(before, b_col - b_row + i_row, -jnp.inf)
    m_inter = b_col + m_prev
    m_t = jnp.maximum(jnp.max(dmat, axis=1, keepdims=True), m_inter)
    scale = dk ** -0.5
    qk = lax.dot_general(q, k, (((1,), (1,)), ((), ())), preferred_element_type=F32)
    s = (qk * (scale * jnp.exp(dmat - m_t))).astype(BF16)
    v_aug = jnp.concatenate([v, jnp.ones((chunk, V7X_LANES), BF16)], axis=1)
    c_prev = c_ref[...]
    inter = scale * jnp.exp(m_inter - m_t)
    num = (jnp.dot(s, v_aug, preferred_element_type=F32)
           + inter * jnp.dot(q, c_prev.astype(BF16), preferred_element_type=F32))
    den = num[:, dv:dv + 1]
    o_ref[:, cols] = (num[:, :dv] / jnp.maximum(jnp.abs(den), jnp.exp(-m_t))).astype(o_ref.dtype)

    g_end = b_last - b_col + i_col
    m_new = jnp.maximum(b_last + m_prev, jnp.max(g_end, axis=0, keepdims=True))
    decay = jnp.exp(b_last + m_prev - m_new)
    wv = (jnp.exp(g_end - m_new) * v_aug.astype(F32)).astype(BF16)
    c_ref[...] = decay * c_prev + lax.dot_general(k, wv, (((0,), (0,)), ((), ())), preferred_element_type=F32)
    m_ref[...] = jnp.broadcast_to(m_new, m_ref.shape)


def _mlstm_kernel(qf_ref, kf_ref, vf_ref, gf_ref, gtf_ref, qb_ref, kb_ref, vb_ref, gb_ref, gtb_ref,
                  b_ref, bt_ref, of_ref, ob_ref, c_scr, m_scr):
    nh = MLSTM_HEADS
    chunk = qf_ref.shape[0]
    dk = qf_ref.shape[1] // nh
    dv = vf_ref.shape[1] // nh

    @pl.when(pl.program_id(1) == 0)
    def _():
        c_scr[...] = jnp.zeros_like(c_scr)
        m_scr[...] = jnp.zeros_like(m_scr)

    t_idx = lax.broadcasted_iota(jnp.int32, (chunk, chunk), 0)
    s_idx = lax.broadcasted_iota(jnp.int32, (chunk, chunk), 1)
    dirs = ((qf_ref, kf_ref, vf_ref, gf_ref, gtf_ref, of_ref), (qb_ref, kb_ref, vb_ref, gb_ref, gtb_ref, ob_ref))
    for d, (q_ref, k_ref, v_ref, g_ref, gt_ref, o_ref) in enumerate(dirs):
        before = s_idx <= t_idx if d == 0 else s_idx >= t_idx
        after = s_idx >= t_idx if d == 0 else s_idx <= t_idx
        g = g_ref[...] + b_ref[...]
        gt = gt_ref[...] + bt_ref[...]
        for h in range(nh):
            ci = d * 2 * nh + h
            cf = ci + nh
            _mlstm_chain(
                q_ref[:, h * dk:(h + 1) * dk], k_ref[:, h * dk:(h + 1) * dk], v_ref[:, h * dv:(h + 1) * dv],
                g[:, ci:ci + 1], g[:, cf:cf + 1], gt[ci:ci + 1, :], gt[cf:cf + 1, :], before, after,
                c_scr.at[d * nh + h], m_scr.at[d * nh + h], o_ref, slice(h * dv, (h + 1) * dv))


def _mlstm(z, gates, gates_t, bias, bias_t, n_batch, seq, ctx_len, width):
    r = z.shape[0]
    nh = MLSTM_HEADS
    dv = width // nh
    dk = dv // 2
    qk_w = nh * dk
    chunk = MLSTM_CHUNK
    n_ctx = ctx_len // chunk
    n_lat = seq // chunk
    n_gate_rows = bias_t.shape[0]

    def tile(d, b, step):
        lat = step - n_ctx
        ctx_tile = n_batch * n_lat + b * n_ctx + (step if d == 0 else n_ctx - 1 - step)
        lat_tile = b * n_lat + (lat if d == 0 else n_lat - 1 - lat)
        return jnp.where(step < n_ctx, ctx_tile, lat_tile)

    def dir_specs(d):
        return [
            pl.BlockSpec((chunk, qk_w), lambda b, s: (tile(d, b, s), 0)),
            pl.BlockSpec((chunk, qk_w), lambda b, s: (tile(d, b, s), 1)),
            pl.BlockSpec((chunk, width), lambda b, s: (tile(d, b, s), 1)),
            pl.BlockSpec((chunk, gates.shape[1]), lambda b, s: (tile(d, b, s), 0)),
            pl.BlockSpec((n_gate_rows, chunk), lambda b, s: (0, tile(d, b, s))),
        ]

    return pl.pallas_call(
        _mlstm_kernel,
        grid=(n_batch, n_ctx + n_lat),
        in_specs=dir_specs(0) + dir_specs(1) + [
            pl.BlockSpec(bias.shape, lambda b, s: (0, 0)),
            pl.BlockSpec(bias_t.shape, lambda b, s: (0, 0)),
        ],
        out_specs=[
            pl.BlockSpec((chunk, width), lambda b, s: (tile(0, b, s), 0)),
            pl.BlockSpec((chunk, width), lambda b, s: (tile(1, b, s), 0)),
        ],
        out_shape=[jax.ShapeDtypeStruct((r, width), BF16), jax.ShapeDtypeStruct((r, width), BF16)],
        scratch_shapes=[
            pltpu.VMEM((N_DIRS * nh, dk, dv + V7X_LANES), F32),
            pltpu.VMEM((N_DIRS * nh, V7X_SUBLANES, V7X_LANES), F32),
        ],
        compiler_params=_params(2),
        name="mlstm_scan",
    )(z, z, z, gates, gates_t, z, z, z, gates, gates_t, bias, bias_t)


def _mix_kernel(hf_ref, hb_ref, o_ref, u_ref, v_ref, bg_ref, cg_ref, xin_ref, mg_ref, ws_ref, bst_ref, gg_ref,
                cw_ref, y_ref, *, n_lat_rows, ctx_len):
    i = pl.program_id(0)
    tm = y_ref.shape[0]
    wm = o_ref.shape[1]
    gw = u_ref.shape[1]
    dv = wm // MLSTM_HEADS
    hdim = gw // GMLP_HEADS

    for hd in range(MLSTM_HEADS):
        cols = slice(hd * dv, (hd + 1) * dv)
        hh = hf_ref[:, cols].astype(F32) + hb_ref[:, cols].astype(F32)
        hh = hh * lax.rsqrt(jnp.mean(hh * hh, axis=-1, keepdims=True) + EPS)
        out = (hh * mg_ref[:, cols]) * jax.nn.sigmoid(o_ref[:, cols].astype(F32))
        y_ref[:, cols] = out.astype(y_ref.dtype)

    u = jax.nn.gelu(u_ref[...].astype(F32))
    v = jax.nn.gelu(v_ref[...].astype(F32))
    v = (v * lax.rsqrt(jnp.mean(v * v, axis=-1, keepdims=True) + EPS)) * gg_ref[...]
    vb = v.astype(BF16)
    for c in range(tm // GMLP_CHUNK):
        rows = slice(c * GMLP_CHUNK, (c + 1) * GMLP_CHUNK)
        for hd in range(GMLP_HEADS):
            cols = slice(hd * hdim, (hd + 1) * hdim)
            sp = jnp.dot(ws_ref[hd], vb[rows, cols], preferred_element_type=F32) + bst_ref[:, hd:hd + 1]
            y_ref[rows, wm + hd * hdim:wm + (hd + 1) * hdim] = (u[rows, cols] * sp).astype(y_ref.dtype)

    yv = cg_ref[...].astype(F32) * xin_ref[...].astype(F32)
    t = lax.broadcasted_iota(jnp.int32, (tm, 1), 0)
    is_lat = i * tm < n_lat_rows
    pos = jnp.where(is_lat, t % GRID_W, t % ctx_len)
    last = jnp.where(is_lat, GRID_W - 1, ctx_len - 1)
    prev = jnp.where(pos != 0, pltpu.roll(yv, 1, axis=0), 0.0)
    nxt = jnp.where(pos != last, pltpu.roll(yv, tm - 1, axis=0), 0.0)
    conv = cw_ref[0:1, :] * prev + cw_ref[1:2, :] * yv + cw_ref[2:3, :] * nxt
    y_ref[:, wm + gw:] = (bg_ref[...].astype(F32) * conv).astype(y_ref.dtype)


def _mix(hf, hb, z_a, z_b, mg, ws, bst, gg, cw, n_rows, n_lat_rows, ctx_len):
    wm = mg.shape[1]
    gw = gg.shape[1]
    cwid = cw.shape[1]
    tm = ROW_TILE
    assert tm % GRID_W == 0 and tm % ctx_len == 0 and tm % GMLP_CHUNK == 0
    assert gw == cwid
    full = lambda i: (0, 0)
    return pl.pallas_call(
        functools.partial(_mix_kernel, n_lat_rows=n_lat_rows, ctx_len=ctx_len),
        grid=(n_rows // tm,),
        in_specs=[
            pl.BlockSpec((tm, wm), lambda i: (i, 0)),
            pl.BlockSpec((tm, wm), lambda i: (i, 0)),
            pl.BlockSpec((tm, wm), lambda i: (i, 2)),
            pl.BlockSpec((tm, gw), lambda i: (i, 0)),
            pl.BlockSpec((tm, gw), lambda i: (i, 1)),
            pl.BlockSpec((tm, gw), lambda i: (i, 2)),
            pl.BlockSpec((tm, gw), lambda i: (i, 3)),
            pl.BlockSpec((tm, gw), lambda i: (i, 4)),
            pl.BlockSpec(mg.shape, full),
            pl.BlockSpec(ws.shape, lambda i: (0, 0, 0)),
            pl.BlockSpec(bst.shape, full),
            pl.BlockSpec(gg.shape, full),
            pl.BlockSpec(cw.shape, full),
        ],
        out_specs=pl.BlockSpec((tm, wm + gw + cwid), lambda i: (i, 0)),
        out_shape=jax.ShapeDtypeStruct((n_rows, wm + gw + cwid), BF16),
        compiler_params=_params(1),
        name="mixer_epilogue",
    )(hf, hb, z_a, z_b, z_b, z_b, z_b, z_b, mg, ws, bst, gg, cw)


def _wout_kernel(y_ref, w_ref, xl_ref, xc_ref, g_ref, o_ref, wb_scr, *, n_lat_tiles):
    @pl.when(pl.program_id(1) == 0)
    def _():
        wb_scr[...] = w_ref[...].astype(BF16)

    x = jnp.where(pl.program_id(1) < n_lat_tiles, xl_ref[...], xc_ref[...])
    acc = jnp.dot(y_ref[...], wb_scr[...], preferred_element_type=F32)
    o_ref[...] = x + g_ref[...] * acc


def _wout(y, w3, layer, src, mod3, n_lat_rows, seq, n_batch):
    m, k = y.shape
    d = w3.shape[2]
    tm = _largest_divisor(m, (MM_ROW_TILE, 256))
    tn = _largest_divisor(d, (1024, 512, 256, 128))
    mrow = functools.partial(_mod_row, tile=tm, n_lat_rows=n_lat_rows, seq=seq, n_batch=n_batch)
    gate_blk0 = (2 * d) // tn
    return pl.pallas_call(
        functools.partial(_wout_kernel, n_lat_tiles=n_lat_rows // tm),
        grid=(d // tn, m // tm),
        in_specs=[
            pl.BlockSpec((tm, k), lambda j, i: (i, 0)),
            pl.BlockSpec((None, k, tn), lambda j, i: (layer, 0, j), pipeline_mode=pl.Buffered(1)),
        ] + _row_specs(src, tm, tn, n_lat_rows, lambda j, i: j) + [
            pl.BlockSpec((None, 1, tn), lambda j, i: (mrow(i), 0, gate_blk0 + j)),
        ],
        out_specs=pl.BlockSpec((tm, tn), lambda j, i: (i, j)),
        out_shape=jax.ShapeDtypeStruct((m, d), F32),
        scratch_shapes=[pltpu.VMEM((k, tn), BF16)],
        compiler_params=_params(2),
        name="out_projection",
    )(y, w3, src[0], src[1], mod3)


def _router_kernel(x_ref, g_ref, sh_ref, sc_ref, wr_ref, br_ref, h_ref, w_ref, e_ref, *, n_experts):
    h = _norm_modulate(x_ref[...], g_ref[...], sh_ref[...], sc_ref[...])
    h_ref[...] = _pack_halves(h)
    tm = h.shape[0]
    lanes = w_ref.shape[1]
    h_hi = h.astype(BF16)
    h_lo = (h - h_hi.astype(F32)).astype(BF16)
    prod = jnp.dot(jnp.concatenate([h_hi, h_lo], axis=0), wr_ref[...], preferred_element_type=F32)
    logits = (prod[:tm, :lanes] + (prod[:tm, lanes:] + prod[tm:, :lanes])) + prod[tm:, lanes:]
    scores = jax.nn.sigmoid(logits)
    sel = scores + br_ref[...]
    epg = n_experts // N_EXPERT_GROUPS
    sel_c = [sel[:, e:e + 1] for e in range(n_experts)]
    sc_c = [scores[:, e:e + 1] for e in range(n_experts)]

    best_g = None
    for gi in range(N_EXPERT_GROUPS):
        mem = sel_c[gi * epg:(gi + 1) * epg]
        gs = None
        for a in range(epg):
            for b in range(a + 1, epg):
                pair = mem[a] + mem[b]
                gs = pair if gs is None else jnp.maximum(gs, pair)
        if best_g is None:
            best_g, gidx = gs, jnp.zeros_like(gs, dtype=jnp.int32)
        else:
            upd = gs > best_g
            gidx = jnp.where(upd, gi, gidx)
            best_g = jnp.where(upd, gs, best_g)

    in_sel, in_sc = [], []
    for a in range(epg):
        vs, vr = sel_c[a], sc_c[a]
        for gi in range(1, N_EXPERT_GROUPS):
            vs = jnp.where(gidx == gi, sel_c[gi * epg + a], vs)
            vr = jnp.where(gidx == gi, sc_c[gi * epg + a], vr)
        in_sel.append(vs)
        in_sc.append(vr)

    v1, i1, s1 = in_sel[0], jnp.zeros_like(gidx), in_sc[0]
    for a in range(1, epg):
        upd = in_sel[a] > v1
        i1 = jnp.where(upd, a, i1)
        s1 = jnp.where(upd, in_sc[a], s1)
        v1 = jnp.where(upd, in_sel[a], v1)
    v2 = jnp.full_like(v1, -jnp.inf)
    i2 = jnp.zeros_like(gidx)
    s2 = jnp.zeros_like(s1)
    for a in range(epg):
        upd = jnp.logical_and(i1 != a, in_sel[a] > v2)
        i2 = jnp.where(upd, a, i2)
        s2 = jnp.where(upd, in_sc[a], s2)
        v2 = jnp.where(upd, in_sel[a], v2)
    denom = s1 + s2
    lane = lax.broadcasted_iota(jnp.int32, w_ref.shape, 1)
    w_ref[...] = jnp.where(lane == 0, s1 / denom, jnp.where(lane == 1, s2 / denom, 0.0))
    e_ref[...] = jnp.where(lane == 0, gidx * epg + i1, jnp.where(lane == 1, gidx * epg + i2, 0))


def _router(x_mix, g, mod3, wr2, br, n_experts, n_lat_rows, seq, n_batch):
    r, d = x_mix.shape
    tm = ROW_TILE
    mrow = functools.partial(_mod_row, tile=tm, n_lat_rows=n_lat_rows, seq=seq, n_batch=n_batch)
    lanes = br.shape[1]
    return pl.pallas_call(
        functools.partial(_router_kernel, n_experts=n_experts),
        grid=(r // tm,),
        in_specs=[
            pl.BlockSpec((tm, d), lambda i: (i, 0)),
            pl.BlockSpec((1, d), lambda i: (0, 0)),
            pl.BlockSpec((None, 1, d), lambda i: (mrow(i), 0, 3)),
            pl.BlockSpec((None, 1, d), lambda i: (mrow(i), 0, 4)),
            pl.BlockSpec(wr2.shape, lambda i: (0, 0)),
            pl.BlockSpec(br.shape, lambda i: (0, 0)),
        ],
        out_specs=[
            pl.BlockSpec((tm, d // 2), lambda i: (i, 0)),
            pl.BlockSpec((tm, lanes), lambda i: (i, 0)),
            pl.BlockSpec((tm, lanes), lambda i: (i, 0)),
        ],
        out_shape=[
            jax.ShapeDtypeStruct((r, d // 2), jnp.uint32),
            jax.ShapeDtypeStruct((r, lanes), F32),
            jax.ShapeDtypeStruct((r, lanes), jnp.int32),
        ],
        compiler_params=_params(1),
        name="moe_router",
    )(x_mix, g.reshape(1, d), mod3, mod3, wr2, br)


def _dispatch_tables(e_pair, n_experts, tm):
    flat_e = e_pair.reshape(-1)
    n_pairs = flat_e.shape[0]
    n_rows = n_pairs + n_experts * tm
    onehot = (flat_e[:, None] == jnp.arange(n_experts, dtype=jnp.int32)[None, :]).astype(jnp.int32)
    rank = jnp.sum((jnp.cumsum(onehot, axis=0) - onehot) * onehot, axis=1)
    counts = jnp.sum(onehot, axis=0)
    padded = ((counts + tm - 1) // tm) * tm
    ends = jnp.cumsum(padded)
    starts = ends - padded
    dest = jnp.sum(onehot * starts[None, :], axis=1) + rank
    n_used = (ends[-1] // tm).astype(jnp.int32)
    return dict(dest=dest.astype(jnp.int32), pad_lo=(starts + counts).astype(jnp.int32),
                pad_hi=ends.astype(jnp.int32), seg_tile0=(starts // tm).astype(jnp.int32),
                seg_tiles=(padded // tm).astype(jnp.int32), n_used=n_used.reshape(1), n_rows=n_rows)


def _dispatch_kernel(dest_ref, plo_ref, phi_ref, h_ref, xs_hbm, zrow, sem, zsem, *, n_experts):
    i = pl.program_id(0)
    tm = h_ref.shape[0]

    @pl.when(i == 0)
    def _():
        zrow[...] = _zeros_like_ref(zrow)

        def zero_copy(r):
            return pltpu.make_async_copy(zrow.at[pl.ds(0, 1)], xs_hbm.at[pl.ds(r, 1)], zsem.at[0])

        def start(r, carry):
            zero_copy(r).start()
            return carry

        def wait(r, carry):
            zero_copy(r).wait()
            return carry

        def tile_copy(t):
            return pltpu.make_async_copy(zrow, xs_hbm.at[pl.ds(t * tm, tm)], zsem.at[0])

        def start_tile(t, carry):
            tile_copy(t).start()
            return carry

        def wait_tile(t, carry):
            tile_copy(t).wait()
            return carry

        n_tiles = xs_hbm.shape[0] // tm
        first_unused = phi_ref[n_experts - 1] // tm
        for e in range(n_experts):
            lax.fori_loop(plo_ref[e], phi_ref[e], start, 0)
        lax.fori_loop(first_unused, n_tiles, start_tile, 0)
        for e in range(n_experts):
            lax.fori_loop(plo_ref[e], phi_ref[e], wait, 0)
        lax.fori_loop(first_unused, n_tiles, wait_tile, 0)

    def scatter(r, carry):
        for k in range(TOP_K):
            row = dest_ref[(i * tm + r) * TOP_K + k]
            pltpu.make_async_copy(h_ref.at[pl.ds(r, 1)], xs_hbm.at[pl.ds(row, 1)], sem.at[0]).start(priority=k)
        return carry

    lax.fori_loop(0, tm, scatter, 0)
    for k in range(TOP_K):
        pltpu.make_async_copy(h_ref, xs_hbm.at[pl.ds(0, tm)], sem.at[0]).wait()


def _dispatch(h2, tables, n_experts):
    r, d = h2.shape
    tm = ROW_TILE
    assert MOE_ROW_TILE % tm == 0
    return pl.pallas_call(
        functools.partial(_dispatch_kernel, n_experts=n_experts),
        grid_spec=pltpu.PrefetchScalarGridSpec(
            num_scalar_prefetch=3,
            grid=(r // tm,),
            in_specs=[pl.BlockSpec((tm, d), lambda i, de, lo, hi: (i, 0))],
            out_specs=pl.BlockSpec(memory_space=pl.ANY),
            scratch_shapes=[pltpu.VMEM((tm, d), h2.dtype), pltpu.SemaphoreType.DMA((1,)),
                            pltpu.SemaphoreType.DMA((1,))],
        ),
        out_shape=jax.ShapeDtypeStruct((tables["n_rows"], d), h2.dtype),
        compiler_params=_params(1),
        name="moe_dispatch",
    )(tables["dest"], tables["pad_lo"], tables["pad_hi"], h2)


def _segment_loop(first, n_tiles, n_stream, restart, load, compute, store):
    @pl.when(restart)
    def _():
        for a in range(TILE_LOOKAHEAD):
            @pl.when(a < n_stream)
            def _():
                load(a, a).start(priority=TILE_DMA_PRIORITY)

    def body(g, carry):
        slot = g % N_TILE_SLOTS
        load(g, slot).wait()
        ahead = g + TILE_LOOKAHEAD

        @pl.when(ahead < n_stream)
        def _():
            load(ahead, ahead % N_TILE_SLOTS).start(priority=TILE_DMA_PRIORITY)

        out_slot = g % 2

        @pl.when(g - first >= 2)
        def _():
            store(g - 2, out_slot).wait()

        compute(slot, out_slot)
        store(g, out_slot).start(priority=TILE_DMA_PRIORITY)
        return carry

    lax.fori_loop(first, first + n_tiles, body, 0)
    last = first + n_tiles - 1

    @pl.when(n_tiles >= 2)
    def _():
        store(last - 1, (last - 1) % 2).wait()

    @pl.when(n_tiles >= 1)
    def _():
        store(last, last % 2).wait()


def _zero_tail_tiles(first_tile, n_tiles, zero_src, tile_copy):
    zero_src[...] = _zeros_like_ref(zero_src)

    def start(t, carry):
        tile_copy(t).start()
        return carry

    def wait(t, carry):
        tile_copy(t).wait()
        return carry

    lax.fori_loop(first_tile, n_tiles, start, 0)
    lax.fori_loop(first_tile, n_tiles, wait, 0)


def _moe_up_kernel(t0_ref, nt_ref, nu_ref, wg_ref, wu_ref, xs_hbm, a_hbm, wgb, wub, xbuf, obuf, xsem, osem):
    j = pl.program_id(0)
    e = pl.program_id(1)
    tm = xbuf.shape[1]

    def load(g, slot):
        return pltpu.make_async_copy(xs_hbm.at[pl.ds(g * tm, tm)], xbuf.at[slot], xsem.at[slot])

    def store(g, slot):
        return pltpu.make_async_copy(obuf.at[slot], a_hbm.at[j, pl.ds(g * tm, tm)], osem.at[slot])

    def compute(slot, out_slot):
        half = xbuf.shape[2]
        x_lo, x_hi = (h.astype(BF16) for h in _unpack_halves(xbuf[slot]))

        def project(w):
            return (jnp.dot(x_lo, w[0:half, :], preferred_element_type=F32)
                    + jnp.dot(x_hi, w[half:2 * half, :], preferred_element_type=F32))

        gate = project(wgb)
        up = project(wub)
        obuf[out_slot] = ((gate * jax.nn.sigmoid(gate)) * up).astype(obuf.dtype)

    @pl.when(nt_ref[e] > 0)
    def _():
        wgb[...] = wg_ref[...].astype(BF16)
        wub[...] = wu_ref[...].astype(BF16)

    _segment_loop(t0_ref[e], nt_ref[e], nu_ref[0], e == 0, load, compute, store)

    @pl.when(e == pl.num_programs(1) - 1)
    def _():
        _zero_tail_tiles(
            nu_ref[0], a_hbm.shape[1] // tm, obuf.at[0],
            lambda t: pltpu.make_async_copy(obuf.at[0], a_hbm.at[j, pl.ds(t * tm, tm)], osem.at[0]))


def _moe_up(xs, tables, wg4, wu4, layer, tm):
    n_rows = xs.shape[0]
    n_experts, d, f = wg4.shape[1:]
    fs = f // MOE_UP_SPLITS
    w_spec = pl.BlockSpec((None, None, d, fs), lambda j, e, t0, nt, nu: (layer, e, 0, j))
    return pl.pallas_call(
        _moe_up_kernel,
        grid_spec=pltpu.PrefetchScalarGridSpec(
            num_scalar_prefetch=3,
            grid=(MOE_UP_SPLITS, n_experts),
            in_specs=[w_spec, w_spec, pl.BlockSpec(memory_space=pl.ANY)],
            out_specs=pl.BlockSpec(memory_space=pl.ANY),
            scratch_shapes=[
                pltpu.VMEM((d, fs), BF16), pltpu.VMEM((d, fs), BF16),
                pltpu.VMEM((N_TILE_SLOTS, tm, xs.shape[1]), xs.dtype), pltpu.VMEM((2, tm, fs), BF16),
                pltpu.SemaphoreType.DMA((N_TILE_SLOTS,)), pltpu.SemaphoreType.DMA((2,)),
            ],
        ),
        out_shape=jax.ShapeDtypeStruct((MOE_UP_SPLITS, n_rows, fs), BF16),
        compiler_params=_params(2),
        name="moe_up",
    )(tables["seg_tile0"], tables["seg_tiles"], tables["n_used"], wg4, wu4, xs)


def _moe_down_kernel(t0_ref, nt_ref, nu_ref, wd_ref, a_hbm, y_hbm, wdb, abuf, ybuf, asem, ysem):
    e = pl.program_id(0)
    n_splits, tm, fs = abuf.shape[1:]

    def load(g, slot):
        return pltpu.make_async_copy(a_hbm.at[:, pl.ds(g * tm, tm)], abuf.at[slot], asem.at[slot])

    def store(g, slot):
        return pltpu.make_async_copy(ybuf.at[slot], y_hbm.at[pl.ds(g * tm, tm)], ysem.at[slot])

    def compute(slot, out_slot):
        acc = jnp.dot(abuf[slot, 0], wdb[0:fs, :], preferred_element_type=F32)
        for s in range(1, n_splits):
            acc = acc + jnp.dot(abuf[slot, s], wdb[s * fs:(s + 1) * fs, :], preferred_element_type=F32)
        ybuf[out_slot] = _pack_halves(acc)

    @pl.when(nt_ref[e] > 0)
    def _():
        wdb[...] = wd_ref[...].astype(BF16)

    _segment_loop(t0_ref[e], nt_ref[e], nu_ref[0], e == 0, load, compute, store)

    @pl.when(e == pl.num_programs(0) - 1)
    def _():
        _zero_tail_tiles(
            nu_ref[0], y_hbm.shape[0] // tm, ybuf.at[0],
            lambda t: pltpu.make_async_copy(ybuf.at[0], y_hbm.at[pl.ds(t * tm, tm)], ysem.at[0]))


def _moe_down(a, tables, wd4, layer, tm):
    n_splits, n_rows, fs = a.shape
    n_experts, f, d = wd4.shape[1:]
    return pl.pallas_call(
        _moe_down_kernel,
        grid_spec=pltpu.PrefetchScalarGridSpec(
            num_scalar_prefetch=3,
            grid=(n_experts,),
            in_specs=[
                pl.BlockSpec((None, None, f, d), lambda e, t0, nt, nu: (layer, e, 0, 0)),
                pl.BlockSpec(memory_space=pl.ANY),
            ],
            out_specs=pl.BlockSpec(memory_space=pl.ANY),
            scratch_shapes=[
                pltpu.VMEM((f, d), BF16),
                pltpu.VMEM((N_TILE_SLOTS, n_splits, tm, fs), BF16), pltpu.VMEM((2, tm, d // 2), jnp.uint32),
                pltpu.SemaphoreType.DMA((N_TILE_SLOTS,)), pltpu.SemaphoreType.DMA((2,)),
            ],
        ),
        out_shape=jax.ShapeDtypeStruct((n_rows, d // 2), jnp.uint32),
        compiler_params=_params(1),
        name="moe_down",
    )(tables["seg_tile0"], tables["seg_tiles"], tables["n_used"], wd4, a)


def _combine_kernel(pos_ref, x_ref, w_ref, g_ref, fg_ref, y_hbm, o_ref, ybuf, sem, *, final_norm):
    i = pl.program_id(0)
    n = pl.num_programs(0)
    tm = x_ref.shape[0]

    def gather(tile, slot):
        def body(r, carry):
            for k in range(TOP_K):
                row = pos_ref[(tile * tm + r) * TOP_K + k]
                pltpu.make_async_copy(
                    y_hbm.at[pl.ds(row, 1)], ybuf.at[slot, k, pl.ds(r, 1)], sem.at[slot]).start(priority=k)
            return carry
        lax.fori_loop(0, tm, body, 0)

    @pl.when(i == 0)
    def _():
        gather(0, 0)

    @pl.when(i + 1 < n)
    def _():
        gather(i + 1, (i + 1) % 2)

    slot = i % 2
    for k in range(TOP_K):
        pltpu.make_async_copy(y_hbm.at[pl.ds(0, tm)], ybuf.at[slot, k], sem.at[slot]).wait()
    w = w_ref[...]
    y_first = jnp.concatenate(_unpack_halves(ybuf[slot, 0]), axis=1)
    y_second = jnp.concatenate(_unpack_halves(ybuf[slot, 1]), axis=1)
    moe = w[:, 0:1] * y_first + w[:, 1:2] * y_second
    out = x_ref[...] + g_ref[...] * moe
    if final_norm:
        out = (out * lax.rsqrt(jnp.mean(out * out, axis=-1, keepdims=True) + EPS)) * fg_ref[...]
    o_ref[...] = out


def _combine(x_mix, route_w, pos, y, mod3, final_g, n_rows, n_lat_rows, seq, n_batch, final_norm):
    d = x_mix.shape[1]
    tm = ROW_TILE
    lanes = route_w.shape[1]
    mrow = functools.partial(_mod_row, tile=tm, n_lat_rows=n_lat_rows, seq=seq, n_batch=n_batch)
    return pl.pallas_call(
        functools.partial(_combine_kernel, final_norm=final_norm),
        grid_spec=pltpu.PrefetchScalarGridSpec(
            num_scalar_prefetch=1,
            grid=(n_rows // tm,),
            in_specs=[
                pl.BlockSpec((tm, d), lambda i, p: (i, 0)),
                pl.BlockSpec((tm, lanes), lambda i, p: (i, 0)),
                pl.BlockSpec((None, 1, d), lambda i, p: (mrow(i), 0, 5)),
                pl.BlockSpec((1, d), lambda i, p: (0, 0)),
                pl.BlockSpec(memory_space=pl.ANY),
            ],
            out_specs=pl.BlockSpec((tm, d), lambda i, p: (i, 0)),
            scratch_shapes=[pltpu.VMEM((2, TOP_K, tm, y.shape[1]), y.dtype), pltpu.SemaphoreType.DMA((2,))],
        ),
        out_shape=jax.ShapeDtypeStruct((n_rows, d), F32),
        compiler_params=_params(1),
        name="moe_combine",
    )(pos, x_mix, route_w, mod3, final_g.reshape(1, d), y)


def kernel(x, c, ctx, c_ctx, w_mod, b_mod, norm1_g, norm2_g, w_in, b_gates, mlstm_norm_g, gmlp_ws, gmlp_bs,
           gmlp_norm_g, conv_w, w_out, w_router, b_router, w_gate_e, w_up_e, w_down_e, final_g):
    n_batch, seq, d = x.shape
    ctx_len = ctx.shape[1]
    depth = w_mod.shape[0]
    wm = mlstm_norm_g.shape[1]
    n_experts = w_router.shape[1]
    n_lat_rows = n_batch * seq
    n_all_rows = n_lat_rows + n_batch * ctx_len
    n_gate_cols = N_DIRS * 2 * MLSTM_HEADS
    gate_col0 = 3 * wm
    assert seq % ROW_TILE == 0 and seq % MM_ROW_TILE == 0 and (n_batch * ctx_len) % ROW_TILE == 0
    assert n_batch + 1 <= V7X_SUBLANES and n_experts <= V7X_LANES and n_gate_cols <= V7X_LANES

    src = (x.reshape(n_lat_rows, d), ctx.reshape(n_batch * ctx_len, d), 0)
    n_in_cols = w_in.shape[2]
    w_in_t = jnp.swapaxes(w_in, 1, 2).reshape(depth * n_in_cols, d)

    cond = jnp.zeros((V7X_SUBLANES, d), F32).at[:n_batch].set(c).at[n_batch].set(c_ctx)
    mods = _mod_table(cond, w_mod, b_mod)

    wr = jnp.zeros((d, V7X_LANES), F32).at[:, :n_experts].set(w_router)
    wr_hi = wr.astype(BF16)
    wr2 = jnp.concatenate([wr_hi, (wr - wr_hi.astype(F32)).astype(BF16)], axis=1)
    br = jnp.zeros((1, V7X_LANES), F32).at[0, :n_experts].set(b_router)

    for layer in range(depth):
        last = layer + 1 == depth
        mod3 = mods[layer].reshape(V7X_SUBLANES, 1, N_MOD * d)
        w_row0 = layer * n_in_cols
        w_gate_t = jnp.zeros((V7X_LANES, d), F32).at[:n_gate_cols].set(
            w_in_t[w_row0 + gate_col0:w_row0 + gate_col0 + n_gate_cols])
        bias = jnp.zeros((1, V7X_LANES), F32).at[0, :n_gate_cols].set(b_gates[layer].reshape(-1))
        bias_t = jnp.broadcast_to(b_gates[layer].reshape(-1, 1), (n_gate_cols, MLSTM_CHUNK))

        h1 = _normmod(src, n_all_rows, norm1_g[layer], mod3, 0, n_lat_rows, seq, n_batch)
        z_a = _matmul_wt(h1, w_in_t, w_row0, gate_col0, BF16)
        z_b = _matmul_wt(h1, w_in_t, w_row0 + gate_col0 + n_gate_cols, n_in_cols - gate_col0 - n_gate_cols, BF16)
        gates, gates_t = _gates(h1, w_gate_t.T, w_gate_t)
        hf, hb = _mlstm(z_a, gates, gates_t, bias, bias_t, n_batch, seq, ctx_len, wm)

        n_rows = n_lat_rows if last else n_all_rows
        y = _mix(hf, hb, z_a, z_b, mlstm_norm_g[layer].reshape(1, wm), gmlp_ws[layer].astype(BF16),
                 gmlp_bs[layer].T, gmlp_norm_g[layer].reshape(1, -1), conv_w[layer], n_rows, n_lat_rows, ctx_len)
        x_mix = _wout(y, w_out, layer, src, mod3, n_lat_rows, seq, n_batch)

        h2, route_w, route_e = _router(x_mix, norm2_g[layer], mod3, wr2, br, n_experts, n_lat_rows, seq, n_batch)
        tables = _dispatch_tables(route_e[:, :TOP_K], n_experts, MOE_ROW_TILE)
        xs = _dispatch(h2, tables, n_experts)
        act = _moe_up(xs, tables, w_gate_e, w_up_e, layer, MOE_ROW_TILE)
        y_moe = _moe_down(act, tables, w_down_e, layer, MOE_ROW_TILE)
        rows = _combine(x_mix, route_w, tables["dest"], y_moe, mod3, final_g, n_rows, n_lat_rows, seq, n_batch, last)
        src = (rows, rows, n_lat_rows)

    return rows.reshape(n_batch, seq, d)
```

```python
import functools

import jax
import jax.numpy as jnp
from jax import lax
from jax.experimental import pallas as pl
from jax.experimental.pallas import tpu as pltpu

F32 = jnp.float32
BF16 = jnp.bfloat16

N_MOD = 6
EPS = 1e-6
N_DIRS = 2
MLSTM_HEADS = 4
MLSTM_CHUNK = 128
GMLP_HEADS = 4
GMLP_CHUNK = 128
GRID_W = 64
N_EXPERT_GROUPS = 4
TOP_K = 2

V7X_LANES = 128
V7X_SUBLANES = 8
V7X_VMEM_BYTES = 64 * 1024 * 1024
VMEM_LIMIT = V7X_VMEM_BYTES - 8 * 1024 * 1024

ROW_TILE = 256
MM_ROW_TILE = 512
MOE_ROW_TILE = 256
MOE_UP_SPLITS = 2
TILE_DMA_PRIORITY = 1
TILE_LOOKAHEAD = 2
N_TILE_SLOTS = TILE_LOOKAHEAD + 1
COMBINE_ROWS = 2 * V7X_SUBLANES
ROW_DMA_UNROLL = 4


def _params(n_axes):
    return pltpu.CompilerParams(
        dimension_semantics=("arbitrary",) * n_axes, vmem_limit_bytes=VMEM_LIMIT)


def _largest_divisor(n, candidates):
    for c in candidates:
        if n % c == 0:
            return c
    raise ValueError(f"no tile in {candidates} divides {n}")


def _mod_row(i, tile, n_lat_rows, seq, n_batch):
    return jnp.where(i * tile < n_lat_rows, (i * tile) // seq, n_batch)


def _mod_kernel(a_ref, w_ref, b_ref, o_ref):
    a = a_ref[...]
    a = a * jax.nn.sigmoid(a)
    acc = jnp.dot(a.astype(BF16), w_ref[...].astype(BF16), preferred_element_type=F32)
    o_ref[...] = acc + b_ref[...]


def _mod_table(a, w_mod, b_mod):
    depth, d, n = w_mod.shape
    tn = _largest_divisor(n, (1024, 512, 256, 128))
    rows = a.shape[0]
    return pl.pallas_call(
        _mod_kernel,
        grid=(depth, n // tn),
        in_specs=[
            pl.BlockSpec((rows, d), lambda l, j: (0, 0)),
            pl.BlockSpec((None, d, tn), lambda l, j: (l, 0, j)),
            pl.BlockSpec((None, 1, tn), lambda l, j: (l, 0, j)),
        ],
        out_specs=pl.BlockSpec((None, rows, tn), lambda l, j: (l, 0, j)),
        out_shape=jax.ShapeDtypeStruct((depth, rows, n), F32),
        compiler_params=_params(2),
        name="mod_table",
    )(a, w_mod, b_mod.reshape(depth, 1, n))


def _norm_modulate(x, g, shift, scale):
    y = x * lax.rsqrt(jnp.mean(x * x, axis=-1, keepdims=True) + EPS)
    return (y * g) * (1.0 + scale) + shift


def _pack_halves(x):
    half = x.shape[1] // 2
    return pltpu.pack_elementwise([x[:, :half], x[:, half:]], packed_dtype=BF16)


def _unpack_halves(p):
    return tuple(pltpu.unpack_elementwise(p, index=k, packed_dtype=BF16, unpacked_dtype=F32) for k in range(2))


def _zeros_like_ref(ref):
    if ref.dtype == jnp.uint32:
        return _pack_halves(jnp.zeros(ref.shape[:-1] + (2 * ref.shape[-1],), F32))
    return jnp.zeros(ref.shape, ref.dtype)


def _row_specs(src, tm, tn, n_lat_rows, col):
    lat, ctx, ctx_row0 = src
    n_lat_tiles = n_lat_rows // tm
    assert n_lat_rows % tm == 0 and ctx_row0 % tm == 0
    return [
        pl.BlockSpec((tm, tn), lambda *g: (jnp.minimum(g[-1], n_lat_tiles - 1), col(*g))),
        pl.BlockSpec((tm, tn), lambda *g: (ctx_row0 // tm + jnp.maximum(g[-1] - n_lat_tiles, 0), col(*g))),
    ]


def _normmod_kernel(xl_ref, xc_ref, g_ref, sh_ref, sc_ref, wg_ref, wgt_ref, o_ref, gate_ref, gate_t_ref, *,
                    n_lat_tiles):
    x = jnp.where(pl.program_id(0) < n_lat_tiles, xl_ref[...], xc_ref[...])
    h = _norm_modulate(x, g_ref[...], sh_ref[...], sc_ref[...]).astype(o_ref.dtype)
    o_ref[...] = h
    gate_ref[...] = jnp.dot(h, wg_ref[...].astype(BF16), preferred_element_type=F32)
    gate_t_ref[...] = lax.dot_general(wgt_ref[...].astype(BF16), h, (((1,), (1,)), ((), ())),
                                      preferred_element_type=F32)


def _normmod(src, n_rows, g, mod3, shift_col, w_gate, w_gate_t, n_lat_rows, seq, n_batch):
    d = g.shape[0]
    lanes = w_gate.shape[1]
    tm = ROW_TILE
    mrow = functools.partial(_mod_row, tile=tm, n_lat_rows=n_lat_rows, seq=seq, n_batch=n_batch)
    return pl.pallas_call(
        functools.partial(_normmod_kernel, n_lat_tiles=n_lat_rows // tm),
        grid=(n_rows // tm,),
        in_specs=_row_specs(src, tm, d, n_lat_rows, lambda i: 0) + [
            pl.BlockSpec((1, d), lambda i: (0, 0)),
            pl.BlockSpec((None, 1, d), lambda i: (mrow(i), 0, shift_col)),
            pl.BlockSpec((None, 1, d), lambda i: (mrow(i), 0, shift_col + 1)),
            pl.BlockSpec((d, lanes), lambda i: (0, 0)),
            pl.BlockSpec((lanes, d), lambda i: (0, 0)),
        ],
        out_specs=[
            pl.BlockSpec((tm, d), lambda i: (i, 0)),
            pl.BlockSpec((tm, lanes), lambda i: (i, 0)),
            pl.BlockSpec((lanes, tm), lambda i: (0, i)),
        ],
        out_shape=[
            jax.ShapeDtypeStruct((n_rows, d), BF16),
            jax.ShapeDtypeStruct((n_rows, lanes), F32),
            jax.ShapeDtypeStruct((lanes, n_rows), F32),
        ],
        compiler_params=_params(1),
        name="norm_modulate",
    )(src[0], src[1], g.reshape(1, d), mod3, mod3, w_gate, w_gate_t)


def _mm_wt_kernel(a_ref, w_ref, o_ref, wb_scr):
    @pl.when(pl.program_id(1) == 0)
    def _():
        wb_scr[...] = w_ref[...].astype(BF16)

    acc = lax.dot_general(a_ref[...], wb_scr[...], (((1,), (1,)), ((), ())), preferred_element_type=F32)
    o_ref[...] = acc.astype(o_ref.dtype)


def _matmul_wt(a, wt, row0, n_cols, out_dtype):
    m, k = a.shape
    tm = _largest_divisor(m, (MM_ROW_TILE, 256, 128))
    tn = _largest_divisor(n_cols, (1024, 512, 256, 128))
    return pl.pallas_call(
        _mm_wt_kernel,
        grid=(n_cols // tn, m // tm),
        in_specs=[
            pl.BlockSpec((tm, k), lambda j, i: (i, 0)),
            pl.BlockSpec((pl.Element(tn), pl.Element(k)),
                         lambda j, i: (pl.multiple_of(row0 + j * tn, V7X_SUBLANES), 0)),
        ],
        out_specs=pl.BlockSpec((tm, tn), lambda j, i: (i, j)),
        out_shape=jax.ShapeDtypeStruct((m, n_cols), out_dtype),
        scratch_shapes=[pltpu.VMEM((tn, k), BF16)],
        compiler_params=_params(2),
        name="projection_wt",
    )(a, wt)


def _log_sigmoid(x):
    return jnp.minimum(x, 0.0) - jnp.log1p(jnp.exp(-jnp.abs(x)))


def _mlstm_chain(q, k, v, i_col, f_col, i_row, f_row, before, after, c_ref, m_ref, o_ref, cols):
    chunk, dk = q.shape
    dv = v.shape[1]
    lf_col = _log_sigmoid(f_col)
    lf_row = _log_sigmoid(f_row)
    b_col = jnp.sum(jnp.where(before, lf_row, 0.0), axis=1, keepdims=True)
    b_row = jnp.sum(jnp.where(after, lf_col, 0.0), axis=0, keepdims=True)
    b_last = jnp.sum(lf_row, axis=1, keepdims=True)

    m_prev = m_ref[0:1, 0:1]
    dmat = jnp.where(before, b_col - b_row + i_row, -jnp.inf)
    m_inter = b_col + m_prev
    m_t = jnp.maximum(jnp.max(dmat, axis=1, keepdims=True), m_inter)
    scale = dk ** -0.5
    qk = lax.dot_general(q, k, (((1,), (1,)), ((), ())), preferred_element_type=F32)
    s = (qk * (scale * jnp.exp(dmat - m_t))).astype(BF16)
    v_aug = jnp.concatenate([v, jnp.ones((chunk, V7X_LANES), BF16)], axis=1)
    c_prev = c_ref[...]
    inter = scale * jnp.exp(m_inter - m_t)
    num = (jnp.dot(s, v_aug, preferred_element_type=F32)
           + inter * jnp.dot(q, c_prev.astype(BF16), preferred_element_type=F32))
    den = num[:, dv:dv + 1]
    o_ref[:, cols] = (num[:, :dv] / jnp.maximum(jnp.abs(den), jnp.exp(-m_t))).astype(o_ref.dtype)

    g_end = b_last - b_col + i_col
    m_new = jnp.maximum(b_last + m_prev, jnp.max(g_end, axis=0, keepdims=True))
    decay = jnp.exp(b_last + m_prev - m_new)
    wv = (jnp.exp(g_end - m_new) * v_aug.astype(F32)).astype(BF16)
    c_ref[...] = decay * c_prev + lax.dot_general(k, wv, (((0,), (0,)), ((), ())), preferred_element_type=F32)
    m_ref[...] = jnp.broadcast_to(m_new, m_ref.shape)


def _mlstm_kernel(qf_ref, kf_ref, vf_ref, gf_ref, gtf_ref, qb_ref, kb_ref, vb_ref, gb_ref, gtb_ref,
                  b_ref, bt_ref, of_ref, ob_ref, c_scr, m_scr):
    nh = MLSTM_HEADS
    chunk = qf_ref.shape[0]
    dk = qf_ref.shape[1] // nh
    dv = vf_ref.shape[1] // nh

    @pl.when(pl.program_id(1) == 0)
    def _():
        c_scr[...] = jnp.zeros_like(c_scr)
        m_scr[...] = jnp.zeros_like(m_scr)

    t_idx = lax.broadcasted_iota(jnp.int32, (chunk, chunk), 0)
    s_idx = lax.broadcasted_iota(jnp.int32, (chunk, chunk), 1)
    dirs = ((qf_ref, kf_ref, vf_ref, gf_ref, gtf_ref, of_ref), (qb_ref, kb_ref, vb_ref, gb_ref, gtb_ref, ob_ref))
    for d, (q_ref, k_ref, v_ref, g_ref, gt_ref, o_ref) in enumerate(dirs):
        before = s_idx <= t_idx if d == 0 else s_idx >= t_idx
        after = s_idx >= t_idx if d == 0 else s_idx <= t_idx
        g = g_ref[...] + b_ref[...]
        gt = gt_ref[...] + bt_ref[...]
        for h in range(nh):
            ci = d * 2 * nh + h
            cf = ci + nh
            _mlstm_chain(
                q_ref[:, h * dk:(h + 1) * dk], k_ref[:, h * dk:(h + 1) * dk], v_ref[:, h * dv:(h + 1) * dv],
                g[:, ci:ci + 1], g[:, cf:cf + 1], gt[ci:ci + 1, :], gt[cf:cf + 1, :], before, after,
                c_scr.at[d * nh + h], m_scr.at[d * nh + h], o_ref, slice(h * dv, (h + 1) * dv))


def _mlstm(z, gates, gates_t, bias, bias_t, n_batch, seq, ctx_len, width):
    r = z.shape[0]
    nh = MLSTM_HEADS
    dv = width // nh
    dk = dv // 2
    qk_w = nh * dk
    chunk = MLSTM_CHUNK
    n_ctx = ctx_len // chunk
    n_lat = seq // chunk
    n_gate_rows = bias_t.shape[0]

    def tile(d, b, step):
        lat = step - n_ctx
        ctx_tile = n_batch * n_lat + b * n_ctx + (step if d == 0 else n_ctx - 1 - step)
        lat_tile = b * n_lat + (lat if d == 0 else n_lat - 1 - lat)
        return jnp.where(step < n_ctx, ctx_tile, lat_tile)

    def dir_specs(d):
        return [
            pl.BlockSpec((chunk, qk_w), lambda b, s: (tile(d, b, s), 0)),
            pl.BlockSpec((chunk, qk_w), lambda b, s: (tile(d, b, s), 1)),
            pl.BlockSpec((chunk, width), lambda b, s: (tile(d, b, s), 1)),
            pl.BlockSpec((chunk, gates.shape[1]), lambda b, s: (tile(d, b, s), 0)),
            pl.BlockSpec((n_gate_rows, chunk), lambda b, s: (0, tile(d, b, s))),
        ]

    return pl.pallas_call(
        _mlstm_kernel,
        grid=(n_batch, n_ctx + n_lat),
        in_specs=dir_specs(0) + dir_specs(1) + [
            pl.BlockSpec(bias.shape, lambda b, s: (0, 0)),
            pl.BlockSpec(bias_t.shape, lambda b, s: (0, 0)),
        ],
        out_specs=[
            pl.BlockSpec((chunk, width), lambda b, s: (tile(0, b, s), 0)),
            pl.BlockSpec((chunk, width), lambda b, s: (tile(1, b, s), 0)),
        ],
        out_shape=[jax.ShapeDtypeStruct((r, width), BF16), jax.ShapeDtypeStruct((r, width), BF16)],
        scratch_shapes=[
            pltpu.VMEM((N_DIRS * nh, dk, dv + V7X_LANES), F32),
            pltpu.VMEM((N_DIRS * nh, V7X_SUBLANES, V7X_LANES), F32),
        ],
        compiler_params=_params(2),
        name="mlstm_scan",
    )(z, z, z, gates, gates_t, z, z, z, gates, gates_t, bias, bias_t)


def _mix_kernel(hf_ref, hb_ref, o_ref, u_ref, v_ref, bg_ref, cg_ref, xin_ref, mg_ref, ws_ref, bst_ref, gg_ref,
                cw_ref, y_ref, *, n_lat_rows, ctx_len):
    i = pl.program_id(0)
    tm = y_ref.shape[0]
    wm = o_ref.shape[1]
    gw = u_ref.shape[1]
    dv = wm // MLSTM_HEADS
    hdim = gw // GMLP_HEADS

    for hd in range(MLSTM_HEADS):
        cols = slice(hd * dv, (hd + 1) * dv)
        hh = hf_ref[:, cols].astype(F32) + hb_ref[:, cols].astype(F32)
        hh = hh * lax.rsqrt(jnp.mean(hh * hh, axis=-1, keepdims=True) + EPS)
        out = (hh * mg_ref[:, cols]) * jax.nn.sigmoid(o_ref[:, cols].astype(F32))
        y_ref[:, cols] = out.astype(y_ref.dtype)

    u = jax.nn.gelu(u_ref[...].astype(F32))
    v = jax.nn.gelu(v_ref[...].astype(F32))
    v = (v * lax.rsqrt(jnp.mean(v * v, axis=-1, keepdims=True) + EPS)) * gg_ref[...]
    vb = v.astype(BF16)
    for c in range(tm // GMLP_CHUNK):
        rows = slice(c * GMLP_CHUNK, (c + 1) * GMLP_CHUNK)
        for hd in range(GMLP_HEADS):
            cols = slice(hd * hdim, (hd + 1) * hdim)
            sp = jnp.dot(ws_ref[hd], vb[rows, cols], preferred_element_type=F32) + bst_ref[:, hd:hd + 1]
            y_ref[rows, wm + hd * hdim:wm + (hd + 1) * hdim] = (u[rows, cols] * sp).astype(y_ref.dtype)

    yv = cg_ref[...].astype(F32) * xin_ref[...].astype(F32)
    t = lax.broadcasted_iota(jnp.int32, (tm, 1), 0)
    is_lat = i * tm < n_lat_rows
    pos = jnp.where(is_lat, t % GRID_W, t % ctx_len)
    last = jnp.where(is_lat, GRID_W - 1, ctx_len - 1)
    prev = jnp.where(pos != 0, pltpu.roll(yv, 1, axis=0), 0.0)
    nxt = jnp.where(pos != last, pltpu.roll(yv, tm - 1, axis=0), 0.0)
    conv = cw_ref[0:1, :] * prev + cw_ref[1:2, :] * yv + cw_ref[2:3, :] * nxt
    y_ref[:, wm + gw:] = (bg_ref[...].astype(F32) * conv).astype(y_ref.dtype)


def _mix(hf, hb, z_a, z_b, mg, ws, bst, gg, cw, n_rows, n_lat_rows, ctx_len):
    wm = mg.shape[1]
    gw = gg.shape[1]
    cwid = cw.shape[1]
    tm = ROW_TILE
    assert tm % GRID_W == 0 and tm % ctx_len == 0 and tm % GMLP_CHUNK == 0
    assert gw == cwid
    full = lambda i: (0, 0)
    return pl.pallas_call(
        functools.partial(_mix_kernel, n_lat_rows=n_lat_rows, ctx_len=ctx_len),
        grid=(n_rows // tm,),
        in_specs=[
            pl.BlockSpec((tm, wm), lambda i: (i, 0)),
            pl.BlockSpec((tm, wm), lambda i: (i, 0)),
            pl.BlockSpec((tm, wm), lambda i: (i, 2)),
            pl.BlockSpec((tm, gw), lambda i: (i, 0)),
            pl.BlockSpec((tm, gw), lambda i: (i, 1)),
            pl.BlockSpec((tm, gw), lambda i: (i, 2)),
            pl.BlockSpec((tm, gw), lambda i: (i, 3)),
            pl.BlockSpec((tm, gw), lambda i: (i, 4)),
            pl.BlockSpec(mg.shape, full),
            pl.BlockSpec(ws.shape, lambda i: (0, 0, 0)),
            pl.BlockSpec(bst.shape, full),
            pl.BlockSpec(gg.shape, full),
            pl.BlockSpec(cw.shape, full),
        ],
        out_specs=pl.BlockSpec((tm, wm + gw + cwid), lambda i: (i, 0)),
        out_shape=jax.ShapeDtypeStruct((n_rows, wm + gw + cwid), BF16),
        compiler_params=_params(1),
        name="mixer_epilogue",
    )(hf, hb, z_a, z_b, z_b, z_b, z_b, z_b, mg, ws, bst, gg, cw)


def _wout_kernel(y_ref, w_ref, xl_ref, xc_ref, g_ref, o_ref, wb_scr, *, n_lat_tiles):
    @pl.when(pl.program_id(1) == 0)
    def _():
        wb_scr[...] = w_ref[...].astype(BF16)

    x = jnp.where(pl.program_id(1) < n_lat_tiles, xl_ref[...], xc_ref[...])
    acc = jnp.dot(y_ref[...], wb_scr[...], preferred_element_type=F32)
    o_ref[...] = x + g_ref[...] * acc


def _wout(y, w3, layer, src, mod3, n_lat_rows, seq, n_batch):
    m, k = y.shape
    d = w3.shape[2]
    tm = _largest_divisor(m, (MM_ROW_TILE, 256))
    tn = _largest_divisor(d, (1024, 512, 256, 128))
    mrow = functools.partial(_mod_row, tile=tm, n_lat_rows=n_lat_rows, seq=seq, n_batch=n_batch)
    gate_blk0 = (2 * d) // tn
    return pl.pallas_call(
        functools.partial(_wout_kernel, n_lat_tiles=n_lat_rows // tm),
        grid=(d // tn, m // tm),
        in_specs=[
            pl.BlockSpec((tm, k), lambda j, i: (i, 0)),
            pl.BlockSpec((None, k, tn), lambda j, i: (layer, 0, j), pipeline_mode=pl.Buffered(1)),
        ] + _row_specs(src, tm, tn, n_lat_rows, lambda j, i: j) + [
            pl.BlockSpec((None, 1, tn), lambda j, i: (mrow(i), 0, gate_blk0 + j)),
        ],
        out_specs=pl.BlockSpec((tm, tn), lambda j, i: (i, j)),
        out_shape=jax.ShapeDtypeStruct((m, d), F32),
        scratch_shapes=[pltpu.VMEM((k, tn), BF16)],
        compiler_params=_params(2),
        name="out_projection",
    )(y, w3, src[0], src[1], mod3)


def _router_kernel(x_ref, g_ref, sh_ref, sc_ref, wr_ref, br_ref, h_ref, w_ref, e_ref, *, n_experts):
    h = _norm_modulate(x_ref[...], g_ref[...], sh_ref[...], sc_ref[...])
    h_ref[...] = _pack_halves(h)
    tm = h.shape[0]
    lanes = w_ref.shape[1]
    h_hi = h.astype(BF16)
    h_lo = (h - h_hi.astype(F32)).astype(BF16)
    prod = jnp.dot(jnp.concatenate([h_hi, h_lo], axis=0), wr_ref[...], preferred_element_type=F32)
    logits = (prod[:tm, :lanes] + (prod[:tm, lanes:] + prod[tm:, :lanes])) + prod[tm:, lanes:]
    scores = jax.nn.sigmoid(logits.T)
    sel = scores + br_ref[...]
    epg = n_experts // N_EXPERT_GROUPS
    sel_c = [sel[e:e + 1, :] for e in range(n_experts)]
    sc_c = [scores[e:e + 1, :] for e in range(n_experts)]

    best_g = None
    for gi in range(N_EXPERT_GROUPS):
        mem = sel_c[gi * epg:(gi + 1) * epg]
        gs = None
        for a in range(epg):
            for b in range(a + 1, epg):
                pair = mem[a] + mem[b]
                gs = pair if gs is None else jnp.maximum(gs, pair)
        if best_g is None:
            best_g, gidx = gs, jnp.zeros_like(gs, dtype=jnp.int32)
        else:
            upd = gs > best_g
            gidx = jnp.where(upd, gi, gidx)
            best_g = jnp.where(upd, gs, best_g)

    in_sel, in_sc = [], []
    for a in range(epg):
        vs, vr = sel_c[a], sc_c[a]
        for gi in range(1, N_EXPERT_GROUPS):
            vs = jnp.where(gidx == gi, sel_c[gi * epg + a], vs)
            vr = jnp.where(gidx == gi, sc_c[gi * epg + a], vr)
        in_sel.append(vs)
        in_sc.append(vr)

    v1, i1, s1 = in_sel[0], jnp.zeros_like(gidx), in_sc[0]
    for a in range(1, epg):
        upd = in_sel[a] > v1
        i1 = jnp.where(upd, a, i1)
        s1 = jnp.where(upd, in_sc[a], s1)
        v1 = jnp.where(upd, in_sel[a], v1)
    v2 = jnp.full_like(v1, -jnp.inf)
    i2 = jnp.zeros_like(gidx)
    s2 = jnp.zeros_like(s1)
    for a in range(epg):
        upd = jnp.logical_and(i1 != a, in_sel[a] > v2)
        i2 = jnp.where(upd, a, i2)
        s2 = jnp.where(upd, in_sc[a], s2)
        v2 = jnp.where(upd, in_sel[a], v2)
    denom = s1 + s2
    sub = lax.broadcasted_iota(jnp.int32, scores.shape, 0)
    w_t = jnp.where(sub == 0, s1 / denom, jnp.where(sub == 1, s2 / denom, 0.0))
    e_t = jnp.where(sub == 0, gidx * epg + i1, jnp.where(sub == 1, gidx * epg + i2, 0))
    w_ref[...] = w_t.T
    e_ref[...] = e_t.astype(F32).T.astype(jnp.int32)


def _router(x_mix, g, mod3, wr2, br, n_experts, n_lat_rows, seq, n_batch):
    r, d = x_mix.shape
    tm = ROW_TILE
    mrow = functools.partial(_mod_row, tile=tm, n_lat_rows=n_lat_rows, seq=seq, n_batch=n_batch)
    lanes = br.shape[0]
    return pl.pallas_call(
        functools.partial(_router_kernel, n_experts=n_experts),
        grid=(r // tm,),
        in_specs=[
            pl.BlockSpec((tm, d), lambda i: (i, 0)),
            pl.BlockSpec((1, d), lambda i: (0, 0)),
            pl.BlockSpec((None, 1, d), lambda i: (mrow(i), 0, 3)),
            pl.BlockSpec((None, 1, d), lambda i: (mrow(i), 0, 4)),
            pl.BlockSpec(wr2.shape, lambda i: (0, 0)),
            pl.BlockSpec(br.shape, lambda i: (0, 0)),
        ],
        out_specs=[
            pl.BlockSpec((tm, d // 2), lambda i: (i, 0)),
            pl.BlockSpec((tm, lanes), lambda i: (i, 0)),
            pl.BlockSpec((tm, lanes), lambda i: (i, 0)),
        ],
        out_shape=[
            jax.ShapeDtypeStruct((r, d // 2), jnp.uint32),
            jax.ShapeDtypeStruct((r, lanes), F32),
            jax.ShapeDtypeStruct((r, lanes), jnp.int32),
        ],
        compiler_params=_params(1),
        name="moe_router",
    )(x_mix, g.reshape(1, d), mod3, mod3, wr2, br)


def _dispatch_tables(e_pair, n_experts, tm):
    flat_e = e_pair.reshape(-1)
    n_pairs = flat_e.shape[0]
    n_rows = n_pairs + n_experts * tm
    onehot = (flat_e[:, None] == jnp.arange(n_experts, dtype=jnp.int32)[None, :]).astype(jnp.int32)
    rank = jnp.sum((jnp.cumsum(onehot, axis=0) - onehot) * onehot, axis=1)
    counts = jnp.sum(onehot, axis=0)
    padded = ((counts + tm - 1) // tm) * tm
    ends = jnp.cumsum(padded)
    starts = ends - padded
    dest = jnp.sum(onehot * starts[None, :], axis=1) + rank
    n_used = (ends[-1] // tm).astype(jnp.int32)
    return dict(dest=dest.astype(jnp.int32), pad_lo=(starts + counts).astype(jnp.int32),
                pad_hi=ends.astype(jnp.int32), seg_tile0=(starts // tm).astype(jnp.int32),
                seg_tiles=(padded // tm).astype(jnp.int32), n_used=n_used.reshape(1), n_rows=n_rows)


def _dispatch_kernel(dest_ref, plo_ref, phi_ref, h_ref, xs_hbm, zrow, sem, zsem, *, n_experts):
    i = pl.program_id(0)
    tm = h_ref.shape[0]

    @pl.when(i == 0)
    def _():
        zrow[...] = _zeros_like_ref(zrow)

        def zero_copy(r):
            return pltpu.make_async_copy(zrow.at[pl.ds(0, 1)], xs_hbm.at[pl.ds(r, 1)], zsem.at[0])

        def start(r, carry):
            zero_copy(r).start()
            return carry

        def wait(r, carry):
            zero_copy(r).wait()
            return carry

        def tile_copy(t):
            return pltpu.make_async_copy(zrow, xs_hbm.at[pl.ds(t * tm, tm)], zsem.at[0])

        def start_tile(t, carry):
            tile_copy(t).start()
            return carry

        def wait_tile(t, carry):
            tile_copy(t).wait()
            return carry

        n_tiles = xs_hbm.shape[0] // tm
        first_unused = phi_ref[n_experts - 1] // tm
        for e in range(n_experts):
            lax.fori_loop(plo_ref[e], phi_ref[e], start, 0)
        lax.fori_loop(first_unused, n_tiles, start_tile, 0)
        for e in range(n_experts):
            lax.fori_loop(plo_ref[e], phi_ref[e], wait, 0)
        lax.fori_loop(first_unused, n_tiles, wait_tile, 0)

    def scatter(r, carry):
        for k in range(TOP_K):
            row = dest_ref[(i * tm + r) * TOP_K + k]
            pltpu.make_async_copy(h_ref.at[pl.ds(r, 1)], xs_hbm.at[pl.ds(row, 1)], sem.at[0]).start(priority=k)
        return carry

    lax.fori_loop(0, tm, scatter, 0, unroll=ROW_DMA_UNROLL)
    for k in range(TOP_K):
        pltpu.make_async_copy(h_ref, xs_hbm.at[pl.ds(0, tm)], sem.at[0]).wait()


def _dispatch(h2, tables, n_experts):
    r, d = h2.shape
    tm = ROW_TILE
    assert MOE_ROW_TILE % tm == 0
    return pl.pallas_call(
        functools.partial(_dispatch_kernel, n_experts=n_experts),
        grid_spec=pltpu.PrefetchScalarGridSpec(
            num_scalar_prefetch=3,
            grid=(r // tm,),
            in_specs=[pl.BlockSpec((tm, d), lambda i, de, lo, hi: (i, 0))],
            out_specs=pl.BlockSpec(memory_space=pl.ANY),
            scratch_shapes=[pltpu.VMEM((tm, d), h2.dtype), pltpu.SemaphoreType.DMA((1,)),
                            pltpu.SemaphoreType.DMA((1,))],
        ),
        out_shape=jax.ShapeDtypeStruct((tables["n_rows"], d), h2.dtype),
        compiler_params=_params(1),
        name="moe_dispatch",
    )(tables["dest"], tables["pad_lo"], tables["pad_hi"], h2)


def _segment_loop(first, n_tiles, n_stream, restart, load, compute, store):
    @pl.when(restart)
    def _():
        for a in range(TILE_LOOKAHEAD):
            @pl.when(a < n_stream)
            def _():
                load(a, a).start(priority=TILE_DMA_PRIORITY)

    def body(g, carry):
        slot = g % N_TILE_SLOTS
        load(g, slot).wait()
        ahead = g + TILE_LOOKAHEAD

        @pl.when(ahead < n_stream)
        def _():
            load(ahead, ahead % N_TILE_SLOTS).start(priority=TILE_DMA_PRIORITY)

        out_slot = g % 2

        @pl.when(g - first >= 2)
        def _():
            store(g - 2, out_slot).wait()

        compute(slot, out_slot)
        store(g, out_slot).start(priority=TILE_DMA_PRIORITY)
        return carry

    lax.fori_loop(first, first + n_tiles, body, 0)
    last = first + n_tiles - 1

    @pl.when(n_tiles >= 2)
    def _():
        store(last - 1, (last - 1) % 2).wait()

    @pl.when(n_tiles >= 1)
    def _():
        store(last, last % 2).wait()


def _zero_tail_tiles(first_tile, n_tiles, zero_src, tile_copy):
    zero_src[...] = _zeros_like_ref(zero_src)

    def start(t, carry):
        tile_copy(t).start()
        return carry

    def wait(t, carry):
        tile_copy(t).wait()
        return carry

    lax.fori_loop(first_tile, n_tiles, start, 0)
    lax.fori_loop(first_tile, n_tiles, wait, 0)


def _moe_up_kernel(t0_ref, nt_ref, nu_ref, wg_ref, wu_ref, xs_hbm, a_hbm, wgb, wub, xbuf, obuf, xsem, osem):
    j = pl.program_id(0)
    e = pl.program_id(1)
    tm = xbuf.shape[1]

    def load(g, slot):
        return pltpu.make_async_copy(xs_hbm.at[pl.ds(g * tm, tm)], xbuf.at[slot], xsem.at[slot])

    def store(g, slot):
        return pltpu.make_async_copy(obuf.at[slot], a_hbm.at[j, pl.ds(g * tm, tm)], osem.at[slot])

    def compute(slot, out_slot):
        half = xbuf.shape[2]
        x_lo, x_hi = (h.astype(BF16) for h in _unpack_halves(xbuf[slot]))

        def project(w):
            return (jnp.dot(x_lo, w[0:half, :], preferred_element_type=F32)
                    + jnp.dot(x_hi, w[half:2 * half, :], preferred_element_type=F32))

        gate = project(wgb)
        up = project(wub)
        obuf[out_slot] = ((gate * jax.nn.sigmoid(gate)) * up).astype(obuf.dtype)

    @pl.when(nt_ref[e] > 0)
    def _():
        wgb[...] = wg_ref[...].astype(BF16)
        wub[...] = wu_ref[...].astype(BF16)

    _segment_loop(t0_ref[e], nt_ref[e], nu_ref[0], e == 0, load, compute, store)

    @pl.when(e == pl.num_programs(1) - 1)
    def _():
        _zero_tail_tiles(
            nu_ref[0], a_hbm.shape[1] // tm, obuf.at[0],
            lambda t: pltpu.make_async_copy(obuf.at[0], a_hbm.at[j, pl.ds(t * tm, tm)], osem.at[0]))


def _moe_up(xs, tables, wg4, wu4, layer, tm):
    n_rows = xs.shape[0]
    n_experts, d, f = wg4.shape[1:]
    fs = f // MOE_UP_SPLITS
    w_spec = pl.BlockSpec((None, None, d, fs), lambda j, e, t0, nt, nu: (layer, e, 0, j))
    return pl.pallas_call(
        _moe_up_kernel,
        grid_spec=pltpu.PrefetchScalarGridSpec(
            num_scalar_prefetch=3,
            grid=(MOE_UP_SPLITS, n_experts),
            in_specs=[w_spec, w_spec, pl.BlockSpec(memory_space=pl.ANY)],
            out_specs=pl.BlockSpec(memory_space=pl.ANY),
            scratch_shapes=[
                pltpu.VMEM((d, fs), BF16), pltpu.VMEM((d, fs), BF16),
                pltpu.VMEM((N_TILE_SLOTS, tm, xs.shape[1]), xs.dtype), pltpu.VMEM((2, tm, fs), BF16),
                pltpu.SemaphoreType.DMA((N_TILE_SLOTS,)), pltpu.SemaphoreType.DMA((2,)),
            ],
        ),
        out_shape=jax.ShapeDtypeStruct((MOE_UP_SPLITS, n_rows, fs), BF16),
        compiler_params=_params(2),
        name="moe_up",
    )(tables["seg_tile0"], tables["seg_tiles"], tables["n_used"], wg4, wu4, xs)


def _moe_down_kernel(t0_ref, nt_ref, nu_ref, wd_ref, a_hbm, y_hbm, wdb, abuf, ybuf, asem, ysem):
    e = pl.program_id(0)
    n_splits, tm, fs = abuf.shape[1:]

    def load(g, slot):
        return pltpu.make_async_copy(a_hbm.at[:, pl.ds(g * tm, tm)], abuf.at[slot], asem.at[slot])

    def store(g, slot):
        return pltpu.make_async_copy(ybuf.at[slot], y_hbm.at[pl.ds(g * tm, tm)], ysem.at[slot])

    def compute(slot, out_slot):
        acc = jnp.dot(abuf[slot, 0], wdb[0:fs, :], preferred_element_type=F32)
        for s in range(1, n_splits):
            acc = acc + jnp.dot(abuf[slot, s], wdb[s * fs:(s + 1) * fs, :], preferred_element_type=F32)
        ybuf[out_slot] = _pack_halves(acc)

    @pl.when(nt_ref[e] > 0)
    def _():
        wdb[...] = wd_ref[...].astype(BF16)

    _segment_loop(t0_ref[e], nt_ref[e], nu_ref[0], e == 0, load, compute, store)

    @pl.when(e == pl.num_programs(0) - 1)
    def _():
        _zero_tail_tiles(
            nu_ref[0], y_hbm.shape[0] // tm, ybuf.at[0],
            lambda t: pltpu.make_async_copy(ybuf.at[0], y_hbm.at[pl.ds(t * tm, tm)], ysem.at[0]))


def _moe_down(a, tables, wd4, layer, tm):
    n_splits, n_rows, fs = a.shape
    n_experts, f, d = wd4.shape[1:]
    return pl.pallas_call(
        _moe_down_kernel,
        grid_spec=pltpu.PrefetchScalarGridSpec(
            num_scalar_prefetch=3,
            grid=(n_experts,),
            in_specs=[
                pl.BlockSpec((None, None, f, d), lambda e, t0, nt, nu: (layer, e, 0, 0)),
                pl.BlockSpec(memory_space=pl.ANY),
            ],
            out_specs=pl.BlockSpec(memory_space=pl.ANY),
            scratch_shapes=[
                pltpu.VMEM((f, d), BF16),
                pltpu.VMEM((N_TILE_SLOTS, n_splits, tm, fs), BF16), pltpu.VMEM((2, tm, d // 2), jnp.uint32),
                pltpu.SemaphoreType.DMA((N_TILE_SLOTS,)), pltpu.SemaphoreType.DMA((2,)),
            ],
        ),
        out_shape=jax.ShapeDtypeStruct((n_rows, d // 2), jnp.uint32),
        compiler_params=_params(1),
        name="moe_down",
    )(tables["seg_tile0"], tables["seg_tiles"], tables["n_used"], wd4, a)


def _combine_kernel(pos_ref, x_ref, w_ref, g_ref, fg_ref, y_hbm, o_ref, ybuf, sem, *, final_norm):
    i = pl.program_id(0)
    n = pl.num_programs(0)
    tm = x_ref.shape[0]

    def gather(tile, slot):
        def body(r, carry):
            for k in range(TOP_K):
                row = pos_ref[(tile * tm + r) * TOP_K + k]
                pltpu.make_async_copy(
                    y_hbm.at[pl.ds(row, 1)], ybuf.at[slot, k, pl.ds(r, 1)], sem.at[slot]).start(priority=k)
            return carry
        lax.fori_loop(0, tm, body, 0, unroll=ROW_DMA_UNROLL)

    @pl.when(i == 0)
    def _():
        gather(0, 0)

    @pl.when(i + 1 < n)
    def _():
        gather(i + 1, (i + 1) % 2)

    slot = i % 2
    for k in range(TOP_K):
        pltpu.make_async_copy(y_hbm.at[pl.ds(0, tm)], ybuf.at[slot, k], sem.at[slot]).wait()

    def rows_block(c, carry):
        rows = pl.ds(pl.multiple_of(c * COMBINE_ROWS, COMBINE_ROWS), COMBINE_ROWS)
        w = w_ref[rows, :]
        y_first = jnp.concatenate(_unpack_halves(ybuf[slot, 0, rows, :]), axis=1)
        y_second = jnp.concatenate(_unpack_halves(ybuf[slot, 1, rows, :]), axis=1)
        moe = w[:, 0:1] * y_first + w[:, 1:2] * y_second
        out = x_ref[rows, :] + g_ref[...] * moe
        if final_norm:
            out = (out * lax.rsqrt(jnp.mean(out * out, axis=-1, keepdims=True) + EPS)) * fg_ref[...]
        o_ref[rows, :] = out
        return carry

    lax.fori_loop(0, tm // COMBINE_ROWS, rows_block, 0, unroll=4)


def _combine(x_mix, route_w, pos, y, mod3, final_g, n_rows, n_lat_rows, seq, n_batch, final_norm):
    d = x_mix.shape[1]
    tm = ROW_TILE
    lanes = route_w.shape[1]
    mrow = functools.partial(_mod_row, tile=tm, n_lat_rows=n_lat_rows, seq=seq, n_batch=n_batch)
    return pl.pallas_call(
        functools.partial(_combine_kernel, final_norm=final_norm),
        grid_spec=pltpu.PrefetchScalarGridSpec(
            num_scalar_prefetch=1,
            grid=(n_rows // tm,),
            in_specs=[
                pl.BlockSpec((tm, d), lambda i, p: (i, 0)),
                pl.BlockSpec((tm, lanes), lambda i, p: (i, 0)),
                pl.BlockSpec((None, 1, d), lambda i, p: (mrow(i), 0, 5)),
                pl.BlockSpec((1, d), lambda i, p: (0, 0)),
                pl.BlockSpec(memory_space=pl.ANY),
            ],
            out_specs=pl.BlockSpec((tm, d), lambda i, p: (i, 0)),
            scratch_shapes=[pltpu.VMEM((2, TOP_K, tm, y.shape[1]), y.dtype), pltpu.SemaphoreType.DMA((2,))],
        ),
        out_shape=jax.ShapeDtypeStruct((n_rows, d), F32),
        compiler_params=_params(1),
        name="moe_combine",
    )(pos, x_mix, route_w, mod3, final_g.reshape(1, d), y)


def kernel(x, c, ctx, c_ctx, w_mod, b_mod, norm1_g, norm2_g, w_in, b_gates, mlstm_norm_g, gmlp_ws, gmlp_bs,
           gmlp_norm_g, conv_w, w_out, w_router, b_router, w_gate_e, w_up_e, w_down_e, final_g):
    n_batch, seq, d = x.shape
    ctx_len = ctx.shape[1]
    depth = w_mod.shape[0]
    wm = mlstm_norm_g.shape[1]
    n_experts = w_router.shape[1]
    n_lat_rows = n_batch * seq
    n_all_rows = n_lat_rows + n_batch * ctx_len
    n_gate_cols = N_DIRS * 2 * MLSTM_HEADS
    gate_col0 = 3 * wm
    assert seq % ROW_TILE == 0 and seq % MM_ROW_TILE == 0 and (n_batch * ctx_len) % ROW_TILE == 0
    assert n_batch + 1 <= V7X_SUBLANES and n_experts <= V7X_LANES and n_gate_cols <= V7X_LANES

    src = (x.reshape(n_lat_rows, d), ctx.reshape(n_batch * ctx_len, d), 0)
    n_in_cols = w_in.shape[2]
    w_in_t = jnp.swapaxes(w_in, 1, 2).reshape(depth * n_in_cols, d)

    cond = jnp.zeros((V7X_SUBLANES, d), F32).at[:n_batch].set(c).at[n_batch].set(c_ctx)
    mods = _mod_table(cond, w_mod, b_mod)

    wr = jnp.zeros((d, V7X_LANES), F32).at[:, :n_experts].set(w_router)
    wr_hi = wr.astype(BF16)
    wr2 = jnp.concatenate([wr_hi, (wr - wr_hi.astype(F32)).astype(BF16)], axis=1)
    br = jnp.zeros((V7X_LANES, 1), F32).at[:n_experts, 0].set(b_router)

    for layer in range(depth):
        last = layer + 1 == depth
        mod3 = mods[layer].reshape(V7X_SUBLANES, 1, N_MOD * d)
        w_row0 = layer * n_in_cols
        w_gate_t = jnp.zeros((V7X_LANES, d), F32).at[:n_gate_cols].set(
            w_in_t[w_row0 + gate_col0:w_row0 + gate_col0 + n_gate_cols])
        bias = jnp.zeros((1, V7X_LANES), F32).at[0, :n_gate_cols].set(b_gates[layer].reshape(-1))
        bias_t = jnp.broadcast_to(b_gates[layer].reshape(-1, 1), (n_gate_cols, MLSTM_CHUNK))

        h1, gates, gates_t = _normmod(src, n_all_rows, norm1_g[layer], mod3, 0, w_gate_t.T, w_gate_t,
                                      n_lat_rows, seq, n_batch)
        z_a = _matmul_wt(h1, w_in_t, w_row0, gate_col0, BF16)
        z_b = _matmul_wt(h1, w_in_t, w_row0 + gate_col0 + n_gate_cols, n_in_cols - gate_col0 - n_gate_cols,
                         BF16)
        hf, hb = _mlstm(z_a, gates, gates_t, bias, bias_t, n_batch, seq, ctx_len, wm)

        n_rows = n_lat_rows if last else n_all_rows
        y = _mix(hf, hb, z_a, z_b, mlstm_norm_g[layer].reshape(1, wm), gmlp_ws[layer].astype(BF16),
                 gmlp_bs[layer].T, gmlp_norm_g[layer].reshape(1, -1), conv_w[layer], n_rows, n_lat_rows, ctx_len)
        x_mix = _wout(y, w_out, layer, src, mod3, n_lat_rows, seq, n_batch)

        h2, route_w, route_e = _router(x_mix, norm2_g[layer], mod3, wr2, br, n_experts, n_lat_rows, seq, n_batch)
        tables = _dispatch_tables(route_e[:, :TOP_K], n_experts, MOE_ROW_TILE)
        xs = _dispatch(h2, tables, n_experts)
        act = _moe_up(xs, tables, w_gate_e, w_up_e, layer, MOE_ROW_TILE)
        y_moe = _moe_down(act, tables, w_down_e, layer, MOE_ROW_TILE)
        rows = _combine(x_mix, route_w, tables["dest"], y_moe, mod3, final_g, n_rows, n_lat_rows, seq, n_batch, last)
        src = (rows, rows, n_lat_rows)

    return rows.reshape(n_batch, seq, d)
```

```python
import functools

import jax
import jax.numpy as jnp
from jax import lax
from jax.experimental import pallas as pl
from jax.experimental.pallas import tpu as pltpu

F32 = jnp.float32
BF16 = jnp.bfloat16

N_MOD = 6
EPS = 1e-6
N_DIRS = 2
MLSTM_HEADS = 4
MLSTM_CHUNK = 128
GMLP_HEADS = 4
GMLP_CHUNK = 128
GRID_W = 64
N_EXPERT_GROUPS = 4
TOP_K = 2

V7X_LANES = 128
V7X_SUBLANES = 8
V7X_VMEM_BYTES = 64 * 1024 * 1024
VMEM_LIMIT = V7X_VMEM_BYTES - 8 * 1024 * 1024

ROW_TILE = 256
MM_ROW_TILE = 512
MOE_ROW_TILE = 256
DISPATCH_ROWS = 512
MOE_UP_SPLITS = 2
TILE_DMA_PRIORITY = 1
TILE_LOOKAHEAD = 2
N_TILE_SLOTS = TILE_LOOKAHEAD + 1
COMBINE_ROWS = 2 * V7X_SUBLANES
ROW_DMA_UNROLL = 8


def _params(n_axes):
    return pltpu.CompilerParams(
        dimension_semantics=("arbitrary",) * n_axes, vmem_limit_bytes=VMEM_LIMIT)


def _largest_divisor(n, candidates):
    for c in candidates:
        if n % c == 0:
            return c
    raise ValueError(f"no tile in {candidates} divides {n}")


def _mod_row(i, tile, n_lat_rows, seq, n_batch):
    return jnp.where(i * tile < n_lat_rows, (i * tile) // seq, n_batch)


def _mod_kernel(a_ref, w_ref, b_ref, o_ref):
    a = a_ref[...]
    a = a * jax.nn.sigmoid(a)
    acc = jnp.dot(a.astype(BF16), w_ref[...].astype(BF16), preferred_element_type=F32)
    o_ref[...] = acc + b_ref[...]


def _mod_table(a, w_mod, b_mod):
    depth, d, n = w_mod.shape
    tn = _largest_divisor(n, (1024, 512, 256, 128))
    rows = a.shape[0]
    return pl.pallas_call(
        _mod_kernel,
        grid=(depth, n // tn),
        in_specs=[
            pl.BlockSpec((rows, d), lambda l, j: (0, 0)),
            pl.BlockSpec((None, d, tn), lambda l, j: (l, 0, j)),
            pl.BlockSpec((None, 1, tn), lambda l, j: (l, 0, j)),
        ],
        out_specs=pl.BlockSpec((None, rows, tn), lambda l, j: (l, 0, j)),
        out_shape=jax.ShapeDtypeStruct((depth, rows, n), F32),
        compiler_params=_params(2),
        name="mod_table",
    )(a, w_mod, b_mod.reshape(depth, 1, n))


def _norm_modulate(x, g, shift, scale):
    y = x * lax.rsqrt(jnp.mean(x * x, axis=-1, keepdims=True) + EPS)
    return (y * g) * (1.0 + scale) + shift


def _pack_halves(x):
    half = x.shape[1] // 2
    return pltpu.pack_elementwise([x[:, :half], x[:, half:]], packed_dtype=BF16)


def _unpack_halves(p):
    return tuple(pltpu.unpack_elementwise(p, index=k, packed_dtype=BF16, unpacked_dtype=F32) for k in range(2))


def _zeros_like_ref(ref):
    if ref.dtype == jnp.uint32:
        return _pack_halves(jnp.zeros(ref.shape[:-1] + (2 * ref.shape[-1],), F32))
    return jnp.zeros(ref.shape, ref.dtype)


def _row_specs(src, tm, tn, n_lat_rows, col):
    lat, ctx, ctx_row0 = src
    n_lat_tiles = n_lat_rows // tm
    assert n_lat_rows % tm == 0 and ctx_row0 % tm == 0
    return [
        pl.BlockSpec((tm, tn), lambda *g: (jnp.minimum(g[-1], n_lat_tiles - 1), col(*g))),
        pl.BlockSpec((tm, tn), lambda *g: (ctx_row0 // tm + jnp.maximum(g[-1] - n_lat_tiles, 0), col(*g))),
    ]


def _normmod_kernel(xl_ref, xc_ref, g_ref, sh_ref, sc_ref, wg_ref, wgt_ref, o_ref, gate_ref, gate_t_ref, *,
                    n_lat_tiles):
    x = jnp.where(pl.program_id(0) < n_lat_tiles, xl_ref[...], xc_ref[...])
    h = _norm_modulate(x, g_ref[...], sh_ref[...], sc_ref[...]).astype(o_ref.dtype)
    o_ref[...] = h
    gate_ref[...] = jnp.dot(h, wg_ref[...].astype(BF16), preferred_element_type=F32)
    gate_t_ref[...] = lax.dot_general(wgt_ref[...].astype(BF16), h, (((1,), (1,)), ((), ())),
                                      preferred_element_type=F32)


def _normmod(src, n_rows, g, mod3, shift_col, w_gate, w_gate_t, n_lat_rows, seq, n_batch):
    d = g.shape[0]
    lanes = w_gate.shape[1]
    tm = ROW_TILE
    mrow = functools.partial(_mod_row, tile=tm, n_lat_rows=n_lat_rows, seq=seq, n_batch=n_batch)
    return pl.pallas_call(
        functools.partial(_normmod_kernel, n_lat_tiles=n_lat_rows // tm),
        grid=(n_rows // tm,),
        in_specs=_row_specs(src, tm, d, n_lat_rows, lambda i: 0) + [
            pl.BlockSpec((1, d), lambda i: (0, 0)),
            pl.BlockSpec((None, 1, d), lambda i: (mrow(i), 0, shift_col)),
            pl.BlockSpec((None, 1, d), lambda i: (mrow(i), 0, shift_col + 1)),
            pl.BlockSpec((d, lanes), lambda i: (0, 0)),
            pl.BlockSpec((lanes, d), lambda i: (0, 0)),
        ],
        out_specs=[
            pl.BlockSpec((tm, d), lambda i: (i, 0)),
            pl.BlockSpec((tm, lanes), lambda i: (i, 0)),
            pl.BlockSpec((lanes, tm), lambda i: (0, i)),
        ],
        out_shape=[
            jax.ShapeDtypeStruct((n_rows, d), BF16),
            jax.ShapeDtypeStruct((n_rows, lanes), F32),
            jax.ShapeDtypeStruct((lanes, n_rows), F32),
        ],
        compiler_params=_params(1),
        name="norm_modulate",
    )(src[0], src[1], g.reshape(1, d), mod3, mod3, w_gate, w_gate_t)


def _mm_wt_kernel(a_ref, w_ref, o_ref, wb_scr):
    @pl.when(pl.program_id(1) == 0)
    def _():
        wb_scr[...] = w_ref[...].astype(BF16)

    acc = lax.dot_general(a_ref[...], wb_scr[...], (((1,), (1,)), ((), ())), preferred_element_type=F32)
    o_ref[...] = acc.astype(o_ref.dtype)


def _matmul_wt(a, wt, row0, n_cols, out_dtype):
    m, k = a.shape
    tm = _largest_divisor(m, (MM_ROW_TILE, 256, 128))
    tn = _largest_divisor(n_cols, (1024, 512, 256, 128))
    return pl.pallas_call(
        _mm_wt_kernel,
        grid=(n_cols // tn, m // tm),
        in_specs=[
            pl.BlockSpec((tm, k), lambda j, i: (i, 0)),
            pl.BlockSpec((pl.Element(tn), pl.Element(k)),
                         lambda j, i: (pl.multiple_of(row0 + j * tn, V7X_SUBLANES), 0)),
        ],
        out_specs=pl.BlockSpec((tm, tn), lambda j, i: (i, j)),
        out_shape=jax.ShapeDtypeStruct((m, n_cols), out_dtype),
        scratch_shapes=[pltpu.VMEM((tn, k), BF16)],
        compiler_params=_params(2),
        name="projection_wt",
    )(a, wt)


def _log_sigmoid(x):
    return jnp.minimum(x, 0.0) - jnp.log1p(jnp.exp(-jnp.abs(x)))


def _mlstm_chain(q, k, v, i_col, f_col, i_row, f_row, before, after, c_ref, m_ref, o_ref, cols):
    chunk, dk = q.shape
    dv = v.shape[1]
    lf_col = _log_sigmoid(f_col)
    lf_row = _log_sigmoid(f_row)
    b_col = jnp.sum(jnp.where(before, lf_row, 0.0), axis=1, keepdims=True)
    b_row = jnp.sum(jnp.where(after, lf_col, 0.0), axis=0, keepdims=True)
    b_last = jnp.sum(lf_row, axis=1, keepdims=True)

    m_prev = m_ref[0:1, 0:1]
    dmat = jnp.where(before, b_col - b_row + i_row, -jnp.inf)
    m_inter = b_col + m_prev
    m_t = jnp.maximum(jnp.max(dmat, axis=1, keepdims=True), m_inter)
    scale = dk ** -0.5
    qk = lax.dot_general(q, k, (((1,), (1,)), ((), ())), preferred_element_type=F32)
    s = (qk * (scale * jnp.exp(dmat - m_t))).astype(BF16)
    v_aug = jnp.concatenate([v, jnp.ones((chunk, V7X_LANES), BF16)], axis=1)
    c_prev = c_ref[...]
    inter = scale * jnp.exp(m_inter - m_t)
    num = (jnp.dot(s, v_aug, preferred_element_type=F32)
           + inter * jnp.dot(q, c_prev.astype(BF16), preferred_element_type=F32))
    den = num[:, dv:dv + 1]
    o_ref[:, cols] = (num[:, :dv] / jnp.maximum(jnp.abs(den), jnp.exp(-m_t))).astype(o_ref.dtype)

    g_end = b_last - b_col + i_col
    m_new = jnp.maximum(b_last + m_prev, jnp.max(g_end, axis=0, keepdims=True))
    decay = jnp.exp(b_last + m_prev - m_new)
    wv = (jnp.exp(g_end - m_new) * v_aug.astype(F32)).astype(BF16)
    c_ref[...] = decay * c_prev + lax.dot_general(k, wv, (((0,), (0,)), ((), ())), preferred_element_type=F32)
    m_ref[...] = jnp.broadcast_to(m_new, m_ref.shape)


def _mlstm_kernel(qf_ref, kf_ref, vf_ref, gf_ref, gtf_ref, qb_ref, kb_ref, vb_ref, gb_ref, gtb_ref,
                  b_ref, bt_ref, of_ref, ob_ref, c_scr, m_scr):
    nh = MLSTM_HEADS
    chunk = qf_ref.shape[0]
    dk = qf_ref.shape[1] // nh
    dv = vf_ref.shape[1] // nh

    @pl.when(pl.program_id(1) == 0)
    def _():
        c_scr[...] = jnp.zeros_like(c_scr)
        m_scr[...] = jnp.zeros_like(m_scr)

    t_idx = lax.broadcasted_iota(jnp.int32, (chunk, chunk), 0)
    s_idx = lax.broadcasted_iota(jnp.int32, (chunk, chunk), 1)
    dirs = ((qf_ref, kf_ref, vf_ref, gf_ref, gtf_ref, of_ref), (qb_ref, kb_ref, vb_ref, gb_ref, gtb_ref, ob_ref))
    for d, (q_ref, k_ref, v_ref, g_ref, gt_ref, o_ref) in enumerate(dirs):
        before = s_idx <= t_idx if d == 0 else s_idx >= t_idx
        after = s_idx >= t_idx if d == 0 else s_idx <= t_idx
        g = g_ref[...] + b_ref[...]
        gt = gt_ref[...] + bt_ref[...]
        for h in range(nh):
            ci = d * 2 * nh + h
            cf = ci + nh
            _mlstm_chain(
                q_ref[:, h * dk:(h + 1) * dk], k_ref[:, h * dk:(h + 1) * dk], v_ref[:, h * dv:(h + 1) * dv],
                g[:, ci:ci + 1], g[:, cf:cf + 1], gt[ci:ci + 1, :], gt[cf:cf + 1, :], before, after,
                c_scr.at[d * nh + h], m_scr.at[d * nh + h], o_ref, slice(h * dv, (h + 1) * dv))


def _mlstm(z, gates, gates_t, bias, bias_t, n_batch, seq, ctx_len, width):
    r = z.shape[0]
    nh = MLSTM_HEADS
    dv = width // nh
    dk = dv // 2
    qk_w = nh * dk
    chunk = MLSTM_CHUNK
    n_ctx = ctx_len // chunk
    n_lat = seq // chunk
    n_gate_rows = bias_t.shape[0]

    def tile(d, b, step):
        lat = step - n_ctx
        ctx_tile = n_batch * n_lat + b * n_ctx + (step if d == 0 else n_ctx - 1 - step)
        lat_tile = b * n_lat + (lat if d == 0 else n_lat - 1 - lat)
        return jnp.where(step < n_ctx, ctx_tile, lat_tile)

    def dir_specs(d):
        return [
            pl.BlockSpec((chunk, qk_w), lambda b, s: (tile(d, b, s), 0)),
            pl.BlockSpec((chunk, qk_w), lambda b, s: (tile(d, b, s), 1)),
            pl.BlockSpec((chunk, width), lambda b, s: (tile(d, b, s), 1)),
            pl.BlockSpec((chunk, gates.shape[1]), lambda b, s: (tile(d, b, s), 0)),
            pl.BlockSpec((n_gate_rows, chunk), lambda b, s: (0, tile(d, b, s))),
        ]

    return pl.pallas_call(
        _mlstm_kernel,
        grid=(n_batch, n_ctx + n_lat),
        in_specs=dir_specs(0) + dir_specs(1) + [
            pl.BlockSpec(bias.shape, lambda b, s: (0, 0)),
            pl.BlockSpec(bias_t.shape, lambda b, s: (0, 0)),
        ],
        out_specs=[
            pl.BlockSpec((chunk, width), lambda b, s: (tile(0, b, s), 0)),
            pl.BlockSpec((chunk, width), lambda b, s: (tile(1, b, s), 0)),
        ],
        out_shape=[jax.ShapeDtypeStruct((r, width), BF16), jax.ShapeDtypeStruct((r, width), BF16)],
        scratch_shapes=[
            pltpu.VMEM((N_DIRS * nh, dk, dv + V7X_LANES), F32),
            pltpu.VMEM((N_DIRS * nh, V7X_SUBLANES, V7X_LANES), F32),
        ],
        compiler_params=_params(2),
        name="mlstm_scan",
    )(z, z, z, gates, gates_t, z, z, z, gates, gates_t, bias, bias_t)


def _mix_kernel(hf_ref, hb_ref, o_ref, u_ref, v_ref, bg_ref, cg_ref, xin_ref, mg_ref, ws_ref, bst_ref, gg_ref,
                cw_ref, y_ref, *, n_lat_rows, ctx_len):
    i = pl.program_id(0)
    tm = y_ref.shape[0]
    wm = o_ref.shape[1]
    gw = u_ref.shape[1]
    dv = wm // MLSTM_HEADS
    hdim = gw // GMLP_HEADS

    for hd in range(MLSTM_HEADS):
        cols = slice(hd * dv, (hd + 1) * dv)
        hh = hf_ref[:, cols].astype(F32) + hb_ref[:, cols].astype(F32)
        hh = hh * lax.rsqrt(jnp.mean(hh * hh, axis=-1, keepdims=True) + EPS)
        out = (hh * mg_ref[:, cols]) * jax.nn.sigmoid(o_ref[:, cols].astype(F32))
        y_ref[:, cols] = out.astype(y_ref.dtype)

    u = jax.nn.gelu(u_ref[...].astype(F32))
    v = jax.nn.gelu(v_ref[...].astype(F32))
    v = (v * lax.rsqrt(jnp.mean(v * v, axis=-1, keepdims=True) + EPS)) * gg_ref[...]
    vb = v.astype(BF16)
    for c in range(tm // GMLP_CHUNK):
        rows = slice(c * GMLP_CHUNK, (c + 1) * GMLP_CHUNK)
        for hd in range(GMLP_HEADS):
            cols = slice(hd * hdim, (hd + 1) * hdim)
            sp = jnp.dot(ws_ref[hd], vb[rows, cols], preferred_element_type=F32) + bst_ref[:, hd:hd + 1]
            y_ref[rows, wm + hd * hdim:wm + (hd + 1) * hdim] = (u[rows, cols] * sp).astype(y_ref.dtype)

    yv = cg_ref[...].astype(F32) * xin_ref[...].astype(F32)
    t = lax.broadcasted_iota(jnp.int32, (tm, 1), 0)
    is_lat = i * tm < n_lat_rows
    pos = jnp.where(is_lat, t % GRID_W, t % ctx_len)
    last = jnp.where(is_lat, GRID_W - 1, ctx_len - 1)
    prev = jnp.where(pos != 0, pltpu.roll(yv, 1, axis=0), 0.0)
    nxt = jnp.where(pos != last, pltpu.roll(yv, tm - 1, axis=0), 0.0)
    conv = cw_ref[0:1, :] * prev + cw_ref[1:2, :] * yv + cw_ref[2:3, :] * nxt
    y_ref[:, wm + gw:] = (bg_ref[...].astype(F32) * conv).astype(y_ref.dtype)


def _mix(hf, hb, z_a, z_b, mg, ws, bst, gg, cw, n_rows, n_lat_rows, ctx_len):
    wm = mg.shape[1]
    gw = gg.shape[1]
    cwid = cw.shape[1]
    tm = ROW_TILE
    assert tm % GRID_W == 0 and tm % ctx_len == 0 and tm % GMLP_CHUNK == 0
    assert gw == cwid
    full = lambda i: (0, 0)
    return pl.pallas_call(
        functools.partial(_mix_kernel, n_lat_rows=n_lat_rows, ctx_len=ctx_len),
        grid=(n_rows // tm,),
        in_specs=[
            pl.BlockSpec((tm, wm), lambda i: (i, 0)),
            pl.BlockSpec((tm, wm), lambda i: (i, 0)),
            pl.BlockSpec((tm, wm), lambda i: (i, 2)),
            pl.BlockSpec((tm, gw), lambda i: (i, 0)),
            pl.BlockSpec((tm, gw), lambda i: (i, 1)),
            pl.BlockSpec((tm, gw), lambda i: (i, 2)),
            pl.BlockSpec((tm, gw), lambda i: (i, 3)),
            pl.BlockSpec((tm, gw), lambda i: (i, 4)),
            pl.BlockSpec(mg.shape, full),
            pl.BlockSpec(ws.shape, lambda i: (0, 0, 0)),
            pl.BlockSpec(bst.shape, full),
            pl.BlockSpec(gg.shape, full),
            pl.BlockSpec(cw.shape, full),
        ],
        out_specs=pl.BlockSpec((tm, wm + gw + cwid), lambda i: (i, 0)),
        out_shape=jax.ShapeDtypeStruct((n_rows, wm + gw + cwid), BF16),
        compiler_params=_params(1),
        name="mixer_epilogue",
    )(hf, hb, z_a, z_b, z_b, z_b, z_b, z_b, mg, ws, bst, gg, cw)


def _wout_kernel(y_ref, w_ref, xl_ref, xc_ref, g_ref, o_ref, wb_scr, *, n_lat_tiles):
    @pl.when(pl.program_id(1) == 0)
    def _():
        wb_scr[...] = w_ref[...].astype(BF16)

    x = jnp.where(pl.program_id(1) < n_lat_tiles, xl_ref[...], xc_ref[...])
    acc = jnp.dot(y_ref[...], wb_scr[...], preferred_element_type=F32)
    o_ref[...] = x + g_ref[...] * acc


def _wout(y, w3, layer, src, mod3, n_lat_rows, seq, n_batch):
    m, k = y.shape
    d = w3.shape[2]
    tm = _largest_divisor(m, (MM_ROW_TILE, 256))
    tn = _largest_divisor(d, (1024, 512, 256, 128))
    mrow = functools.partial(_mod_row, tile=tm, n_lat_rows=n_lat_rows, seq=seq, n_batch=n_batch)
    gate_blk0 = (2 * d) // tn
    return pl.pallas_call(
        functools.partial(_wout_kernel, n_lat_tiles=n_lat_rows // tm),
        grid=(d // tn, m // tm),
        in_specs=[
            pl.BlockSpec((tm, k), lambda j, i: (i, 0)),
            pl.BlockSpec((None, k, tn), lambda j, i: (layer, 0, j), pipeline_mode=pl.Buffered(1)),
        ] + _row_specs(src, tm, tn, n_lat_rows, lambda j, i: j) + [
            pl.BlockSpec((None, 1, tn), lambda j, i: (mrow(i), 0, gate_blk0 + j)),
        ],
        out_specs=pl.BlockSpec((tm, tn), lambda j, i: (i, j)),
        out_shape=jax.ShapeDtypeStruct((m, d), F32),
        scratch_shapes=[pltpu.VMEM((k, tn), BF16)],
        compiler_params=_params(2),
        name="out_projection",
    )(y, w3, src[0], src[1], mod3)


def _router_kernel(x_ref, g_ref, sh_ref, sc_ref, wr_ref, br_ref, h_ref, w_ref, e_ref, *, n_experts):
    h = _norm_modulate(x_ref[...], g_ref[...], sh_ref[...], sc_ref[...])
    h_ref[...] = _pack_halves(h)
    tm = h.shape[0]
    lanes = w_ref.shape[1]
    h_hi = h.astype(BF16)
    h_lo = (h - h_hi.astype(F32)).astype(BF16)
    prod = jnp.dot(jnp.concatenate([h_hi, h_lo], axis=0), wr_ref[...], preferred_element_type=F32)
    logits = (prod[:tm, :lanes] + (prod[:tm, lanes:] + prod[tm:, :lanes])) + prod[tm:, lanes:]
    scores = jax.nn.sigmoid(logits.T)
    sel = scores + br_ref[...]
    epg = n_experts // N_EXPERT_GROUPS
    sel_c = [sel[e:e + 1, :] for e in range(n_experts)]
    sc_c = [scores[e:e + 1, :] for e in range(n_experts)]

    best_g = None
    for gi in range(N_EXPERT_GROUPS):
        mem = sel_c[gi * epg:(gi + 1) * epg]
        gs = None
        for a in range(epg):
            for b in range(a + 1, epg):
                pair = mem[a] + mem[b]
                gs = pair if gs is None else jnp.maximum(gs, pair)
        if best_g is None:
            best_g, gidx = gs, jnp.zeros_like(gs, dtype=jnp.int32)
        else:
            upd = gs > best_g
            gidx = jnp.where(upd, gi, gidx)
            best_g = jnp.where(upd, gs, best_g)

    in_sel, in_sc = [], []
    for a in range(epg):
        vs, vr = sel_c[a], sc_c[a]
        for gi in range(1, N_EXPERT_GROUPS):
            vs = jnp.where(gidx == gi, sel_c[gi * epg + a], vs)
            vr = jnp.where(gidx == gi, sc_c[gi * epg + a], vr)
        in_sel.append(vs)
        in_sc.append(vr)

    v1, i1, s1 = in_sel[0], jnp.zeros_like(gidx), in_sc[0]
    for a in range(1, epg):
        upd = in_sel[a] > v1
        i1 = jnp.where(upd, a, i1)
        s1 = jnp.where(upd, in_sc[a], s1)
        v1 = jnp.where(upd, in_sel[a], v1)
    v2 = jnp.full_like(v1, -jnp.inf)
    i2 = jnp.zeros_like(gidx)
    s2 = jnp.zeros_like(s1)
    for a in range(epg):
        upd = jnp.logical_and(i1 != a, in_sel[a] > v2)
        i2 = jnp.where(upd, a, i2)
        s2 = jnp.where(upd, in_sc[a], s2)
        v2 = jnp.where(upd, in_sel[a], v2)
    denom = s1 + s2
    sub = lax.broadcasted_iota(jnp.int32, scores.shape, 0)
    w_t = jnp.where(sub == 0, s1 / denom, jnp.where(sub == 1, s2 / denom, 0.0))
    e_t = jnp.where(sub == 0, gidx * epg + i1, jnp.where(sub == 1, gidx * epg + i2, 0))
    w_ref[...] = w_t.T
    e_ref[...] = e_t.astype(F32).T.astype(jnp.int32)


def _router(x_mix, g, mod3, wr2, br, n_experts, n_lat_rows, seq, n_batch):
    r, d = x_mix.shape
    tm = ROW_TILE
    mrow = functools.partial(_mod_row, tile=tm, n_lat_rows=n_lat_rows, seq=seq, n_batch=n_batch)
    lanes = br.shape[0]
    return pl.pallas_call(
        functools.partial(_router_kernel, n_experts=n_experts),
        grid=(r // tm,),
        in_specs=[
            pl.BlockSpec((tm, d), lambda i: (i, 0)),
            pl.BlockSpec((1, d), lambda i: (0, 0)),
            pl.BlockSpec((None, 1, d), lambda i: (mrow(i), 0, 3)),
            pl.BlockSpec((None, 1, d), lambda i: (mrow(i), 0, 4)),
            pl.BlockSpec(wr2.shape, lambda i: (0, 0)),
            pl.BlockSpec(br.shape, lambda i: (0, 0)),
        ],
        out_specs=[
            pl.BlockSpec((tm, d // 2), lambda i: (i, 0)),
            pl.BlockSpec((tm, lanes), lambda i: (i, 0)),
            pl.BlockSpec((tm, lanes), lambda i: (i, 0)),
        ],
        out_shape=[
            jax.ShapeDtypeStruct((r, d // 2), jnp.uint32),
            jax.ShapeDtypeStruct((r, lanes), F32),
            jax.ShapeDtypeStruct((r, lanes), jnp.int32),
        ],
        compiler_params=_params(1),
        name="moe_router",
    )(x_mix, g.reshape(1, d), mod3, mod3, wr2, br)


def _dispatch_tables(e_pair, n_experts, tm):
    flat_e = e_pair.reshape(-1)
    n_pairs = flat_e.shape[0]
    n_rows = n_pairs + n_experts * tm
    onehot = (flat_e[:, None] == jnp.arange(n_experts, dtype=jnp.int32)[None, :]).astype(jnp.int32)
    rank = jnp.sum((jnp.cumsum(onehot, axis=0) - onehot) * onehot, axis=1)
    counts = jnp.sum(onehot, axis=0)
    padded = ((counts + tm - 1) // tm) * tm
    ends = jnp.cumsum(padded)
    starts = ends - padded
    dest = jnp.sum(onehot * starts[None, :], axis=1) + rank
    n_used = (ends[-1] // tm).astype(jnp.int32)
    return dict(dest=dest.astype(jnp.int32), pad_lo=(starts + counts).astype(jnp.int32),
                pad_hi=ends.astype(jnp.int32), seg_tile0=(starts // tm).astype(jnp.int32),
                seg_tiles=(padded // tm).astype(jnp.int32), n_used=n_used.reshape(1), n_rows=n_rows)


def _dispatch_kernel(dest_ref, plo_ref, phi_ref, h_ref, xs_hbm, zrow, sem, zsem, *, n_experts):
    i = pl.program_id(0)
    tm = h_ref.shape[0]

    @pl.when(i == 0)
    def _():
        zrow[...] = _zeros_like_ref(zrow)

        def zero_copy(r):
            return pltpu.make_async_copy(zrow.at[pl.ds(0, 1)], xs_hbm.at[pl.ds(r, 1)], zsem.at[0])

        def start(r, carry):
            zero_copy(r).start()
            return carry

        def wait(r, carry):
            zero_copy(r).wait()
            return carry

        zt = zrow.shape[0]

        def tile_copy(t):
            return pltpu.make_async_copy(zrow, xs_hbm.at[pl.ds(t * zt, zt)], zsem.at[0])

        def start_tile(t, carry):
            tile_copy(t).start()
            return carry

        def wait_tile(t, carry):
            tile_copy(t).wait()
            return carry

        n_tiles = xs_hbm.shape[0] // zt
        first_unused = phi_ref[n_experts - 1] // zt
        for e in range(n_experts):
            lax.fori_loop(plo_ref[e], phi_ref[e], start, 0)
        lax.fori_loop(first_unused, n_tiles, start_tile, 0)
        for e in range(n_experts):
            lax.fori_loop(plo_ref[e], phi_ref[e], wait, 0)
        lax.fori_loop(first_unused, n_tiles, wait_tile, 0)

    def scatter(r, carry):
        for k in range(TOP_K):
            row = dest_ref[(i * tm + r) * TOP_K + k]
            pltpu.make_async_copy(h_ref.at[pl.ds(r, 1)], xs_hbm.at[pl.ds(row, 1)], sem.at[0]).start(priority=k)
        return carry

    lax.fori_loop(0, tm, scatter, 0, unroll=ROW_DMA_UNROLL)
    for k in range(TOP_K):
        pltpu.make_async_copy(h_ref, xs_hbm.at[pl.ds(0, tm)], sem.at[0]).wait()


def _dispatch(h2, tables, n_experts):
    r, d = h2.shape
    tm = _largest_divisor(r, (DISPATCH_ROWS, ROW_TILE))
    return pl.pallas_call(
        functools.partial(_dispatch_kernel, n_experts=n_experts),
        grid_spec=pltpu.PrefetchScalarGridSpec(
            num_scalar_prefetch=3,
            grid=(r // tm,),
            in_specs=[pl.BlockSpec((tm, d), lambda i, de, lo, hi: (i, 0))],
            out_specs=pl.BlockSpec(memory_space=pl.ANY),
            scratch_shapes=[pltpu.VMEM((MOE_ROW_TILE, d), h2.dtype), pltpu.SemaphoreType.DMA((1,)),
                            pltpu.SemaphoreType.DMA((1,))],
        ),
        out_shape=jax.ShapeDtypeStruct((tables["n_rows"], d), h2.dtype),
        compiler_params=_params(1),
        name="moe_dispatch",
    )(tables["dest"], tables["pad_lo"], tables["pad_hi"], h2)


def _segment_loop(first, n_tiles, n_stream, restart, load, compute, store):
    @pl.when(restart)
    def _():
        for a in range(TILE_LOOKAHEAD):
            @pl.when(a < n_stream)
            def _():
                load(a, a).start(priority=TILE_DMA_PRIORITY)

    def body(g, carry):
        slot = g % N_TILE_SLOTS
        load(g, slot).wait()
        ahead = g + TILE_LOOKAHEAD

        @pl.when(ahead < n_stream)
        def _():
            load(ahead, ahead % N_TILE_SLOTS).start(priority=TILE_DMA_PRIORITY)

        out_slot = g % 2

        @pl.when(g - first >= 2)
        def _():
            store(g - 2, out_slot).wait()

        compute(slot, out_slot)
        store(g, out_slot).start(priority=TILE_DMA_PRIORITY)
        return carry

    lax.fori_loop(first, first + n_tiles, body, 0)
    last = first + n_tiles - 1

    @pl.when(n_tiles >= 2)
    def _():
        store(last - 1, (last - 1) % 2).wait()

    @pl.when(n_tiles >= 1)
    def _():
        store(last, last % 2).wait()


def _zero_tail_tiles(first_tile, n_tiles, zero_src, tile_copy):
    zero_src[...] = _zeros_like_ref(zero_src)

    def start(t, carry):
        tile_copy(t).start()
        return carry

    def wait(t, carry):
        tile_copy(t).wait()
        return carry

    lax.fori_loop(first_tile, n_tiles, start, 0)
    lax.fori_loop(first_tile, n_tiles, wait, 0)


def _moe_up_kernel(t0_ref, nt_ref, nu_ref, wg_ref, wu_ref, xs_hbm, a_hbm, wgb, wub, xbuf, obuf, xsem, osem):
    j = pl.program_id(0)
    e = pl.program_id(1)
    tm = xbuf.shape[1]

    def load(g, slot):
        return pltpu.make_async_copy(xs_hbm.at[pl.ds(g * tm, tm)], xbuf.at[slot], xsem.at[slot])

    def store(g, slot):
        return pltpu.make_async_copy(obuf.at[slot], a_hbm.at[j, pl.ds(g * tm, tm)], osem.at[slot])

    def compute(slot, out_slot):
        half = xbuf.shape[2]
        x_lo, x_hi = (h.astype(BF16) for h in _unpack_halves(xbuf[slot]))

        def project(w):
            return (jnp.dot(x_lo, w[0:half, :], preferred_element_type=F32)
                    + jnp.dot(x_hi, w[half:2 * half, :], preferred_element_type=F32))

        gate = project(wgb)
        up = project(wub)
        obuf[out_slot] = ((gate * jax.nn.sigmoid(gate)) * up).astype(obuf.dtype)

    @pl.when(nt_ref[e] > 0)
    def _():
        wgb[...] = wg_ref[...].astype(BF16)
        wub[...] = wu_ref[...].astype(BF16)

    _segment_loop(t0_ref[e], nt_ref[e], nu_ref[0], e == 0, load, compute, store)

    @pl.when(e == pl.num_programs(1) - 1)
    def _():
        _zero_tail_tiles(
            nu_ref[0], a_hbm.shape[1] // tm, obuf.at[0],
            lambda t: pltpu.make_async_copy(obuf.at[0], a_hbm.at[j, pl.ds(t * tm, tm)], osem.at[0]))


def _moe_up(xs, tables, wg4, wu4, layer, tm):
    n_rows = xs.shape[0]
    n_experts, d, f = wg4.shape[1:]
    fs = f // MOE_UP_SPLITS
    w_spec = pl.BlockSpec((None, None, d, fs), lambda j, e, t0, nt, nu: (layer, e, 0, j))
    return pl.pallas_call(
        _moe_up_kernel,
        grid_spec=pltpu.PrefetchScalarGridSpec(
            num_scalar_prefetch=3,
            grid=(MOE_UP_SPLITS, n_experts),
            in_specs=[w_spec, w_spec, pl.BlockSpec(memory_space=pl.ANY)],
            out_specs=pl.BlockSpec(memory_space=pl.ANY),
            scratch_shapes=[
                pltpu.VMEM((d, fs), BF16), pltpu.VMEM((d, fs), BF16),
                pltpu.VMEM((N_TILE_SLOTS, tm, xs.shape[1]), xs.dtype), pltpu.VMEM((2, tm, fs), BF16),
                pltpu.SemaphoreType.DMA((N_TILE_SLOTS,)), pltpu.SemaphoreType.DMA((2,)),
            ],
        ),
        out_shape=jax.ShapeDtypeStruct((MOE_UP_SPLITS, n_rows, fs), BF16),
        compiler_params=_params(2),
        name="moe_up",
    )(tables["seg_tile0"], tables["seg_tiles"], tables["n_used"], wg4, wu4, xs)


def _moe_down_kernel(t0_ref, nt_ref, nu_ref, wd_ref, a_hbm, y_hbm, wdb, abuf, ybuf, asem, ysem):
    e = pl.program_id(0)
    n_splits, tm, fs = abuf.shape[1:]

    def load(g, slot):
        return pltpu.make_async_copy(a_hbm.at[:, pl.ds(g * tm, tm)], abuf.at[slot], asem.at[slot])

    def store(g, slot):
        return pltpu.make_async_copy(ybuf.at[slot], y_hbm.at[pl.ds(g * tm, tm)], ysem.at[slot])

    def compute(slot, out_slot):
        acc = jnp.dot(abuf[slot, 0], wdb[0:fs, :], preferred_element_type=F32)
        for s in range(1, n_splits):
            acc = acc + jnp.dot(abuf[slot, s], wdb[s * fs:(s + 1) * fs, :], preferred_element_type=F32)
        ybuf[out_slot] = _pack_halves(acc)

    @pl.when(nt_ref[e] > 0)
    def _():
        wdb[...] = wd_ref[...].astype(BF16)

    _segment_loop(t0_ref[e], nt_ref[e], nu_ref[0], e == 0, load, compute, store)

    @pl.when(e == pl.num_programs(0) - 1)
    def _():
        _zero_tail_tiles(
            nu_ref[0], y_hbm.shape[0] // tm, ybuf.at[0],
            lambda t: pltpu.make_async_copy(ybuf.at[0], y_hbm.at[pl.ds(t * tm, tm)], ysem.at[0]))


def _moe_down(a, tables, wd4, layer, tm):
    n_splits, n_rows, fs = a.shape
    n_experts, f, d = wd4.shape[1:]
    return pl.pallas_call(
        _moe_down_kernel,
        grid_spec=pltpu.PrefetchScalarGridSpec(
            num_scalar_prefetch=3,
            grid=(n_experts,),
            in_specs=[
                pl.BlockSpec((None, None, f, d), lambda e, t0, nt, nu: (layer, e, 0, 0)),
                pl.BlockSpec(memory_space=pl.ANY),
            ],
            out_specs=pl.BlockSpec(memory_space=pl.ANY),
            scratch_shapes=[
                pltpu.VMEM((f, d), BF16),
                pltpu.VMEM((N_TILE_SLOTS, n_splits, tm, fs), BF16), pltpu.VMEM((2, tm, d // 2), jnp.uint32),
                pltpu.SemaphoreType.DMA((N_TILE_SLOTS,)), pltpu.SemaphoreType.DMA((2,)),
            ],
        ),
        out_shape=jax.ShapeDtypeStruct((n_rows, d // 2), jnp.uint32),
        compiler_params=_params(1),
        name="moe_down",
    )(tables["seg_tile0"], tables["seg_tiles"], tables["n_used"], wd4, a)


def _combine_kernel(pos_ref, x_ref, w_ref, g_ref, fg_ref, y_hbm, o_ref, ybuf, sem, *, final_norm):
    i = pl.program_id(0)
    n = pl.num_programs(0)
    tm = x_ref.shape[0]

    def gather(tile, slot):
        def body(r, carry):
            for k in range(TOP_K):
                row = pos_ref[(tile * tm + r) * TOP_K + k]
                pltpu.make_async_copy(
                    y_hbm.at[pl.ds(row, 1)], ybuf.at[slot, k, pl.ds(r, 1)], sem.at[slot]).start(priority=k)
            return carry
        lax.fori_loop(0, tm, body, 0, unroll=ROW_DMA_UNROLL)

    @pl.when(i == 0)
    def _():
        gather(0, 0)

    @pl.when(i + 1 < n)
    def _():
        gather(i + 1, (i + 1) % 2)

    slot = i % 2
    for k in range(TOP_K):
        pltpu.make_async_copy(y_hbm.at[pl.ds(0, tm)], ybuf.at[slot, k], sem.at[slot]).wait()

    def rows_block(c, carry):
        rows = pl.ds(pl.multiple_of(c * COMBINE_ROWS, COMBINE_ROWS), COMBINE_ROWS)
        w = w_ref[rows, :]
        y_first = jnp.concatenate(_unpack_halves(ybuf[slot, 0, rows, :]), axis=1)
        y_second = jnp.concatenate(_unpack_halves(ybuf[slot, 1, rows, :]), axis=1)
        moe = w[:, 0:1] * y_first + w[:, 1:2] * y_second
        out = x_ref[rows, :] + g_ref[...] * moe
        if final_norm:
            out = (out * lax.rsqrt(jnp.mean(out * out, axis=-1, keepdims=True) + EPS)) * fg_ref[...]
        o_ref[rows, :] = out
        return carry

    lax.fori_loop(0, tm // COMBINE_ROWS, rows_block, 0, unroll=4)


def _combine(x_mix, route_w, pos, y, mod3, final_g, n_rows, n_lat_rows, seq, n_batch, final_norm):
    d = x_mix.shape[1]
    tm = ROW_TILE
    lanes = route_w.shape[1]
    mrow = functools.partial(_mod_row, tile=tm, n_lat_rows=n_lat_rows, seq=seq, n_batch=n_batch)
    return pl.pallas_call(
        functools.partial(_combine_kernel, final_norm=final_norm),
        grid_spec=pltpu.PrefetchScalarGridSpec(
            num_scalar_prefetch=1,
            grid=(n_rows // tm,),
            in_specs=[
                pl.BlockSpec((tm, d), lambda i, p: (i, 0)),
                pl.BlockSpec((tm, lanes), lambda i, p: (i, 0)),
                pl.BlockSpec((None, 1, d), lambda i, p: (mrow(i), 0, 5)),
                pl.BlockSpec((1, d), lambda i, p: (0, 0)),
                pl.BlockSpec(memory_space=pl.ANY),
            ],
            out_specs=pl.BlockSpec((tm, d), lambda i, p: (i, 0)),
            scratch_shapes=[pltpu.VMEM((2, TOP_K, tm, y.shape[1]), y.dtype), pltpu.SemaphoreType.DMA((2,))],
        ),
        out_shape=jax.ShapeDtypeStruct((n_rows, d), F32),
        compiler_params=_params(1),
        name="moe_combine",
    )(pos, x_mix, route_w, mod3, final_g.reshape(1, d), y)


def kernel(x, c, ctx, c_ctx, w_mod, b_mod, norm1_g, norm2_g, w_in, b_gates, mlstm_norm_g, gmlp_ws, gmlp_bs,
           gmlp_norm_g, conv_w, w_out, w_router, b_router, w_gate_e, w_up_e, w_down_e, final_g):
    n_batch, seq, d = x.shape
    ctx_len = ctx.shape[1]
    depth = w_mod.shape[0]
    wm = mlstm_norm_g.shape[1]
    n_experts = w_router.shape[1]
    n_lat_rows = n_batch * seq
    n_all_rows = n_lat_rows + n_batch * ctx_len
    n_gate_cols = N_DIRS * 2 * MLSTM_HEADS
    gate_col0 = 3 * wm
    assert seq % ROW_TILE == 0 and seq % MM_ROW_TILE == 0 and (n_batch * ctx_len) % ROW_TILE == 0
    assert n_batch + 1 <= V7X_SUBLANES and n_experts <= V7X_LANES and n_gate_cols <= V7X_LANES

    src = (x.reshape(n_lat_rows, d), ctx.reshape(n_batch * ctx_len, d), 0)
    n_in_cols = w_in.shape[2]
    w_in_t = jnp.swapaxes(w_in, 1, 2).reshape(depth * n_in_cols, d)

    cond = jnp.zeros((V7X_SUBLANES, d), F32).at[:n_batch].set(c).at[n_batch].set(c_ctx)
    mods = _mod_table(cond, w_mod, b_mod)

    wr = jnp.zeros((d, V7X_LANES), F32).at[:, :n_experts].set(w_router)
    wr_hi = wr.astype(BF16)
    wr2 = jnp.concatenate([wr_hi, (wr - wr_hi.astype(F32)).astype(BF16)], axis=1)
    br = jnp.zeros((V7X_LANES, 1), F32).at[:n_experts, 0].set(b_router)

    for layer in range(depth):
        last = layer + 1 == depth
        mod3 = mods[layer].reshape(V7X_SUBLANES, 1, N_MOD * d)
        w_row0 = layer * n_in_cols
        w_gate_t = jnp.zeros((V7X_LANES, d), F32).at[:n_gate_cols].set(
            w_in_t[w_row0 + gate_col0:w_row0 + gate_col0 + n_gate_cols])
        bias = jnp.zeros((1, V7X_LANES), F32).at[0, :n_gate_cols].set(b_gates[layer].reshape(-1))
        bias_t = jnp.broadcast_to(b_gates[layer].reshape(-1, 1), (n_gate_cols, MLSTM_CHUNK))

        h1, gates, gates_t = _normmod(src, n_all_rows, norm1_g[layer], mod3, 0, w_gate_t.T, w_gate_t,
                                      n_lat_rows, seq, n_batch)
        z_a = _matmul_wt(h1, w_in_t, w_row0, gate_col0, BF16)
        z_b = _matmul_wt(h1, w_in_t, w_row0 + gate_col0 + n_gate_cols, n_in_cols - gate_col0 - n_gate_cols,
                         BF16)
        hf, hb = _mlstm(z_a, gates, gates_t, bias, bias_t, n_batch, seq, ctx_len, wm)

        n_rows = n_lat_rows if last else n_all_rows
        y = _mix(hf, hb, z_a, z_b, mlstm_norm_g[layer].reshape(1, wm), gmlp_ws[layer].astype(BF16),
                 gmlp_bs[layer].T, gmlp_norm_g[layer].reshape(1, -1), conv_w[layer], n_rows, n_lat_rows, ctx_len)
        x_mix = _wout(y, w_out, layer, src, mod3, n_lat_rows, seq, n_batch)

        h2, route_w, route_e = _router(x_mix, norm2_g[layer], mod3, wr2, br, n_experts, n_lat_rows, seq, n_batch)
        tables = _dispatch_tables(route_e[:, :TOP_K], n_experts, MOE_ROW_TILE)
        xs = _dispatch(h2, tables, n_experts)
        act = _moe_up(xs, tables, w_gate_e, w_up_e, layer, MOE_ROW_TILE)
        y_moe = _moe_down(act, tables, w_down_e, layer, MOE_ROW_TILE)
        rows = _combine(x_mix, route_w, tables["dest"], y_moe, mod3, final_g, n_rows, n_lat_rows, seq, n_batch, last)
        src = (rows, rows, n_lat_rows)

    return rows.reshape(n_batch, seq, d)
```

```python
import functools

import jax
import jax.numpy as jnp
from jax import lax
from jax.experimental import pallas as pl
from jax.experimental.pallas import tpu as pltpu

F32 = jnp.float32
BF16 = jnp.bfloat16

N_MOD = 6
EPS = 1e-6
N_DIRS = 2
MLSTM_HEADS = 4
MLSTM_CHUNK = 128
GMLP_HEADS = 4
GMLP_CHUNK = 128
GRID_W = 64
N_EXPERT_GROUPS = 4
TOP_K = 2

V7X_LANES = 128
V7X_SUBLANES = 8
V7X_VMEM_BYTES = 64 * 1024 * 1024
VMEM_LIMIT = V7X_VMEM_BYTES - 8 * 1024 * 1024

ROW_TILE = 256
MM_ROW_TILE = 512
MOE_ROW_TILE = 256
DISPATCH_ROWS = 512
WOUT_W_CHUNKS = 8
MOE_UP_SPLITS = 2
TILE_DMA_PRIORITY = 1
TILE_LOOKAHEAD = 2
N_TILE_SLOTS = TILE_LOOKAHEAD + 1
COMBINE_ROWS = 2 * V7X_SUBLANES
ROW_DMA_UNROLL = 8


def _params(n_axes):
    return pltpu.CompilerParams(
        dimension_semantics=("arbitrary",) * n_axes, vmem_limit_bytes=VMEM_LIMIT)


def _largest_divisor(n, candidates):
    for c in candidates:
        if n % c == 0:
            return c
    raise ValueError(f"no tile in {candidates} divides {n}")


def _mod_row(i, tile, n_lat_rows, seq, n_batch):
    return jnp.where(i * tile < n_lat_rows, (i * tile) // seq, n_batch)


def _mod_kernel(a_ref, w_ref, b_ref, o_ref):
    a = a_ref[...]
    a = a * jax.nn.sigmoid(a)
    acc = jnp.dot(a.astype(BF16), w_ref[...].astype(BF16), preferred_element_type=F32)
    o_ref[...] = acc + b_ref[...]


def _mod_table(a, w_mod, b_mod):
    depth, d, n = w_mod.shape
    tn = _largest_divisor(n, (1024, 512, 256, 128))
    rows = a.shape[0]
    return pl.pallas_call(
        _mod_kernel,
        grid=(depth, n // tn),
        in_specs=[
            pl.BlockSpec((rows, d), lambda l, j: (0, 0)),
            pl.BlockSpec((None, d, tn), lambda l, j: (l, 0, j)),
            pl.BlockSpec((None, 1, tn), lambda l, j: (l, 0, j)),
        ],
        out_specs=pl.BlockSpec((None, rows, tn), lambda l, j: (l, 0, j)),
        out_shape=jax.ShapeDtypeStruct((depth, rows, n), F32),
        compiler_params=_params(2),
        name="mod_table",
    )(a, w_mod, b_mod.reshape(depth, 1, n))


def _norm_modulate(x, g, shift, scale):
    y = x * lax.rsqrt(jnp.mean(x * x, axis=-1, keepdims=True) + EPS)
    return (y * g) * (1.0 + scale) + shift


def _pack_halves(x):
    half = x.shape[1] // 2
    return pltpu.pack_elementwise([x[:, :half], x[:, half:]], packed_dtype=BF16)


def _unpack_halves(p):
    return tuple(pltpu.unpack_elementwise(p, index=k, packed_dtype=BF16, unpacked_dtype=F32) for k in range(2))


def _zeros_like_ref(ref):
    if ref.dtype == jnp.uint32:
        return _pack_halves(jnp.zeros(ref.shape[:-1] + (2 * ref.shape[-1],), F32))
    return jnp.zeros(ref.shape, ref.dtype)


def _row_specs(src, tm, tn, n_lat_rows, col):
    lat, ctx, ctx_row0 = src
    n_lat_tiles = n_lat_rows // tm
    assert n_lat_rows % tm == 0 and ctx_row0 % tm == 0
    return [
        pl.BlockSpec((tm, tn), lambda *g: (jnp.minimum(g[-1], n_lat_tiles - 1), col(*g))),
        pl.BlockSpec((tm, tn), lambda *g: (ctx_row0 // tm + jnp.maximum(g[-1] - n_lat_tiles, 0), col(*g))),
    ]


def _normmod_kernel(xl_ref, xc_ref, g_ref, sh_ref, sc_ref, wg_ref, wgt_ref, o_ref, gate_ref, gate_t_ref, *,
                    n_lat_tiles):
    x = jnp.where(pl.program_id(0) < n_lat_tiles, xl_ref[...], xc_ref[...])
    h = _norm_modulate(x, g_ref[...], sh_ref[...], sc_ref[...]).astype(o_ref.dtype)
    o_ref[...] = h
    gate_ref[...] = jnp.dot(h, wg_ref[...].astype(BF16), preferred_element_type=F32)
    gate_t_ref[...] = lax.dot_general(wgt_ref[...].astype(BF16), h, (((1,), (1,)), ((), ())),
                                      preferred_element_type=F32)


def _normmod(src, n_rows, g, mod3, shift_col, w_gate, w_gate_t, n_lat_rows, seq, n_batch):
    d = g.shape[0]
    lanes = w_gate.shape[1]
    tm = ROW_TILE
    mrow = functools.partial(_mod_row, tile=tm, n_lat_rows=n_lat_rows, seq=seq, n_batch=n_batch)
    return pl.pallas_call(
        functools.partial(_normmod_kernel, n_lat_tiles=n_lat_rows // tm),
        grid=(n_rows // tm,),
        in_specs=_row_specs(src, tm, d, n_lat_rows, lambda i: 0) + [
            pl.BlockSpec((1, d), lambda i: (0, 0)),
            pl.BlockSpec((None, 1, d), lambda i: (mrow(i), 0, shift_col)),
            pl.BlockSpec((None, 1, d), lambda i: (mrow(i), 0, shift_col + 1)),
            pl.BlockSpec((d, lanes), lambda i: (0, 0)),
            pl.BlockSpec((lanes, d), lambda i: (0, 0)),
        ],
        out_specs=[
            pl.BlockSpec((tm, d), lambda i: (i, 0)),
            pl.BlockSpec((tm, lanes), lambda i: (i, 0)),
            pl.BlockSpec((lanes, tm), lambda i: (0, i)),
        ],
        out_shape=[
            jax.ShapeDtypeStruct((n_rows, d), BF16),
            jax.ShapeDtypeStruct((n_rows, lanes), F32),
            jax.ShapeDtypeStruct((lanes, n_rows), F32),
        ],
        compiler_params=_params(1),
        name="norm_modulate",
    )(src[0], src[1], g.reshape(1, d), mod3, mod3, w_gate, w_gate_t)


def _mm_wt_kernel(a_ref, w_ref, o_ref, wb_scr):
    @pl.when(pl.program_id(1) == 0)
    def _():
        wb_scr[...] = w_ref[...].astype(BF16)

    acc = lax.dot_general(a_ref[...], wb_scr[...], (((1,), (1,)), ((), ())), preferred_element_type=F32)
    o_ref[...] = acc.astype(o_ref.dtype)


def _matmul_wt(a, wt, row0, n_cols, out_dtype):
    m, k = a.shape
    tm = _largest_divisor(m, (MM_ROW_TILE, 256, 128))
    tn = _largest_divisor(n_cols, (1024, 512, 256, 128))
    return pl.pallas_call(
        _mm_wt_kernel,
        grid=(n_cols // tn, m // tm),
        in_specs=[
            pl.BlockSpec((tm, k), lambda j, i: (i, 0)),
            pl.BlockSpec((pl.Element(tn), pl.Element(k)),
                         lambda j, i: (pl.multiple_of(row0 + j * tn, V7X_SUBLANES), 0)),
        ],
        out_specs=pl.BlockSpec((tm, tn), lambda j, i: (i, j)),
        out_shape=jax.ShapeDtypeStruct((m, n_cols), out_dtype),
        scratch_shapes=[pltpu.VMEM((tn, k), BF16)],
        compiler_params=_params(2),
        name="projection_wt",
    )(a, wt)


def _log_sigmoid(x):
    return jnp.minimum(x, 0.0) - jnp.log1p(jnp.exp(-jnp.abs(x)))


def _mlstm_chain(q, k, v, i_col, f_col, i_row, f_row, before, after, c_ref, m_ref, o_ref, cols):
    chunk, dk = q.shape
    dv = v.shape[1]
    lf_col = _log_sigmoid(f_col)
    lf_row = _log_sigmoid(f_row)
    b_col = jnp.sum(jnp.where(before, lf_row, 0.0), axis=1, keepdims=True)
    b_row = jnp.sum(jnp.where(after, lf_col, 0.0), axis=0, keepdims=True)
    b_last = jnp.sum(lf_row, axis=1, keepdims=True)

    m_prev = m_ref[0:1, 0:1]
    dmat = jnp.where(before, b_col - b_row + i_row, -jnp.inf)
    m_inter = b_col + m_prev
    m_t = jnp.maximum(jnp.max(dmat, axis=1, keepdims=True), m_inter)
    scale = dk ** -0.5
    qk = lax.dot_general(q, k, (((1,), (1,)), ((), ())), preferred_element_type=F32)
    s = (qk * (scale * jnp.exp(dmat - m_t))).astype(BF16)
    v_aug = jnp.concatenate([v, jnp.ones((chunk, V7X_LANES), BF16)], axis=1)
    c_prev = c_ref[...]
    inter = scale * jnp.exp(m_inter - m_t)
    num = (jnp.dot(s, v_aug, preferred_element_type=F32)
           + inter * jnp.dot(q, c_prev.astype(BF16), preferred_element_type=F32))
    den = num[:, dv:dv + 1]
    o_ref[:, cols] = (num[:, :dv] / jnp.maximum(jnp.abs(den), jnp.exp(-m_t))).astype(o_ref.dtype)

    g_end = b_last - b_col + i_col
    m_new = jnp.maximum(b_last + m_prev, jnp.max(g_end, axis=0, keepdims=True))
    decay = jnp.exp(b_last + m_prev - m_new)
    wv = (jnp.exp(g_end - m_new) * v_aug.astype(F32)).astype(BF16)
    c_ref[...] = decay * c_prev + lax.dot_general(k, wv, (((0,), (0,)), ((), ())), preferred_element_type=F32)
    m_ref[...] = jnp.broadcast_to(m_new, m_ref.shape)


def _mlstm_kernel(qf_ref, kf_ref, vf_ref, gf_ref, gtf_ref, qb_ref, kb_ref, vb_ref, gb_ref, gtb_ref,
                  b_ref, bt_ref, of_ref, ob_ref, c_scr, m_scr):
    nh = MLSTM_HEADS
    chunk = qf_ref.shape[0]
    dk = qf_ref.shape[1] // nh
    dv = vf_ref.shape[1] // nh

    @pl.when(pl.program_id(1) == 0)
    def _():
        c_scr[...] = jnp.zeros_like(c_scr)
        m_scr[...] = jnp.zeros_like(m_scr)

    t_idx = lax.broadcasted_iota(jnp.int32, (chunk, chunk), 0)
    s_idx = lax.broadcasted_iota(jnp.int32, (chunk, chunk), 1)
    dirs = ((qf_ref, kf_ref, vf_ref, gf_ref, gtf_ref, of_ref), (qb_ref, kb_ref, vb_ref, gb_ref, gtb_ref, ob_ref))
    for d, (q_ref, k_ref, v_ref, g_ref, gt_ref, o_ref) in enumerate(dirs):
        before = s_idx <= t_idx if d == 0 else s_idx >= t_idx
        after = s_idx >= t_idx if d == 0 else s_idx <= t_idx
        g = g_ref[...] + b_ref[...]
        gt = gt_ref[...] + bt_ref[...]
        for h in range(nh):
            ci = d * 2 * nh + h
            cf = ci + nh
            _mlstm_chain(
                q_ref[:, h * dk:(h + 1) * dk], k_ref[:, h * dk:(h + 1) * dk], v_ref[:, h * dv:(h + 1) * dv],
                g[:, ci:ci + 1], g[:, cf:cf + 1], gt[ci:ci + 1, :], gt[cf:cf + 1, :], before, after,
                c_scr.at[d * nh + h], m_scr.at[d * nh + h], o_ref, slice(h * dv, (h + 1) * dv))


def _mlstm(z, gates, gates_t, bias, bias_t, n_batch, seq, ctx_len, width):
    r = z.shape[0]
    nh = MLSTM_HEADS
    dv = width // nh
    dk = dv // 2
    qk_w = nh * dk
    chunk = MLSTM_CHUNK
    n_ctx = ctx_len // chunk
    n_lat = seq // chunk
    n_gate_rows = bias_t.shape[0]

    def tile(d, b, step):
        lat = step - n_ctx
        ctx_tile = n_batch * n_lat + b * n_ctx + (step if d == 0 else n_ctx - 1 - step)
        lat_tile = b * n_lat + (lat if d == 0 else n_lat - 1 - lat)
        return jnp.where(step < n_ctx, ctx_tile, lat_tile)

    def dir_specs(d):
        return [
            pl.BlockSpec((chunk, qk_w), lambda b, s: (tile(d, b, s), 0)),
            pl.BlockSpec((chunk, qk_w), lambda b, s: (tile(d, b, s), 1)),
            pl.BlockSpec((chunk, width), lambda b, s: (tile(d, b, s), 1)),
            pl.BlockSpec((chunk, gates.shape[1]), lambda b, s: (tile(d, b, s), 0)),
            pl.BlockSpec((n_gate_rows, chunk), lambda b, s: (0, tile(d, b, s))),
        ]

    return pl.pallas_call(
        _mlstm_kernel,
        grid=(n_batch, n_ctx + n_lat),
        in_specs=dir_specs(0) + dir_specs(1) + [
            pl.BlockSpec(bias.shape, lambda b, s: (0, 0)),
            pl.BlockSpec(bias_t.shape, lambda b, s: (0, 0)),
        ],
        out_specs=[
            pl.BlockSpec((chunk, width), lambda b, s: (tile(0, b, s), 0)),
            pl.BlockSpec((chunk, width), lambda b, s: (tile(1, b, s), 0)),
        ],
        out_shape=[jax.ShapeDtypeStruct((r, width), BF16), jax.ShapeDtypeStruct((r, width), BF16)],
        scratch_shapes=[
            pltpu.VMEM((N_DIRS * nh, dk, dv + V7X_LANES), F32),
            pltpu.VMEM((N_DIRS * nh, V7X_SUBLANES, V7X_LANES), F32),
        ],
        compiler_params=_params(2),
        name="mlstm_scan",
    )(z, z, z, gates, gates_t, z, z, z, gates, gates_t, bias, bias_t)


def _mix_kernel(hf_ref, hb_ref, o_ref, u_ref, v_ref, bg_ref, cg_ref, xin_ref, mg_ref, ws_ref, bst_ref, gg_ref,
                cw_ref, y_ref, *, n_lat_rows, ctx_len):
    i = pl.program_id(0)
    tm = y_ref.shape[0]
    wm = o_ref.shape[1]
    gw = u_ref.shape[1]
    dv = wm // MLSTM_HEADS
    hdim = gw // GMLP_HEADS

    for hd in range(MLSTM_HEADS):
        cols = slice(hd * dv, (hd + 1) * dv)
        hh = hf_ref[:, cols].astype(F32) + hb_ref[:, cols].astype(F32)
        hh = hh * lax.rsqrt(jnp.mean(hh * hh, axis=-1, keepdims=True) + EPS)
        out = (hh * mg_ref[:, cols]) * jax.nn.sigmoid(o_ref[:, cols].astype(F32))
        y_ref[:, cols] = out.astype(y_ref.dtype)

    u = jax.nn.gelu(u_ref[...].astype(F32))
    v = jax.nn.gelu(v_ref[...].astype(F32))
    v = (v * lax.rsqrt(jnp.mean(v * v, axis=-1, keepdims=True) + EPS)) * gg_ref[...]
    vb = v.astype(BF16)
    for c in range(tm // GMLP_CHUNK):
        rows = slice(c * GMLP_CHUNK, (c + 1) * GMLP_CHUNK)
        for hd in range(GMLP_HEADS):
            cols = slice(hd * hdim, (hd + 1) * hdim)
            sp = jnp.dot(ws_ref[hd], vb[rows, cols], preferred_element_type=F32) + bst_ref[:, hd:hd + 1]
            y_ref[rows, wm + hd * hdim:wm + (hd + 1) * hdim] = (u[rows, cols] * sp).astype(y_ref.dtype)

    yv = cg_ref[...].astype(F32) * xin_ref[...].astype(F32)
    t = lax.broadcasted_iota(jnp.int32, (tm, 1), 0)
    is_lat = i * tm < n_lat_rows
    pos = jnp.where(is_lat, t % GRID_W, t % ctx_len)
    last = jnp.where(is_lat, GRID_W - 1, ctx_len - 1)
    prev = jnp.where(pos != 0, pltpu.roll(yv, 1, axis=0), 0.0)
    nxt = jnp.where(pos != last, pltpu.roll(yv, tm - 1, axis=0), 0.0)
    conv = cw_ref[0:1, :] * prev + cw_ref[1:2, :] * yv + cw_ref[2:3, :] * nxt
    y_ref[:, wm + gw:] = (bg_ref[...].astype(F32) * conv).astype(y_ref.dtype)


def _mix(hf, hb, z_a, z_b, mg, ws, bst, gg, cw, n_rows, n_lat_rows, ctx_len):
    wm = mg.shape[1]
    gw = gg.shape[1]
    cwid = cw.shape[1]
    tm = ROW_TILE
    assert tm % GRID_W == 0 and tm % ctx_len == 0 and tm % GMLP_CHUNK == 0
    assert gw == cwid
    full = lambda i: (0, 0)
    return pl.pallas_call(
        functools.partial(_mix_kernel, n_lat_rows=n_lat_rows, ctx_len=ctx_len),
        grid=(n_rows // tm,),
        in_specs=[
            pl.BlockSpec((tm, wm), lambda i: (i, 0)),
            pl.BlockSpec((tm, wm), lambda i: (i, 0)),
            pl.BlockSpec((tm, wm), lambda i: (i, 2)),
            pl.BlockSpec((tm, gw), lambda i: (i, 0)),
            pl.BlockSpec((tm, gw), lambda i: (i, 1)),
            pl.BlockSpec((tm, gw), lambda i: (i, 2)),
            pl.BlockSpec((tm, gw), lambda i: (i, 3)),
            pl.BlockSpec((tm, gw), lambda i: (i, 4)),
            pl.BlockSpec(mg.shape, full),
            pl.BlockSpec(ws.shape, lambda i: (0, 0, 0)),
            pl.BlockSpec(bst.shape, full),
            pl.BlockSpec(gg.shape, full),
            pl.BlockSpec(cw.shape, full),
        ],
        out_specs=pl.BlockSpec((tm, wm + gw + cwid), lambda i: (i, 0)),
        out_shape=jax.ShapeDtypeStruct((n_rows, wm + gw + cwid), BF16),
        compiler_params=_params(1),
        name="mixer_epilogue",
    )(hf, hb, z_a, z_b, z_b, z_b, z_b, z_b, mg, ws, bst, gg, cw)


def _wout_kernel(y_ref, xl_ref, xc_ref, g_ref, w_hbm, o_ref, w_stage, wb_scr, wsem, *, n_lat_tiles, layer,
                 n_chunks):
    j = pl.program_id(0)
    i = pl.program_id(1)
    k, tn = wb_scr.shape
    rows = k // n_chunks

    def w_chunk(col_tile, c):
        return pltpu.make_async_copy(
            w_hbm.at[layer, pl.ds(c * rows, rows), pl.ds(pl.multiple_of(col_tile * tn, tn), tn)],
            w_stage.at[pl.ds(c * rows, rows)], wsem.at[0])

    @pl.when(jnp.logical_and(j == 0, i == 0))
    def _():
        for c in range(n_chunks):
            w_chunk(0, c).start()

    @pl.when(i == 0)
    def _():
        for c in range(n_chunks):
            w_chunk(j, c).wait()
        wb_scr[...] = w_stage[...].astype(BF16)

    for c in range(n_chunks):
        @pl.when(jnp.logical_and(i == c, j + 1 < pl.num_programs(0)))
        def _():
            w_chunk(j + 1, c).start()

    x = jnp.where(i < n_lat_tiles, xl_ref[...], xc_ref[...])
    acc = jnp.dot(y_ref[...], wb_scr[...], preferred_element_type=F32)
    o_ref[...] = x + g_ref[...] * acc


def _wout(y, w3, layer, src, mod3, n_lat_rows, seq, n_batch):
    m, k = y.shape
    d = w3.shape[2]
    tm = _largest_divisor(m, (MM_ROW_TILE, 256))
    tn = _largest_divisor(d, (1024, 512, 256, 128))
    n_chunks = max(c for c in (1, 2, 4, WOUT_W_CHUNKS) if c <= m // tm and k % c == 0)
    mrow = functools.partial(_mod_row, tile=tm, n_lat_rows=n_lat_rows, seq=seq, n_batch=n_batch)
    gate_blk0 = (2 * d) // tn
    return pl.pallas_call(
        functools.partial(_wout_kernel, n_lat_tiles=n_lat_rows // tm, layer=layer, n_chunks=n_chunks),
        grid=(d // tn, m // tm),
        in_specs=[
            pl.BlockSpec((tm, k), lambda j, i: (i, 0)),
        ] + _row_specs(src, tm, tn, n_lat_rows, lambda j, i: j) + [
            pl.BlockSpec((None, 1, tn), lambda j, i: (mrow(i), 0, gate_blk0 + j)),
            pl.BlockSpec(memory_space=pl.ANY),
        ],
        out_specs=pl.BlockSpec((tm, tn), lambda j, i: (i, j)),
        out_shape=jax.ShapeDtypeStruct((m, d), F32),
        scratch_shapes=[pltpu.VMEM((k, tn), F32), pltpu.VMEM((k, tn), BF16), pltpu.SemaphoreType.DMA((1,))],
        compiler_params=_params(2),
        name="out_projection",
    )(y, src[0], src[1], mod3, w3)


def _router_kernel(x_ref, g_ref, sh_ref, sc_ref, wr_ref, br_ref, h_ref, w_ref, e_ref, *, n_experts):
    h = _norm_modulate(x_ref[...], g_ref[...], sh_ref[...], sc_ref[...])
    h_ref[...] = _pack_halves(h)
    tm = h.shape[0]
    lanes = w_ref.shape[1]
    h_hi = h.astype(BF16)
    h_lo = (h - h_hi.astype(F32)).astype(BF16)
    prod = jnp.dot(jnp.concatenate([h_hi, h_lo], axis=0), wr_ref[...], preferred_element_type=F32)
    logits = (prod[:tm, :lanes] + (prod[:tm, lanes:] + prod[tm:, :lanes])) + prod[tm:, lanes:]
    scores = jax.nn.sigmoid(logits.T)
    sel = scores + br_ref[...]
    epg = n_experts // N_EXPERT_GROUPS
    sel_c = [sel[e:e + 1, :] for e in range(n_experts)]
    sc_c = [scores[e:e + 1, :] for e in range(n_experts)]

    best_g = None
    for gi in range(N_EXPERT_GROUPS):
        mem = sel_c[gi * epg:(gi + 1) * epg]
        gs = None
        for a in range(epg):
            for b in range(a + 1, epg):
                pair = mem[a] + mem[b]
                gs = pair if gs is None else jnp.maximum(gs, pair)
        if best_g is None:
            best_g, gidx = gs, jnp.zeros_like(gs, dtype=jnp.int32)
        else:
            upd = gs > best_g
            gidx = jnp.where(upd, gi, gidx)
            best_g = jnp.where(upd, gs, best_g)

    in_sel, in_sc = [], []
    for a in range(epg):
        vs, vr = sel_c[a], sc_c[a]
        for gi in range(1, N_EXPERT_GROUPS):
            vs = jnp.where(gidx == gi, sel_c[gi * epg + a], vs)
            vr = jnp.where(gidx == gi, sc_c[gi * epg + a], vr)
        in_sel.append(vs)
        in_sc.append(vr)

    v1, i1, s1 = in_sel[0], jnp.zeros_like(gidx), in_sc[0]
    for a in range(1, epg):
        upd = in_sel[a] > v1
        i1 = jnp.where(upd, a, i1)
        s1 = jnp.where(upd, in_sc[a], s1)
        v1 = jnp.where(upd, in_sel[a], v1)
    v2 = jnp.full_like(v1, -jnp.inf)
    i2 = jnp.zeros_like(gidx)
    s2 = jnp.zeros_like(s1)
    for a in range(epg):
        upd = jnp.logical_and(i1 != a, in_sel[a] > v2)
        i2 = jnp.where(upd, a, i2)
        s2 = jnp.where(upd, in_sc[a], s2)
        v2 = jnp.where(upd, in_sel[a], v2)
    denom = s1 + s2
    sub = lax.broadcasted_iota(jnp.int32, scores.shape, 0)
    w_t = jnp.where(sub == 0, s1 / denom, jnp.where(sub == 1, s2 / denom, 0.0))
    e_t = jnp.where(sub == 0, gidx * epg + i1, jnp.where(sub == 1, gidx * epg + i2, 0))
    w_ref[...] = w_t.T
    e_ref[...] = e_t.astype(F32).T.astype(jnp.int32)


def _router(x_mix, g, mod3, wr2, br, n_experts, n_lat_rows, seq, n_batch):
    r, d = x_mix.shape
    tm = ROW_TILE
    mrow = functools.partial(_mod_row, tile=tm, n_lat_rows=n_lat_rows, seq=seq, n_batch=n_batch)
    lanes = br.shape[0]
    return pl.pallas_call(
        functools.partial(_router_kernel, n_experts=n_experts),
        grid=(r // tm,),
        in_specs=[
            pl.BlockSpec((tm, d), lambda i: (i, 0)),
            pl.BlockSpec((1, d), lambda i: (0, 0)),
            pl.BlockSpec((None, 1, d), lambda i: (mrow(i), 0, 3)),
            pl.BlockSpec((None, 1, d), lambda i: (mrow(i), 0, 4)),
            pl.BlockSpec(wr2.shape, lambda i: (0, 0)),
            pl.BlockSpec(br.shape, lambda i: (0, 0)),
        ],
        out_specs=[
            pl.BlockSpec((tm, d // 2), lambda i: (i, 0)),
            pl.BlockSpec((tm, lanes), lambda i: (i, 0)),
            pl.BlockSpec((tm, lanes), lambda i: (i, 0)),
        ],
        out_shape=[
            jax.ShapeDtypeStruct((r, d // 2), jnp.uint32),
            jax.ShapeDtypeStruct((r, lanes), F32),
            jax.ShapeDtypeStruct((r, lanes), jnp.int32),
        ],
        compiler_params=_params(1),
        name="moe_router",
    )(x_mix, g.reshape(1, d), mod3, mod3, wr2, br)


def _dispatch_tables(e_pair, n_experts, tm):
    flat_e = e_pair.reshape(-1)
    n_pairs = flat_e.shape[0]
    n_rows = n_pairs + n_experts * tm
    onehot = (flat_e[:, None] == jnp.arange(n_experts, dtype=jnp.int32)[None, :]).astype(jnp.int32)
    rank = jnp.sum((jnp.cumsum(onehot, axis=0) - onehot) * onehot, axis=1)
    counts = jnp.sum(onehot, axis=0)
    padded = ((counts + tm - 1) // tm) * tm
    ends = jnp.cumsum(padded)
    starts = ends - padded
    dest = jnp.sum(onehot * starts[None, :], axis=1) + rank
    n_used = (ends[-1] // tm).astype(jnp.int32)
    return dict(dest=dest.astype(jnp.int32), pad_lo=(starts + counts).astype(jnp.int32),
                pad_hi=ends.astype(jnp.int32), seg_tile0=(starts // tm).astype(jnp.int32),
                seg_tiles=(padded // tm).astype(jnp.int32), n_used=n_used.reshape(1), n_rows=n_rows)


def _dispatch_kernel(dest_ref, plo_ref, phi_ref, h_ref, xs_hbm, zrow, sem, zsem, *, n_experts):
    i = pl.program_id(0)
    tm = h_ref.shape[0]

    @pl.when(i == 0)
    def _():
        zrow[...] = _zeros_like_ref(zrow)

        def zero_copy(r):
            return pltpu.make_async_copy(zrow.at[pl.ds(0, 1)], xs_hbm.at[pl.ds(r, 1)], zsem.at[0])

        def start(r, carry):
            zero_copy(r).start()
            return carry

        def wait(r, carry):
            zero_copy(r).wait()
            return carry

        zt = zrow.shape[0]

        def tile_copy(t):
            return pltpu.make_async_copy(zrow, xs_hbm.at[pl.ds(t * zt, zt)], zsem.at[0])

        def start_tile(t, carry):
            tile_copy(t).start()
            return carry

        def wait_tile(t, carry):
            tile_copy(t).wait()
            return carry

        n_tiles = xs_hbm.shape[0] // zt
        first_unused = phi_ref[n_experts - 1] // zt
        for e in range(n_experts):
            lax.fori_loop(plo_ref[e], phi_ref[e], start, 0)
        lax.fori_loop(first_unused, n_tiles, start_tile, 0)
        for e in range(n_experts):
            lax.fori_loop(plo_ref[e], phi_ref[e], wait, 0)
        lax.fori_loop(first_unused, n_tiles, wait_tile, 0)

    def scatter(r, carry):
        for k in range(TOP_K):
            row = dest_ref[(i * tm + r) * TOP_K + k]
            pltpu.make_async_copy(h_ref.at[pl.ds(r, 1)], xs_hbm.at[pl.ds(row, 1)], sem.at[0]).start(priority=k)
        return carry

    lax.fori_loop(0, tm, scatter, 0, unroll=ROW_DMA_UNROLL)
    for k in range(TOP_K):
        pltpu.make_async_copy(h_ref, xs_hbm.at[pl.ds(0, tm)], sem.at[0]).wait()


def _dispatch(h2, tables, n_experts):
    r, d = h2.shape
    tm = _largest_divisor(r, (DISPATCH_ROWS, ROW_TILE))
    return pl.pallas_call(
        functools.partial(_dispatch_kernel, n_experts=n_experts),
        grid_spec=pltpu.PrefetchScalarGridSpec(
            num_scalar_prefetch=3,
            grid=(r // tm,),
            in_specs=[pl.BlockSpec((tm, d), lambda i, de, lo, hi: (i, 0))],
            out_specs=pl.BlockSpec(memory_space=pl.ANY),
            scratch_shapes=[pltpu.VMEM((MOE_ROW_TILE, d), h2.dtype), pltpu.SemaphoreType.DMA((1,)),
                            pltpu.SemaphoreType.DMA((1,))],
        ),
        out_shape=jax.ShapeDtypeStruct((tables["n_rows"], d), h2.dtype),
        compiler_params=_params(1),
        name="moe_dispatch",
    )(tables["dest"], tables["pad_lo"], tables["pad_hi"], h2)


def _segment_loop(first, n_tiles, n_stream, restart, load, compute, store):
    @pl.when(restart)
    def _():
        for a in range(TILE_LOOKAHEAD):
            @pl.when(a < n_stream)
            def _():
                load(a, a).start(priority=TILE_DMA_PRIORITY)

    def body(g, carry):
        slot = g % N_TILE_SLOTS
        load(g, slot).wait()
        ahead = g + TILE_LOOKAHEAD

        @pl.when(ahead < n_stream)
        def _():
            load(ahead, ahead % N_TILE_SLOTS).start(priority=TILE_DMA_PRIORITY)

        out_slot = g % 2

        @pl.when(g - first >= 2)
        def _():
            store(g - 2, out_slot).wait()

        compute(slot, out_slot)
        store(g, out_slot).start(priority=TILE_DMA_PRIORITY)
        return carry

    lax.fori_loop(first, first + n_tiles, body, 0)
    last = first + n_tiles - 1

    @pl.when(n_tiles >= 2)
    def _():
        store(last - 1, (last - 1) % 2).wait()

    @pl.when(n_tiles >= 1)
    def _():
        store(last, last % 2).wait()


def _zero_tail_tiles(first_tile, n_tiles, zero_src, tile_copy):
    zero_src[...] = _zeros_like_ref(zero_src)

    def start(t, carry):
        tile_copy(t).start()
        return carry

    def wait(t, carry):
        tile_copy(t).wait()
        return carry

    lax.fori_loop(first_tile, n_tiles, start, 0)
    lax.fori_loop(first_tile, n_tiles, wait, 0)


def _moe_up_kernel(t0_ref, nt_ref, nu_ref, wg_ref, wu_ref, xs_hbm, a_hbm, wgb, wub, xbuf, obuf, xsem, osem):
    j = pl.program_id(0)
    e = pl.program_id(1)
    tm = xbuf.shape[1]

    def load(g, slot):
        return pltpu.make_async_copy(xs_hbm.at[pl.ds(g * tm, tm)], xbuf.at[slot], xsem.at[slot])

    def store(g, slot):
        return pltpu.make_async_copy(obuf.at[slot], a_hbm.at[j, pl.ds(g * tm, tm)], osem.at[slot])

    def compute(slot, out_slot):
        half = xbuf.shape[2]
        x_lo, x_hi = (h.astype(BF16) for h in _unpack_halves(xbuf[slot]))

        def project(w):
            return (jnp.dot(x_lo, w[0:half, :], preferred_element_type=F32)
                    + jnp.dot(x_hi, w[half:2 * half, :], preferred_element_type=F32))

        gate = project(wgb)
        up = project(wub)
        obuf[out_slot] = ((gate * jax.nn.sigmoid(gate)) * up).astype(obuf.dtype)

    @pl.when(nt_ref[e] > 0)
    def _():
        wgb[...] = wg_ref[...].astype(BF16)
        wub[...] = wu_ref[...].astype(BF16)

    _segment_loop(t0_ref[e], nt_ref[e], nu_ref[0], e == 0, load, compute, store)

    @pl.when(e == pl.num_programs(1) - 1)
    def _():
        _zero_tail_tiles(
            nu_ref[0], a_hbm.shape[1] // tm, obuf.at[0],
            lambda t: pltpu.make_async_copy(obuf.at[0], a_hbm.at[j, pl.ds(t * tm, tm)], osem.at[0]))


def _moe_up(xs, tables, wg4, wu4, layer, tm):
    n_rows = xs.shape[0]
    n_experts, d, f = wg4.shape[1:]
    fs = f // MOE_UP_SPLITS
    w_spec = pl.BlockSpec((None, None, d, fs), lambda j, e, t0, nt, nu: (layer, e, 0, j))
    return pl.pallas_call(
        _moe_up_kernel,
        grid_spec=pltpu.PrefetchScalarGridSpec(
            num_scalar_prefetch=3,
            grid=(MOE_UP_SPLITS, n_experts),
            in_specs=[w_spec, w_spec, pl.BlockSpec(memory_space=pl.ANY)],
            out_specs=pl.BlockSpec(memory_space=pl.ANY),
            scratch_shapes=[
                pltpu.VMEM((d, fs), BF16), pltpu.VMEM((d, fs), BF16),
                pltpu.VMEM((N_TILE_SLOTS, tm, xs.shape[1]), xs.dtype), pltpu.VMEM((2, tm, fs), BF16),
                pltpu.SemaphoreType.DMA((N_TILE_SLOTS,)), pltpu.SemaphoreType.DMA((2,)),
            ],
        ),
        out_shape=jax.ShapeDtypeStruct((MOE_UP_SPLITS, n_rows, fs), BF16),
        compiler_params=_params(2),
        name="moe_up",
    )(tables["seg_tile0"], tables["seg_tiles"], tables["n_used"], wg4, wu4, xs)


def _moe_down_kernel(t0_ref, nt_ref, nu_ref, wd_ref, a_hbm, y_hbm, wdb, abuf, ybuf, asem, ysem):
    e = pl.program_id(0)
    n_splits, tm, fs = abuf.shape[1:]

    def load(g, slot):
        return pltpu.make_async_copy(a_hbm.at[:, pl.ds(g * tm, tm)], abuf.at[slot], asem.at[slot])

    def store(g, slot):
        return pltpu.make_async_copy(ybuf.at[slot], y_hbm.at[pl.ds(g * tm, tm)], ysem.at[slot])

    def compute(slot, out_slot):
        acc = jnp.dot(abuf[slot, 0], wdb[0:fs, :], preferred_element_type=F32)
        for s in range(1, n_splits):
            acc = acc + jnp.dot(abuf[slot, s], wdb[s * fs:(s + 1) * fs, :], preferred_element_type=F32)
        ybuf[out_slot] = _pack_halves(acc)

    @pl.when(nt_ref[e] > 0)
    def _():
        wdb[...] = wd_ref[...].astype(BF16)

    _segment_loop(t0_ref[e], nt_ref[e], nu_ref[0], e == 0, load, compute, store)

    @pl.when(e == pl.num_programs(0) - 1)
    def _():
        _zero_tail_tiles(
            nu_ref[0], y_hbm.shape[0] // tm, ybuf.at[0],
            lambda t: pltpu.make_async_copy(ybuf.at[0], y_hbm.at[pl.ds(t * tm, tm)], ysem.at[0]))


def _moe_down(a, tables, wd4, layer, tm):
    n_splits, n_rows, fs = a.shape
    n_experts, f, d = wd4.shape[1:]
    return pl.pallas_call(
        _moe_down_kernel,
        grid_spec=pltpu.PrefetchScalarGridSpec(
            num_scalar_prefetch=3,
            grid=(n_experts,),
            in_specs=[
                pl.BlockSpec((None, None, f, d), lambda e, t0, nt, nu: (layer, e, 0, 0)),
                pl.BlockSpec(memory_space=pl.ANY),
            ],
            out_specs=pl.BlockSpec(memory_space=pl.ANY),
            scratch_shapes=[
                pltpu.VMEM((f, d), BF16),
                pltpu.VMEM((N_TILE_SLOTS, n_splits, tm, fs), BF16), pltpu.VMEM((2, tm, d // 2), jnp.uint32),
                pltpu.SemaphoreType.DMA((N_TILE_SLOTS,)), pltpu.SemaphoreType.DMA((2,)),
            ],
        ),
        out_shape=jax.ShapeDtypeStruct((n_rows, d // 2), jnp.uint32),
        compiler_params=_params(1),
        name="moe_down",
    )(tables["seg_tile0"], tables["seg_tiles"], tables["n_used"], wd4, a)


def _combine_kernel(pos_ref, x_ref, w_ref, g_ref, fg_ref, y_hbm, o_ref, ybuf, sem, *, final_norm):
    i = pl.program_id(0)
    n = pl.num_programs(0)
    tm = x_ref.shape[0]

    def gather(tile, slot):
        def body(r, carry):
            for k in range(TOP_K):
                row = pos_ref[(tile * tm + r) * TOP_K + k]
                pltpu.make_async_copy(
                    y_hbm.at[pl.ds(row, 1)], ybuf.at[slot, k, pl.ds(r, 1)], sem.at[slot]).start(priority=k)
            return carry
        lax.fori_loop(0, tm, body, 0, unroll=ROW_DMA_UNROLL)

    @pl.when(i == 0)
    def _():
        gather(0, 0)

    @pl.when(i + 1 < n)
    def _():
        gather(i + 1, (i + 1) % 2)

    slot = i % 2
    for k in range(TOP_K):
        pltpu.make_async_copy(y_hbm.at[pl.ds(0, tm)], ybuf.at[slot, k], sem.at[slot]).wait()

    def rows_block(c, carry):
        rows = pl.ds(pl.multiple_of(c * COMBINE_ROWS, COMBINE_ROWS), COMBINE_ROWS)
        w = w_ref[rows, :]
        y_first = jnp.concatenate(_unpack_halves(ybuf[slot, 0, rows, :]), axis=1)
        y_second = jnp.concatenate(_unpack_halves(ybuf[slot, 1, rows, :]), axis=1)
        moe = w[:, 0:1] * y_first + w[:, 1:2] * y_second
        out = x_ref[rows, :] + g_ref[...] * moe
        if final_norm:
            out = (out * lax.rsqrt(jnp.mean(out * out, axis=-1, keepdims=True) + EPS)) * fg_ref[...]
        o_ref[rows, :] = out
        return carry

    lax.fori_loop(0, tm // COMBINE_ROWS, rows_block, 0, unroll=4)


def _combine(x_mix, route_w, pos, y, mod3, final_g, n_rows, n_lat_rows, seq, n_batch, final_norm):
    d = x_mix.shape[1]
    tm = ROW_TILE
    lanes = route_w.shape[1]
    mrow = functools.partial(_mod_row, tile=tm, n_lat_rows=n_lat_rows, seq=seq, n_batch=n_batch)
    return pl.pallas_call(
        functools.partial(_combine_kernel, final_norm=final_norm),
        grid_spec=pltpu.PrefetchScalarGridSpec(
            num_scalar_prefetch=1,
            grid=(n_rows // tm,),
            in_specs=[
                pl.BlockSpec((tm, d), lambda i, p: (i, 0)),
                pl.BlockSpec((tm, lanes), lambda i, p: (i, 0)),
                pl.BlockSpec((None, 1, d), lambda i, p: (mrow(i), 0, 5)),
                pl.BlockSpec((1, d), lambda i, p: (0, 0)),
                pl.BlockSpec(memory_space=pl.ANY),
            ],
            out_specs=pl.BlockSpec((tm, d), lambda i, p: (i, 0)),
            scratch_shapes=[pltpu.VMEM((2, TOP_K, tm, y.shape[1]), y.dtype), pltpu.SemaphoreType.DMA((2,))],
        ),
        out_shape=jax.ShapeDtypeStruct((n_rows, d), F32),
        compiler_params=_params(1),
        name="moe_combine",
    )(pos, x_mix, route_w, mod3, final_g.reshape(1, d), y)


def kernel(x, c, ctx, c_ctx, w_mod, b_mod, norm1_g, norm2_g, w_in, b_gates, mlstm_norm_g, gmlp_ws, gmlp_bs,
           gmlp_norm_g, conv_w, w_out, w_router, b_router, w_gate_e, w_up_e, w_down_e, final_g):
    n_batch, seq, d = x.shape
    ctx_len = ctx.shape[1]
    depth = w_mod.shape[0]
    wm = mlstm_norm_g.shape[1]
    n_experts = w_router.shape[1]
    n_lat_rows = n_batch * seq
    n_all_rows = n_lat_rows + n_batch * ctx_len
    n_gate_cols = N_DIRS * 2 * MLSTM_HEADS
    gate_col0 = 3 * wm
    assert seq % ROW_TILE == 0 and seq % MM_ROW_TILE == 0 and (n_batch * ctx_len) % ROW_TILE == 0
    assert n_batch + 1 <= V7X_SUBLANES and n_experts <= V7X_LANES and n_gate_cols <= V7X_LANES

    src = (x.reshape(n_lat_rows, d), ctx.reshape(n_batch * ctx_len, d), 0)
    n_in_cols = w_in.shape[2]
    w_in_t = jnp.swapaxes(w_in, 1, 2).reshape(depth * n_in_cols, d)

    cond = jnp.zeros((V7X_SUBLANES, d), F32).at[:n_batch].set(c).at[n_batch].set(c_ctx)
    mods = _mod_table(cond, w_mod, b_mod)

    wr = jnp.zeros((d, V7X_LANES), F32).at[:, :n_experts].set(w_router)
    wr_hi = wr.astype(BF16)
    wr2 = jnp.concatenate([wr_hi, (wr - wr_hi.astype(F32)).astype(BF16)], axis=1)
    br = jnp.zeros((V7X_LANES, 1), F32).at[:n_experts, 0].set(b_router)

    for layer in range(depth):
        last = layer + 1 == depth
        mod3 = mods[layer].reshape(V7X_SUBLANES, 1, N_MOD * d)
        w_row0 = layer * n_in_cols
        w_gate_t = jnp.zeros((V7X_LANES, d), F32).at[:n_gate_cols].set(
            w_in_t[w_row0 + gate_col0:w_row0 + gate_col0 + n_gate_cols])
        bias = jnp.zeros((1, V7X_LANES), F32).at[0, :n_gate_cols].set(b_gates[layer].reshape(-1))
        bias_t = jnp.broadcast_to(b_gates[layer].reshape(-1, 1), (n_gate_cols, MLSTM_CHUNK))

        h1, gates, gates_t = _normmod(src, n_all_rows, norm1_g[layer], mod3, 0, w_gate_t.T, w_gate_t,
                                      n_lat_rows, seq, n_batch)
        z_a = _matmul_wt(h1, w_in_t, w_row0, gate_col0, BF16)
        z_b = _matmul_wt(h1, w_in_t, w_row0 + gate_col0 + n_gate_cols, n_in_cols - gate_col0 - n_gate_cols,
                         BF16)
        hf, hb = _mlstm(z_a, gates, gates_t, bias, bias_t, n_batch, seq, ctx_len, wm)

        n_rows = n_lat_rows if last else n_all_rows
        y = _mix(hf, hb, z_a, z_b, mlstm_norm_g[layer].reshape(1, wm), gmlp_ws[layer].astype(BF16),
                 gmlp_bs[layer].T, gmlp_norm_g[layer].reshape(1, -1), conv_w[layer], n_rows, n_lat_rows, ctx_len)
        x_mix = _wout(y, w_out, layer, src, mod3, n_lat_rows, seq, n_batch)

        h2, route_w, route_e = _router(x_mix, norm2_g[layer], mod3, wr2, br, n_experts, n_lat_rows, seq, n_batch)
        tables = _dispatch_tables(route_e[:, :TOP_K], n_experts, MOE_ROW_TILE)
        xs = _dispatch(h2, tables, n_experts)
        act = _moe_up(xs, tables, w_gate_e, w_up_e, layer, MOE_ROW_TILE)
        y_moe = _moe_down(act, tables, w_down_e, layer, MOE_ROW_TILE)
        rows = _combine(x_mix, route_w, tables["dest"], y_moe, mod3, final_g, n_rows, n_lat_rows, seq, n_batch, last)
        src = (rows, rows, n_lat_rows)

    return rows.reshape(n_batch, seq, d)
```
